```python
import math
import jax, jax.numpy as jnp
from jax import lax
import numpy as np

D_MODEL = 1024
BATCH = 16
SEQ = 256
DEPTH = 4
DEC_BATCH = 8
DEC_SEQ = 2048
PAST_LEN = 256

GRID_W = 64
N_HEADS = 8
N_KV_HEADS = 2
HEAD_DIM = 64
ATTN_WIDTH = N_HEADS * HEAD_DIM
KV_WIDTH = N_KV_HEADS * HEAD_DIM
POOL_WIDTH = D_MODEL - ATTN_WIDTH
POOL_WINDOWS = (2, 4, 8, 16)
N_POOL_GROUPS = len(POOL_WINDOWS)
POOL_GROUP = POOL_WIDTH // N_POOL_GROUPS
FOURIER_WIDTH = D_MODEL // 4
N_FOURIER_GROUPS = 4
FOURIER_GROUP = FOURIER_WIDTH // N_FOURIER_GROUPS
CONV_WIDTH = D_MODEL - FOURIER_WIDTH
CONV_K = 3
D_FF = 2816
FFN_CONV_K = 3
ROPE_THETA = 10000.0
Q_BLOCK = 128
LN_EPS = 1e-6
RMS_EPS = 1e-6
N_EVEN = (DEPTH + 1) // 2
N_ODD = DEPTH // 2
N_ATTN_LAYERS = N_EVEN
IN_EVEN = ATTN_WIDTH + 2 * KV_WIDTH + POOL_WIDTH
IN_ODD = 3 * CONV_WIDTH + FOURIER_WIDTH
DEEPNORM_ALPHA = (2 * DEPTH) ** 0.25
DEEPNORM_BETA = (8 * DEPTH) ** -0.25

kernel_name = 'hybrid_diffusion_prefix_trunk_step'


def layer_norm(x, g=None, b=None):
    xf = x.astype(jnp.float32)
    mu = jnp.mean(xf, -1, keepdims=True)
    var = jnp.mean(jnp.square(xf - mu), -1, keepdims=True)
    y = (xf - mu) * lax.rsqrt(var + LN_EPS)
    if g is not None:
        y = y * g.astype(jnp.float32) + b.astype(jnp.float32)
    return y.astype(x.dtype)


def rms_norm(x, g):
    xf = x.astype(jnp.float32)
    y = xf * lax.rsqrt(jnp.mean(xf * xf, -1, keepdims=True) + RMS_EPS) * g.astype(jnp.float32)
    return y.astype(x.dtype)


def axial_rope_tables(L):
    rows = L // GRID_W
    row = jnp.repeat(jnp.arange(rows, dtype=jnp.float32), GRID_W)
    col = jnp.tile(jnp.arange(GRID_W, dtype=jnp.float32), rows)
    n_freq = HEAD_DIM // 4
    inv = 1.0 / (ROPE_THETA ** (jnp.arange(n_freq, dtype=jnp.float32) / n_freq))
    ang = jnp.concatenate([row[:, None] * inv, col[:, None] * inv], -1)
    return jnp.cos(ang), jnp.sin(ang)


def apply_rope(x, cos, sin):
    xp = x.astype(jnp.float32).reshape(x.shape[:-1] + (HEAD_DIM // 2, 2))
    x0, x1 = xp[..., 0], xp[..., 1]
    c = cos[None, :, None, :]
    s = sin[None, :, None, :]
    out = jnp.stack([x0 * c - x1 * s, x0 * s + x1 * c], -1)
    return out.reshape(x.shape).astype(x.dtype)


def block_attention(q, k, v):
    B, Lq = q.shape[0], q.shape[1]
    nb = Lq // Q_BLOCK
    G = N_HEADS // N_KV_HEADS
    qb = q.reshape(B, nb, Q_BLOCK, N_KV_HEADS, G, HEAD_DIM).transpose(1, 0, 2, 3, 4, 5)
    scale = HEAD_DIM ** -0.5

    def one_block(qblk):
        s = jnp.einsum('bqkgd,bskd->bkgqs', qblk, k).astype(jnp.float32) * scale
        p = jax.nn.softmax(s, axis=-1).astype(v.dtype)
        return jnp.einsum('bkgqs,bskd->bqkgd', p, v)

    o = lax.map(one_block, qb)
    return o.transpose(1, 0, 2, 3, 4, 5).reshape(B, Lq, ATTN_WIDTH)


def centred_mean_pool(u, w):
    B, L, C = u.shape
    cs = jnp.concatenate([jnp.zeros((B, 1, C), jnp.float32), jnp.cumsum(u.astype(jnp.float32), axis=1)], 1)
    t = jnp.arange(L)
    lo = jnp.clip(t - w // 2, 0, L)
    hi = jnp.clip(t + w // 2, 0, L)
    s = cs[:, hi] - cs[:, lo]
    cnt = (hi - lo).astype(jnp.float32)
    return (s / cnt[None, :, None]).astype(u.dtype)


def pool_mixer(p, w_pool, pool_scale):
    B, L, _ = p.shape
    grp = p.reshape(B, L, N_POOL_GROUPS, POOL_GROUP)
    pooled = jnp.stack([centred_mean_pool(grp[:, :, i], w) for i, w in enumerate(POOL_WINDOWS)], 2) - grp
    mixed = jnp.einsum('blgc,gcd->blgd', pooled, w_pool)
    return mixed.reshape(B, L, POOL_WIDTH) * pool_scale


def depthwise_conv(x, w, b):
    K = w.shape[0]
    pad = K // 2
    L = x.shape[1]
    xp = jnp.pad(x, ((0, 0), (pad, K - 1 - pad), (0, 0)))
    y = b
    for j in range(K):
        y = y + xp[:, j:j + L] * w[j]
    return y


def fourier_mix(f):
    B, L, _ = f.shape
    fg = f.reshape(B, L, N_FOURIER_GROUPS, FOURIER_GROUP).astype(jnp.float32)
    out = jnp.fft.fft2(fg, axes=(1, 3), norm='ortho').real
    return out.reshape(B, L, FOURIER_WIDTH).astype(f.dtype)


def even_mixer(u, w_in, q_g, k_g, w_pool, pool_scale, rope, ctx_k, ctx_v):
    B, L, _ = u.shape
    proj = u @ w_in
    o1 = ATTN_WIDTH
    o2 = o1 + KV_WIDTH
    o3 = o2 + KV_WIDTH
    q = rms_norm(proj[..., :o1].reshape(B, L, N_HEADS, HEAD_DIM), q_g)
    k = rms_norm(proj[..., o1:o2].reshape(B, L, N_KV_HEADS, HEAD_DIM), k_g)
    v = proj[..., o2:o3].reshape(B, L, N_KV_HEADS, HEAD_DIM)
    p = proj[..., o3:]
    if rope is None:
        attn = block_attention(q, k, v)
    else:
        cos, sin = rope
        qr = apply_rope(q, cos, sin)
        kr = apply_rope(k, cos, sin)
        attn = block_attention(qr, jnp.concatenate([kr, ctx_k], 1), jnp.concatenate([v, ctx_v], 1))
    pool = pool_mixer(p, w_pool, pool_scale)
    return jnp.concatenate([attn, pool], -1), k, v


def odd_mixer(u, w_in, conv_w, conv_b):
    proj = u @ w_in
    h = proj[..., :CONV_WIDTH]
    bg = proj[..., CONV_WIDTH:2 * CONV_WIDTH]
    cg = proj[..., 2 * CONV_WIDTH:3 * CONV_WIDTH]
    f = proj[..., 3 * CONV_WIDTH:]
    conv_out = bg * depthwise_conv(cg * h, conv_w, conv_b)
    return jnp.concatenate([conv_out, fourier_mix(f)], -1)


def conv_ffn(u, w_up, cw, cb, w_down):
    a, g = jnp.split(u @ w_up, 2, axis=-1)
    return (jax.nn.silu(depthwise_conv(a, cw, cb)) * g) @ w_down


def run_trunk(x, cond, ctx_k, ctx_v, w_ada, b_ada, w_in_even, q_norm_g, k_norm_g, w_pool, pool_scale,
              w_in_odd, conv_w, conv_b, w_out, w_up, ffn_conv_w, ffn_conv_b, w_down, ln_g, ln_b):
    is_latent = ctx_k is not None
    rope = axial_rope_tables(x.shape[1]) if is_latent else None
    new_k, new_v = [], []
    for l in range(DEPTH):
        ada = jax.nn.silu(cond) @ w_ada[l] + b_ada[l]
        sh1, sc1, g1, sh2, sc2, g2 = jnp.split(ada[:, None, :], 6, axis=-1)
        u = layer_norm(x) * (1 + sc1) + sh1
        if l % 2 == 0:
            e = l // 2
            m, k, v = even_mixer(u, w_in_even[e], q_norm_g[e], k_norm_g[e], w_pool[e], pool_scale[e], rope,
                                 ctx_k[:, e] if is_latent else None, ctx_v[:, e] if is_latent else None)
            if not is_latent:
                new_k.append(k)
                new_v.append(v)
        else:
            o = l // 2
            m = odd_mixer(u, w_in_odd[o], conv_w[o], conv_b[o])
        x = layer_norm(DEEPNORM_ALPHA * x + g1 * (m @ w_out[l]), ln_g[l, 0], ln_b[l, 0])
        u = layer_norm(x) * (1 + sc2) + sh2
        f = conv_ffn(u, w_up[l], ffn_conv_w[l], ffn_conv_b[l], w_down[l])
        x = layer_norm(DEEPNORM_ALPHA * x + g2 * f, ln_g[l, 1], ln_b[l, 1])
    return x, new_k, new_v


def setup_inputs(seed: int = 0) -> dict:
    key = jax.random.key(seed)
    ks = iter(jax.random.split(key, 32))

    def nrm(shape, scale):
        return jax.random.normal(next(ks), shape, jnp.float32) * scale

    D = D_MODEL
    return {
        'x_prompt': nrm((BATCH, SEQ, D), 1.0),
        'x_sample': nrm((DEC_BATCH, DEC_SEQ, D), 1.0),
        'cache_k': nrm((DEC_BATCH, N_ATTN_LAYERS, PAST_LEN, N_KV_HEADS, HEAD_DIM), 1.0),
        'cache_v': nrm((DEC_BATCH, N_ATTN_LAYERS, PAST_LEN, N_KV_HEADS, HEAD_DIM), 1.0),
        'c': nrm((DEC_BATCH, D), 1.0),
        'c_ctx': nrm((D,), 1.0),
        'w_ada': nrm((DEPTH, D, 6 * D), D ** -0.5),
        'b_ada': nrm((DEPTH, 6 * D), 0.02),
        'w_in_even': nrm((N_EVEN, D, IN_EVEN), D ** -0.5),
        'q_norm_g': 1.0 + nrm((N_EVEN, HEAD_DIM), 0.02),
        'k_norm_g': 1.0 + nrm((N_EVEN, HEAD_DIM), 0.02),
        'w_pool': nrm((N_EVEN, N_POOL_GROUPS, POOL_GROUP, POOL_GROUP), POOL_GROUP ** -0.5),
        'pool_scale': 1.0 + nrm((N_EVEN, POOL_WIDTH), 0.02),
        'w_in_odd': nrm((N_ODD, D, IN_ODD), D ** -0.5),
        'conv_w': nrm((N_ODD, CONV_K, CONV_WIDTH), CONV_K ** -0.5),
        'conv_b': nrm((N_ODD, CONV_WIDTH), 0.02),
        'w_out': nrm((DEPTH, D, D), D ** -0.5 * DEEPNORM_BETA),
        'w_up': nrm((DEPTH, D, 2 * D_FF), D ** -0.5),
        'ffn_conv_w': nrm((DEPTH, FFN_CONV_K, D_FF), FFN_CONV_K ** -0.5),
        'ffn_conv_b': nrm((DEPTH, D_FF), 0.02),
        'w_down': nrm((DEPTH, D_FF, D), D_FF ** -0.5 * DEEPNORM_BETA),
        'ln_g': 1.0 + nrm((DEPTH, 2, D), 0.02),
        'ln_b': nrm((DEPTH, 2, D), 0.02),
    }


def reference(x_prompt, x_sample, cache_k, cache_v, c, c_ctx, w_ada, b_ada, w_in_even, q_norm_g, k_norm_g,
              w_pool, pool_scale, w_in_odd, conv_w, conv_b, w_out, w_up, ffn_conv_w, ffn_conv_b, w_down,
              ln_g, ln_b):
    y_prompt, ks, vs = run_trunk(x_prompt, c_ctx[None, :], None, None, w_ada, b_ada, w_in_even, q_norm_g,
                                 k_norm_g, w_pool, pool_scale, w_in_odd, conv_w, conv_b, w_out, w_up,
                                 ffn_conv_w, ffn_conv_b, w_down, ln_g, ln_b)
    new_cache_k = jnp.stack(ks, 1)
    new_cache_v = jnp.stack(vs, 1)
    y_sample, _, _ = run_trunk(x_sample, c, cache_k, cache_v, w_ada, b_ada, w_in_even, q_norm_g, k_norm_g,
                               w_pool, pool_scale, w_in_odd, conv_w, conv_b, w_out, w_up, ffn_conv_w,
                               ffn_conv_b, w_down, ln_g, ln_b)
    return (y_prompt, y_sample, new_cache_k, new_cache_v)
```

```python
import functools
import math

import jax
import jax.numpy as jnp
from jax import lax
from jax.experimental import pallas as pl
from jax.experimental.pallas import tpu as pltpu

D_MODEL = 1024
DEPTH = 4
GRID_W = 64
N_HEADS = 8
N_KV_HEADS = 2
HEAD_DIM = 64
ATTN_WIDTH = N_HEADS * HEAD_DIM
KV_WIDTH = N_KV_HEADS * HEAD_DIM
POOL_WIDTH = D_MODEL - ATTN_WIDTH
POOL_WINDOWS = (2, 4, 8, 16)
POOL_GROUP = POOL_WIDTH // len(POOL_WINDOWS)
FOURIER_WIDTH = D_MODEL // 4
N_FOURIER_GROUPS = 4
FOURIER_GROUP = FOURIER_WIDTH // N_FOURIER_GROUPS
CONV_WIDTH = D_MODEL - FOURIER_WIDTH
D_FF = 2816
ROPE_THETA = 10000.0
LN_EPS = 1e-6
RMS_EPS = 1e-6
IN_EVEN = ATTN_WIDTH + 2 * KV_WIDTH + POOL_WIDTH
IN_ODD = 3 * CONV_WIDTH + FOURIER_WIDTH
DEEPNORM_ALPHA = (2 * DEPTH) ** 0.25

SUBLANES = 8
LANES = 128
VMEM_LIMIT_BYTES = 56 * 1024 * 1024

TILE_M = 256
HALO = SUBLANES
COND_ROWS = 16
CTX_ROW = 8

F32 = jnp.float32
BF16 = jnp.bfloat16


def _params(n_axes):
    return pltpu.CompilerParams(dimension_semantics=("arbitrary",) * n_axes,
                                vmem_limit_bytes=VMEM_LIMIT_BYTES)


def _resident(block_shape, index_map):
    return pl.BlockSpec(block_shape, index_map, pipeline_mode=pl.Buffered(1))


def _dot(a, b):
    return jnp.dot(a, b, preferred_element_type=F32)


def _layer_norm(x):
    mu = jnp.mean(x, axis=-1, keepdims=True)
    xc = x - mu
    var = jnp.mean(xc * xc, axis=-1, keepdims=True)
    return xc * lax.rsqrt(var + LN_EPS)


def _split_dot(t, w):
    hi = t.astype(BF16)
    lo = (t - hi.astype(F32)).astype(BF16)
    return _dot(hi, w) + _dot(lo, w)


def _ada_kernel(cond_ref, w_ref, b_ref, o_ref):
    cnd = cond_ref[...]
    act = (cnd * jax.nn.sigmoid(cnd)).astype(BF16)
    o_ref[...] = _dot(act, w_ref[...].astype(BF16)) + b_ref[...]


def _ada_all(cond, w_ada, b_ada):
    tn = 1536
    n_out = 6 * D_MODEL
    return pl.pallas_call(
        _ada_kernel,
        grid=(DEPTH, n_out // tn),
        in_specs=[
            pl.BlockSpec((COND_ROWS, D_MODEL), lambda l, j: (0, 0)),
            pl.BlockSpec((None, D_MODEL, tn), lambda l, j: (l, 0, j)),
            pl.BlockSpec((None, 1, tn), lambda l, j: (l, 0, j)),
        ],
        out_specs=pl.BlockSpec((None, COND_ROWS, tn), lambda l, j: (l, 0, j)),
        out_shape=jax.ShapeDtypeStruct((DEPTH, COND_ROWS, n_out), F32),
        compiler_params=_params(2),
        name="ada",
    )(cond, w_ada, b_ada.reshape(DEPTH, 1, n_out))


class _Seq:
    def __init__(self, n_batch, seq_len, latent):
        self.n_batch = n_batch
        self.seq_len = seq_len
        self.latent = latent
        self.tiles_per_seq = seq_len // TILE_M
        self.n_tiles = n_batch * self.tiles_per_seq
        self.n_rows = n_batch * seq_len

    def cond_row(self, i):
        return i // self.tiles_per_seq if self.latent else CTX_ROW

    def tile_spec(self, width):
        return pl.BlockSpec((TILE_M, width), lambda i: (i, 0))

    def halo_specs(self, width):
        per_tile = TILE_M // HALO
        last = self.n_rows // HALO - 1
        prev = pl.BlockSpec((HALO, width), lambda i: (jnp.maximum(i * per_tile - 1, 0), 0))
        nxt = pl.BlockSpec((HALO, width), lambda i: (jnp.minimum((i + 1) * per_tile, last), 0))
        return prev, nxt

    def mod_spec(self, layer, which):
        return pl.BlockSpec((None, None, None, 1, D_MODEL),
                            lambda i: (layer, self.cond_row(i), which, 0, 0))

    def edge_flags(self):
        j = pl.program_id(0) % self.tiles_per_seq
        return j > 0, j < self.tiles_per_seq - 1


def _row_spec(layer, width, n_lead=1):
    if n_lead == 1:
        return pl.BlockSpec((None, 1, width), lambda i: (layer, 0, 0))
    return pl.BlockSpec((None, None, 1, width), lambda i: (layer[0], layer[1], 0, 0))


def _shift_rows(cur, prev_row, next_row):
    n = cur.shape[0]
    row = lax.broadcasted_iota(jnp.int32, (n, 1), 0)
    up = jnp.where(row == 0, prev_row, pltpu.roll(cur, 1, 0))
    down = jnp.where(row == n - 1, next_row, pltpu.roll(cur, n - 1, 0))
    return up, down


def _post_norm(x, gate, y, g, b):
    return _layer_norm(DEEPNORM_ALPHA * x + gate * y) * g + b


def _even_in_kernel(seq, x_ref, sc_ref, sh_ref, w_ref, gq_ref, gk_ref, bd_ref, *rest):
    if seq.latent:
        cos_ref, sin_ref, q_out, k_out, v_out, p_out = rest
    else:
        q_out, k_out, v_out, p_out, kraw_out, vraw_out = rest
    u = (_layer_norm(x_ref[...]) * (1.0 + sc_ref[...]) + sh_ref[...]).astype(BF16)
    proj = _dot(u, w_ref[...])
    o1 = ATTN_WIDTH
    o2 = o1 + KV_WIDTH
    o3 = o2 + KV_WIDTH
    q = proj[:, :o1]
    k = proj[:, o1:o2]
    v = proj[:, o2:o3]
    p_out[...] = proj[:, o3:]

    bd = bd_ref[...]
    q = q * lax.rsqrt(_split_dot(q * q, bd) * (1.0 / HEAD_DIM) + RMS_EPS) * gq_ref[...]
    k = k * lax.rsqrt(_split_dot(k * k, bd[:KV_WIDTH, :KV_WIDTH]) * (1.0 / HEAD_DIM) + RMS_EPS) * gk_ref[...]

    if seq.latent:
        cos = cos_ref[...]
        sin = sin_ref[...]
        even_lane = lax.broadcasted_iota(jnp.int32, (TILE_M, LANES), 1) % 2 == 0

        def rope(t):
            outs = []
            for s in range(t.shape[1] // LANES):
                slab = t[:, s * LANES:(s + 1) * LANES]
                partner = jnp.where(even_lane, pltpu.roll(slab, LANES - 1, 1), pltpu.roll(slab, 1, 1))
                outs.append(slab * cos + partner * sin)
            return outs[0] if len(outs) == 1 else jnp.concatenate(outs, axis=1)

        q = rope(q)
        k = rope(k)
    else:
        kraw_out[...] = k
        vraw_out[...] = v
    q_out[...] = (q * (HEAD_DIM ** -0.5)).astype(BF16)
    k_out[...] = k.astype(BF16)
    v_out[...] = v.astype(BF16)


def _even_in(seq, layer, e, x, ada5, w_in, gq, gk, bd, rope_tabs):
    n = seq.n_rows
    in_specs = [
        seq.tile_spec(D_MODEL),
        seq.mod_spec(layer, 1),
        seq.mod_spec(layer, 0),
        _resident((None, D_MODEL, IN_EVEN), lambda i: (e, 0, 0)),
        _row_spec(e, ATTN_WIDTH),
        _row_spec(e, KV_WIDTH),
        _resident((ATTN_WIDTH, ATTN_WIDTH), lambda i: (0, 0)),
    ]
    args = [x, ada5, ada5, w_in, gq, gk, bd]
    out_specs = [seq.tile_spec(ATTN_WIDTH), seq.tile_spec(KV_WIDTH), seq.tile_spec(KV_WIDTH),
                 seq.tile_spec(POOL_WIDTH)]
    out_shape = [jax.ShapeDtypeStruct((n, ATTN_WIDTH), BF16), jax.ShapeDtypeStruct((n, KV_WIDTH), BF16),
                 jax.ShapeDtypeStruct((n, KV_WIDTH), BF16), jax.ShapeDtypeStruct((n, POOL_WIDTH), F32)]
    if seq.latent:
        tps = seq.tiles_per_seq
        in_specs += [pl.BlockSpec((TILE_M, LANES), lambda i: (i % tps, 0))] * 2
        args += list(rope_tabs)
    else:
        out_specs += [seq.tile_spec(KV_WIDTH)] * 2
        out_shape += [jax.ShapeDtypeStruct((n, KV_WIDTH), F32)] * 2
    return pl.pallas_call(
        functools.partial(_even_in_kernel, seq),
        grid=(seq.n_tiles,),
        in_specs=in_specs,
        out_specs=out_specs,
        out_shape=out_shape,
        compiler_params=_params(1),
        name="even_in",
    )(*args)


def _attn_kernel(q_ref, k_ref, v_ref, o_ref):
    group = N_HEADS // N_KV_HEADS
    for g in range(N_KV_HEADS):
        kg = k_ref[:, g * HEAD_DIM:(g + 1) * HEAD_DIM]
        vg = v_ref[:, g * HEAD_DIM:(g + 1) * HEAD_DIM]
        for hh in range(group):
            h = g * group + hh
            qh = q_ref[:, h * HEAD_DIM:(h + 1) * HEAD_DIM]
            s = lax.dot_general(qh, kg, (((1,), (1,)), ((), ())), preferred_element_type=F32)
            m = jnp.max(s, axis=-1, keepdims=True)
            p = jnp.exp(s - m)
            denom = jnp.sum(p, axis=-1, keepdims=True)
            o = _dot(p.astype(BF16), vg) / denom
            o_ref[:, h * HEAD_DIM:(h + 1) * HEAD_DIM] = o.astype(BF16)


def _attention(seq, q, k, v):
    n_keys = k.shape[1]
    tps = seq.tiles_per_seq
    return pl.pallas_call(
        _attn_kernel,
        grid=(seq.n_batch, tps),
        in_specs=[
            pl.BlockSpec((TILE_M, ATTN_WIDTH), lambda b, j: (b * tps + j, 0)),
            pl.BlockSpec((None, n_keys, KV_WIDTH), lambda b, j: (b, 0, 0)),
            pl.BlockSpec((None, n_keys, KV_WIDTH), lambda b, j: (b, 0, 0)),
        ],
        out_specs=pl.BlockSpec((TILE_M, ATTN_WIDTH), lambda b, j: (b * tps + j, 0)),
        out_shape=jax.ShapeDtypeStruct((seq.n_rows, ATTN_WIDTH), BF16),
        compiler_params=_params(2),
        name="attention",
    )(q, k, v)


def _even_out_kernel(seq, attn_ref, p_ref, pp_ref, pn_ref, x_ref, gate_ref, wp_ref, ps_ref, wo_ref,
                     g_ref, b_ref, o_ref):
    has_prev, has_next = seq.edge_flags()
    p = p_ref[...]
    ext = jnp.concatenate([jnp.where(has_prev, pp_ref[...], 0.0), p,
                           jnp.where(has_next, pn_ref[...], 0.0)], axis=0)
    n_ext = TILE_M + 2 * HALO
    pos = (pl.program_id(0) % seq.tiles_per_seq) * TILE_M + lax.broadcasted_iota(jnp.int32, (TILE_M, 1), 0)
    mixed = []
    for gi, w in enumerate(POOL_WINDOWS):
        half = w // 2
        lanes = slice(gi * POOL_GROUP, (gi + 1) * POOL_GROUP)
        run = ext[:, lanes]
        span = 1
        while span < w:
            run = run + pltpu.roll(run, span, 0)
            span *= 2
        if half > 1:
            run = pltpu.roll(run, n_ext - (half - 1), 0)
        total = run[HALO:HALO + TILE_M]
        cnt = jnp.minimum(pos + half, seq.seq_len) - jnp.maximum(pos - half, 0)
        centred = total / cnt.astype(F32) - p[:, lanes]
        mixed.append(_dot(centred.astype(BF16), wp_ref[gi]))
    pool = jnp.concatenate(mixed, axis=1) * ps_ref[...]
    m = jnp.concatenate([attn_ref[...], pool.astype(BF16)], axis=1)
    y = _dot(m, wo_ref[...])
    o_ref[...] = _post_norm(x_ref[...], gate_ref[...], y, g_ref[...], b_ref[...])


def _even_out(seq, layer, e, attn, p, x, ada5, w_pool, pool_scale, w_out, ln_g, ln_b):
    prev_spec, next_spec = seq.halo_specs(POOL_WIDTH)
    return pl.pallas_call(
        functools.partial(_even_out_kernel, seq),
        grid=(seq.n_tiles,),
        in_specs=[
            seq.tile_spec(ATTN_WIDTH),
            seq.tile_spec(POOL_WIDTH), prev_spec, next_spec,
            seq.tile_spec(D_MODEL),
            seq.mod_spec(layer, 2),
            _resident((None, len(POOL_WINDOWS), POOL_GROUP, POOL_GROUP), lambda i: (e, 0, 0, 0)),
            _row_spec(e, POOL_WIDTH),
            _resident((None, D_MODEL, D_MODEL), lambda i: (layer, 0, 0)),
            _row_spec((layer, 0), D_MODEL, 2),
            _row_spec((layer, 0), D_MODEL, 2),
        ],
        out_specs=seq.tile_spec(D_MODEL),
        out_shape=jax.ShapeDtypeStruct((seq.n_rows, D_MODEL), F32),
        compiler_params=_params(1),
        name="even_out",
    )(attn, p, p, p, x, ada5, w_pool, pool_scale, w_out, ln_g, ln_b)


def _odd_in_kernel(x_ref, sc_ref, sh_ref, w_ref, ch_out, bg_out, f_out):
    u = (_layer_norm(x_ref[...]) * (1.0 + sc_ref[...]) + sh_ref[...]).astype(BF16)
    proj = _dot(u, w_ref[...])
    c = CONV_WIDTH
    ch_out[...] = proj[:, 2 * c:3 * c] * proj[:, :c]
    bg_out[...] = proj[:, c:2 * c]
    f_out[...] = proj[:, 3 * c:]


def _odd_in(seq, layer, o, x, ada5, w_in):
    n = seq.n_rows
    return pl.pallas_call(
        _odd_in_kernel,
        grid=(seq.n_tiles,),
        in_specs=[
            seq.tile_spec(D_MODEL),
            seq.mod_spec(layer, 1),
            seq.mod_spec(layer, 0),
            _resident((None, D_MODEL, IN_ODD), lambda i: (o, 0, 0)),
        ],
        out_specs=[seq.tile_spec(CONV_WIDTH), seq.tile_spec(CONV_WIDTH), seq.tile_spec(FOURIER_WIDTH)],
        out_shape=[jax.ShapeDtypeStruct((n, CONV_WIDTH), F32), jax.ShapeDtypeStruct((n, CONV_WIDTH), F32),
                   jax.ShapeDtypeStruct((n, FOURIER_WIDTH), F32)],
        compiler_params=_params(1),
        name="odd_in",
    )(x, ada5, ada5, w_in)


def _fourier_kernel(scale, f_ref, cl_ref, sl_ref, cc_ref, sc_ref, o_ref):
    fb = f_ref[...].astype(BF16)
    g_cos = (_dot(fb, cc_ref[...]) * scale).astype(BF16)
    g_sin = (_dot(fb, sc_ref[...]) * scale).astype(BF16)
    o_ref[...] = _dot(cl_ref[...], g_cos) - _dot(sl_ref[...], g_sin)


def _dft_tables(n):
    idx = lax.iota(jnp.int32, n)
    jk = (idx[:, None] * idx[None, :]) % n
    ang = jk.astype(F32) * (2.0 * math.pi / n)
    return jnp.cos(ang), jnp.sin(ang)


def _fourier(seq, f, tabs):
    L = seq.seq_len
    cl, sl, cc, sc = tabs
    scale = 1.0 / math.sqrt(L * FOURIER_GROUP)
    return pl.pallas_call(
        functools.partial(_fourier_kernel, scale),
        grid=(seq.n_batch,),
        in_specs=[
            pl.BlockSpec((L, FOURIER_WIDTH), lambda b: (b, 0)),
            _resident((L, L), lambda b: (0, 0)),
            _resident((L, L), lambda b: (0, 0)),
            _resident((FOURIER_WIDTH, FOURIER_WIDTH), lambda b: (0, 0)),
            _resident((FOURIER_WIDTH, FOURIER_WIDTH), lambda b: (0, 0)),
        ],
        out_specs=pl.BlockSpec((L, FOURIER_WIDTH), lambda b: (b, 0)),
        out_shape=jax.ShapeDtypeStruct((seq.n_rows, FOURIER_WIDTH), F32),
        compiler_params=_params(1),
        name="fourier",
    )(f, cl, sl, cc, sc)


def _odd_out_kernel(seq, ch_ref, chp_ref, chn_ref, bg_ref, fo_ref, x_ref, gate_ref, cw_ref, cb_ref, wo_ref,
                    g_ref, b_ref, o_ref):
    has_prev, has_next = seq.edge_flags()
    ch = ch_ref[...]
    prev_row = jnp.where(has_prev, chp_ref[HALO - 1:HALO, :], 0.0)
    next_row = jnp.where(has_next, chn_ref[0:1, :], 0.0)
    up, down = _shift_rows(ch, prev_row, next_row)
    conv = cb_ref[...] + up * cw_ref[0:1, :] + ch * cw_ref[1:2, :] + down * cw_ref[2:3, :]
    conv_out = bg_ref[...] * conv
    m = jnp.concatenate([conv_out.astype(BF16), fo_ref[...].astype(BF16)], axis=1)
    y = _dot(m, wo_ref[...])
    o_ref[...] = _post_norm(x_ref[...], gate_ref[...], y, g_ref[...], b_ref[...])


def _odd_out(seq, layer, o, ch, bg, fo, x, ada5, conv_w, conv_b, w_out, ln_g, ln_b):
    prev_spec, next_spec = seq.halo_specs(CONV_WIDTH)
    return pl.pallas_call(
        functools.partial(_odd_out_kernel, seq),
        grid=(seq.n_tiles,),
        in_specs=[
            seq.tile_spec(CONV_WIDTH), prev_spec, next_spec,
            seq.tile_spec(CONV_WIDTH),
            seq.tile_spec(FOURIER_WIDTH),
            seq.tile_spec(D_MODEL),
            seq.mod_spec(layer, 2),
            pl.BlockSpec((None, 3, CONV_WIDTH), lambda i: (o, 0, 0)),
            _row_spec(o, CONV_WIDTH),
            _resident((None, D_MODEL, D_MODEL), lambda i: (layer, 0, 0)),
            _row_spec((layer, 0), D_MODEL, 2),
            _row_spec((layer, 0), D_MODEL, 2),
        ],
        out_specs=seq.tile_spec(D_MODEL),
        out_shape=jax.ShapeDtypeStruct((seq.n_rows, D_MODEL), F32),
        compiler_params=_params(1),
        name="odd_out",
    )(ch, ch, ch, bg, fo, x, ada5, conv_w, conv_b, w_out, ln_g, ln_b)


def _ffn_kernel(seq, x_ref, xp_ref, xn_ref, sc_ref, sh_ref, gate_ref, wa_ref, wg_ref, cw_ref, cb_ref, wd_ref,
                g_ref, b_ref, o_ref):
    has_prev, has_next = seq.edge_flags()
    x = x_ref[...]
    n_ext = TILE_M + 2 * HALO
    x_ext = jnp.concatenate([xp_ref[...], x, xn_ref[...]], axis=0)
    u_ext = (_layer_norm(x_ext) * (1.0 + sc_ref[...]) + sh_ref[...]).astype(BF16)
    a_ext = _dot(u_ext, wa_ref[...])
    row = lax.broadcasted_iota(jnp.int32, (n_ext, 1), 0)
    inside = ((row >= HALO) | has_prev) & ((row < HALO + TILE_M) | has_next)
    a_ext = jnp.where(inside, a_ext, 0.0)
    up = pltpu.roll(a_ext, 1, 0)[HALO:HALO + TILE_M]
    mid = a_ext[HALO:HALO + TILE_M]
    down = pltpu.roll(a_ext, n_ext - 1, 0)[HALO:HALO + TILE_M]
    conv = cb_ref[...] + up * cw_ref[0:1, :] + mid * cw_ref[1:2, :] + down * cw_ref[2:3, :]
    gate_lin = _dot(u_ext[HALO:HALO + TILE_M], wg_ref[...])
    hidden = (conv * jax.nn.sigmoid(conv) * gate_lin).astype(BF16)
    y = _dot(hidden, wd_ref[...])
    o_ref[...] = _post_norm(x, gate_ref[...], y, g_ref[...], b_ref[...])


def _ffn(seq, layer, x, ada5, w_up, ffn_conv_w, ffn_conv_b, w_down, ln_g, ln_b):
    prev_spec, next_spec = seq.halo_specs(D_MODEL)
    return pl.pallas_call(
        functools.partial(_ffn_kernel, seq),
        grid=(seq.n_tiles,),
        in_specs=[
            seq.tile_spec(D_MODEL), prev_spec, next_spec,
            seq.mod_spec(layer, 4),
            seq.mod_spec(layer, 3),
            seq.mod_spec(layer, 5),
            _resident((None, D_MODEL, D_FF), lambda i: (layer, 0, 0)),
            _resident((None, D_MODEL, D_FF), lambda i: (layer, 0, 1)),
            pl.BlockSpec((None, 3, D_FF), lambda i: (layer, 0, 0)),
            _row_spec(layer, D_FF),
            _resident((None, D_FF, D_MODEL), lambda i: (layer, 0, 0)),
            _row_spec((layer, 1), D_MODEL, 2),
            _row_spec((layer, 1), D_MODEL, 2),
        ],
        out_specs=seq.tile_spec(D_MODEL),
        out_shape=jax.ShapeDtypeStruct((seq.n_rows, D_MODEL), F32),
        compiler_params=_params(1),
        name="conv_ffn",
    )(x, x, x, ada5, ada5, ada5, w_up, w_up, ffn_conv_w, ffn_conv_b, w_down, ln_g, ln_b)


def _rope_tables(seq_len):
    t = lax.iota(jnp.int32, seq_len)
    row = (t // GRID_W).astype(F32)
    col = (t % GRID_W).astype(F32)
    n_freq = HEAD_DIM // 4
    inv = 1.0 / (ROPE_THETA ** (jnp.arange(n_freq, dtype=F32) / n_freq))
    ang = jnp.concatenate([row[:, None] * inv, col[:, None] * inv], -1)
    cos = jnp.repeat(jnp.cos(ang), 2, axis=1)
    sin = jnp.repeat(jnp.sin(ang), 2, axis=1) * jnp.tile(jnp.array([-1.0, 1.0], F32), HEAD_DIM // 2)
    reps = LANES // HEAD_DIM
    return jnp.tile(cos, (1, reps)), jnp.tile(sin, (1, reps))


def _fourier_tables(seq_len):
    cl, sl = _dft_tables(seq_len)
    cc, sc = _dft_tables(FOURIER_GROUP)
    eye = jnp.eye(N_FOURIER_GROUPS, dtype=F32)
    return (cl.astype(BF16), sl.astype(BF16), jnp.kron(eye, cc).astype(BF16), jnp.kron(eye, sc).astype(BF16))


def _run_trunk(seq, x, ada5, ctx_k, ctx_v, wts):
    (w_in_even, gq, gk, bd, w_pool, pool_scale, w_in_odd, conv_w, conv_b, w_out, w_up, ffn_conv_w,
     ffn_conv_b, w_down, ln_g, ln_b) = wts
    rope_tabs = _rope_tables(seq.seq_len) if seq.latent else None
    four_tabs = _fourier_tables(seq.seq_len)
    new_k, new_v = [], []
    for layer in range(DEPTH):
        if layer % 2 == 0:
            e = layer // 2
            outs = _even_in(seq, layer, e, x, ada5, w_in_even, gq, gk, bd, rope_tabs)
            q, k, v, p = outs[:4]
            k = k.reshape(seq.n_batch, seq.seq_len, KV_WIDTH)
            v = v.reshape(seq.n_batch, seq.seq_len, KV_WIDTH)
            if seq.latent:
                k = jnp.concatenate([k, ctx_k[:, e]], axis=1)
                v = jnp.concatenate([v, ctx_v[:, e]], axis=1)
            else:
                new_k.append(outs[4])
                new_v.append(outs[5])
            attn = _attention(seq, q, k, v)
            x = _even_out(seq, layer, e, attn, p, x, ada5, w_pool, pool_scale, w_out, ln_g, ln_b)
        else:
            o = layer // 2
            ch, bg, f = _odd_in(seq, layer, o, x, ada5, w_in_odd)
            fo = _fourier(seq, f, four_tabs)
            x = _odd_out(seq, layer, o, ch, bg, fo, x, ada5, conv_w, conv_b, w_out, ln_g, ln_b)
        x = _ffn(seq, layer, x, ada5, w_up, ffn_conv_w, ffn_conv_b, w_down, ln_g, ln_b)
    return x, new_k, new_v


def kernel(x_prompt, x_sample, cache_k, cache_v, c, c_ctx, w_ada, b_ada, w_in_even, q_norm_g, k_norm_g,
           w_pool, pool_scale, w_in_odd, conv_w, conv_b, w_out, w_up, ffn_conv_w, ffn_conv_b, w_down,
           ln_g, ln_b):
    n_prompt, prompt_len, _ = x_prompt.shape
    n_sample, sample_len, _ = x_sample.shape
    n_even = w_in_even.shape[0]
    n_odd = w_in_odd.shape[0]
    assert n_sample <= CTX_ROW and prompt_len % TILE_M == 0 and sample_len % TILE_M == 0

    cond = jnp.zeros((COND_ROWS, D_MODEL), F32).at[:n_sample].set(c).at[CTX_ROW].set(c_ctx)
    ada = _ada_all(cond, w_ada, b_ada)
    ada5 = ada.reshape(DEPTH, COND_ROWS, 6, 1, D_MODEL)

    head_of = lax.iota(jnp.int32, ATTN_WIDTH) // HEAD_DIM
    bd = (head_of[:, None] == head_of[None, :]).astype(BF16)
    wts = (
        w_in_even.astype(BF16),
        jnp.tile(q_norm_g, (1, N_HEADS)).reshape(n_even, 1, ATTN_WIDTH),
        jnp.tile(k_norm_g, (1, N_KV_HEADS)).reshape(n_even, 1, KV_WIDTH),
        bd,
        w_pool.astype(BF16),
        pool_scale.reshape(n_even, 1, POOL_WIDTH),
        w_in_odd.astype(BF16),
        conv_w,
        conv_b.reshape(n_odd, 1, CONV_WIDTH),
        w_out.astype(BF16),
        w_up.astype(BF16),
        ffn_conv_w,
        ffn_conv_b.reshape(DEPTH, 1, D_FF),
        w_down.astype(BF16),
        ln_g.reshape(DEPTH, 2, 1, D_MODEL),
        ln_b.reshape(DEPTH, 2, 1, D_MODEL),
    )

    prompt = _Seq(n_prompt, prompt_len, latent=False)
    y_prompt, ks, vs = _run_trunk(prompt, x_prompt.reshape(-1, D_MODEL), ada5, None, None, wts)
    cache_shape = (n_prompt, prompt_len, N_KV_HEADS, HEAD_DIM)
    new_cache_k = jnp.stack([k.reshape(cache_shape) for k in ks], 1)
    new_cache_v = jnp.stack([v.reshape(cache_shape) for v in vs], 1)

    sample = _Seq(n_sample, sample_len, latent=True)
    past_len = cache_k.shape[2]
    ctx_k = cache_k.reshape(n_sample, n_even, past_len, KV_WIDTH).astype(BF16)
    ctx_v = cache_v.reshape(n_sample, n_even, past_len, KV_WIDTH).astype(BF16)
    y_sample, _, _ = _run_trunk(sample, x_sample.reshape(-1, D_MODEL), ada5, ctx_k, ctx_v, wts)

    return (y_prompt.reshape(x_prompt.shape), y_sample.reshape(x_sample.shape), new_cache_k, new_cache_v)
```

```python
import functools
import math

import jax
import jax.numpy as jnp
from jax import lax
from jax.experimental import pallas as pl
from jax.experimental.pallas import tpu as pltpu

D_MODEL = 1024
DEPTH = 4
GRID_W = 64
N_HEADS = 8
N_KV_HEADS = 2
HEAD_DIM = 64
ATTN_WIDTH = N_HEADS * HEAD_DIM
KV_WIDTH = N_KV_HEADS * HEAD_DIM
POOL_WIDTH = D_MODEL - ATTN_WIDTH
POOL_WINDOWS = (2, 4, 8, 16)
POOL_GROUP = POOL_WIDTH // len(POOL_WINDOWS)
FOURIER_WIDTH = D_MODEL // 4
N_FOURIER_GROUPS = 4
FOURIER_GROUP = FOURIER_WIDTH // N_FOURIER_GROUPS
CONV_WIDTH = D_MODEL - FOURIER_WIDTH
D_FF = 2816
ROPE_THETA = 10000.0
LN_EPS = 1e-6
RMS_EPS = 1e-6
IN_EVEN = ATTN_WIDTH + 2 * KV_WIDTH + POOL_WIDTH
IN_ODD = 3 * CONV_WIDTH + FOURIER_WIDTH
DEEPNORM_ALPHA = (2 * DEPTH) ** 0.25

SUBLANES = 8
LANES = 128
VMEM_LIMIT_BYTES = 56 * 1024 * 1024

TILE_M = 256
FFN_TILE_M = 256
FFN_CHUNKS = 1
HALO_F32 = SUBLANES
HALO_BF16 = 2 * SUBLANES
COND_ROWS = 16
CTX_ROW = 8

F32 = jnp.float32
BF16 = jnp.bfloat16


def _params(n_axes):
    return pltpu.CompilerParams(dimension_semantics=("arbitrary",) * n_axes,
                                vmem_limit_bytes=VMEM_LIMIT_BYTES)


def _resident(block_shape, index_map):
    return pl.BlockSpec(block_shape, index_map, pipeline_mode=pl.Buffered(1))


def _dot(a, b):
    return jnp.dot(a, b, preferred_element_type=F32)


def _layer_norm(x):
    mu = jnp.mean(x, axis=-1, keepdims=True)
    xc = x - mu
    var = jnp.mean(xc * xc, axis=-1, keepdims=True)
    return xc * lax.rsqrt(var + LN_EPS)


def _split_dot(t, w):
    hi = t.astype(BF16)
    lo = (t - hi.astype(F32)).astype(BF16)
    return _dot(hi, w) + _dot(lo, w)


def _ada_kernel(cond_ref, w_ref, b_ref, o_ref):
    cnd = cond_ref[...]
    act = (cnd * jax.nn.sigmoid(cnd)).astype(BF16)
    o_ref[...] = _dot(act, w_ref[...].astype(BF16)) + b_ref[...]


def _ada_all(cond, w_ada, b_ada):
    tn = 1536
    n_out = 6 * D_MODEL
    return pl.pallas_call(
        _ada_kernel,
        grid=(DEPTH, n_out // tn),
        in_specs=[
            pl.BlockSpec((COND_ROWS, D_MODEL), lambda l, j: (0, 0)),
            pl.BlockSpec((None, D_MODEL, tn), lambda l, j: (l, 0, j)),
            pl.BlockSpec((None, 1, tn), lambda l, j: (l, 0, j)),
        ],
        out_specs=pl.BlockSpec((None, COND_ROWS, tn), lambda l, j: (l, 0, j)),
        out_shape=jax.ShapeDtypeStruct((DEPTH, COND_ROWS, n_out), F32),
        compiler_params=_params(2),
        name="ada",
    )(cond, w_ada, b_ada.reshape(DEPTH, 1, n_out))


class _Seq:
    def __init__(self, n_batch, seq_len, latent, tile_m=TILE_M):
        assert seq_len % tile_m == 0
        self.n_batch = n_batch
        self.seq_len = seq_len
        self.latent = latent
        self.tile_m = tile_m
        self.tiles_per_seq = seq_len // tile_m
        self.n_tiles = n_batch * self.tiles_per_seq
        self.n_rows = n_batch * seq_len

    def with_tile(self, tile_m):
        return _Seq(self.n_batch, self.seq_len, self.latent, tile_m)

    def cond_row(self, i):
        return i // self.tiles_per_seq if self.latent else CTX_ROW

    def tile_spec(self, width):
        return pl.BlockSpec((self.tile_m, width), lambda i: (i, 0))

    def halo_specs(self, width, halo):
        per_tile = self.tile_m // halo
        last = self.n_rows // halo - 1
        prev = pl.BlockSpec((halo, width), lambda i: (jnp.maximum(i * per_tile - 1, 0), 0))
        nxt = pl.BlockSpec((halo, width), lambda i: (jnp.minimum((i + 1) * per_tile, last), 0))
        return prev, nxt

    def mod_spec(self, layer, which):
        return pl.BlockSpec((None, None, None, 1, D_MODEL),
                            lambda i: (layer, self.cond_row(i), which, 0, 0))

    def edge_flags(self):
        j = pl.program_id(0) % self.tiles_per_seq
        return j > 0, j < self.tiles_per_seq - 1


def _row_spec(layer, width, n_lead=1):
    if n_lead == 1:
        return pl.BlockSpec((None, 1, width), lambda i: (layer, 0, 0))
    return pl.BlockSpec((None, None, 1, width), lambda i: (layer[0], layer[1], 0, 0))


def _shift_rows(cur, prev_row, next_row):
    n = cur.shape[0]
    row = lax.broadcasted_iota(jnp.int32, (n, 1), 0)
    up = jnp.where(row == 0, prev_row, pltpu.roll(cur, 1, 0))
    down = jnp.where(row == n - 1, next_row, pltpu.roll(cur, n - 1, 0))
    return up, down


def _post_norm(x, gate, y, g, b):
    return _layer_norm(DEEPNORM_ALPHA * x + gate * y) * g + b


def _even_in_kernel(seq, x_ref, sc_ref, sh_ref, w_ref, gq_ref, gk_ref, bd_ref, *rest):
    if seq.latent:
        cos_ref, sin_ref, q_out, k_out, v_out, p_out = rest
    else:
        q_out, k_out, v_out, p_out, kraw_out, vraw_out = rest
    u = (_layer_norm(x_ref[...]) * (1.0 + sc_ref[...]) + sh_ref[...]).astype(BF16)
    proj = _dot(u, w_ref[...])
    o1 = ATTN_WIDTH
    o2 = o1 + KV_WIDTH
    o3 = o2 + KV_WIDTH
    q = proj[:, :o1]
    k = proj[:, o1:o2]
    v = proj[:, o2:o3]
    p_out[...] = proj[:, o3:].astype(BF16)

    bd = bd_ref[...]
    q = q * lax.rsqrt(_split_dot(q * q, bd) * (1.0 / HEAD_DIM) + RMS_EPS) * gq_ref[...]
    k = k * lax.rsqrt(_split_dot(k * k, bd[:KV_WIDTH, :KV_WIDTH]) * (1.0 / HEAD_DIM) + RMS_EPS) * gk_ref[...]

    if seq.latent:
        cos = cos_ref[...]
        sin = sin_ref[...]
        even_lane = (lax.broadcasted_iota(jnp.int32, (seq.tile_m, LANES), 1) & 1) == 0

        def rope(t):
            outs = []
            for s in range(t.shape[1] // LANES):
                slab = t[:, s * LANES:(s + 1) * LANES]
                partner = jnp.where(even_lane, pltpu.roll(slab, LANES - 1, 1), pltpu.roll(slab, 1, 1))
                outs.append(slab * cos + partner * sin)
            return outs[0] if len(outs) == 1 else jnp.concatenate(outs, axis=1)

        q = rope(q)
        k = rope(k)
    else:
        kraw_out[...] = k
        vraw_out[...] = v
    q_out[...] = (q * (HEAD_DIM ** -0.5 * math.log2(math.e))).astype(BF16)
    k_out[...] = k.astype(BF16)
    v_out[...] = v.astype(BF16)


def _even_in(seq, layer, e, x, ada5, w_in, gq, gk, bd, rope_tabs):
    n = seq.n_rows
    in_specs = [
        seq.tile_spec(D_MODEL),
        seq.mod_spec(layer, 1),
        seq.mod_spec(layer, 0),
        _resident((None, D_MODEL, IN_EVEN), lambda i: (e, 0, 0)),
        _row_spec(e, ATTN_WIDTH),
        _row_spec(e, KV_WIDTH),
        _resident((ATTN_WIDTH, ATTN_WIDTH), lambda i: (0, 0)),
    ]
    args = [x, ada5, ada5, w_in, gq, gk, bd]
    out_specs = [seq.tile_spec(ATTN_WIDTH), seq.tile_spec(KV_WIDTH), seq.tile_spec(KV_WIDTH),
                 seq.tile_spec(POOL_WIDTH)]
    out_shape = [jax.ShapeDtypeStruct((n, ATTN_WIDTH), BF16), jax.ShapeDtypeStruct((n, KV_WIDTH), BF16),
                 jax.ShapeDtypeStruct((n, KV_WIDTH), BF16), jax.ShapeDtypeStruct((n, POOL_WIDTH), BF16)]
    if seq.latent:
        tps = seq.tiles_per_seq
        in_specs += [pl.BlockSpec((seq.tile_m, LANES), lambda i: (i % tps, 0))] * 2
        args += list(rope_tabs)
    else:
        out_specs += [seq.tile_spec(KV_WIDTH)] * 2
        out_shape += [jax.ShapeDtypeStruct((n, KV_WIDTH), F32)] * 2
    return pl.pallas_call(
        functools.partial(_even_in_kernel, seq),
        grid=(seq.n_tiles,),
        in_specs=in_specs,
        out_specs=out_specs,
        out_shape=out_shape,
        compiler_params=_params(1),
        name="even_in",
    )(*args)


def _reduce_rows(x, op, final, chunk=256):
    n = x.shape[0]
    if n > chunk and n % chunk == 0:
        parts = [x[i:i + chunk] for i in range(0, n, chunk)]
        while len(parts) > 1:
            parts = [op(parts[i], parts[i + 1]) if i + 1 < len(parts) else parts[i]
                     for i in range(0, len(parts), 2)]
        x = parts[0]
        n = chunk
    while n > SUBLANES and n % (2 * SUBLANES) == 0:
        n //= 2
        x = op(x[:n], x[n:])
    return final(x, axis=0, keepdims=True)


def _attn_kernel(q_ref, k_ref, vt_ref, o_ref):
    group = N_HEADS // N_KV_HEADS
    heads_per_dot = 2
    tq = q_ref.shape[0]

    def scores(h0):
        qs = jnp.concatenate([q_ref[:, h * HEAD_DIM:(h + 1) * HEAD_DIM]
                              for h in range(h0, h0 + heads_per_dot)], axis=0)
        return lax.dot_general(k_ref[h0 // group], qs, (((1,), (1,)), ((), ())), preferred_element_type=F32)

    outs = []
    st_next = scores(0)
    for h0 in range(0, N_HEADS, heads_per_dot):
        st = st_next
        if h0 + heads_per_dot < N_HEADS:
            st_next = scores(h0 + heads_per_dot)
        g = h0 // group
        m = _reduce_rows(st, jnp.maximum, jnp.max)
        p = jnp.exp2(st - m)
        denom = _reduce_rows(p, jnp.add, jnp.sum)
        ot = _dot(vt_ref[g * HEAD_DIM:(g + 1) * HEAD_DIM, :], p.astype(BF16)) / denom
        outs += [ot[:, i * tq:(i + 1) * tq] for i in range(heads_per_dot)]
    o_ref[...] = jnp.concatenate(outs, axis=0).T.astype(BF16)


def _attention(seq, q, k, vt):
    n_keys = k.shape[2]
    tps = seq.tiles_per_seq
    return pl.pallas_call(
        _attn_kernel,
        grid=(seq.n_batch, tps),
        in_specs=[
            pl.BlockSpec((seq.tile_m, ATTN_WIDTH), lambda b, j: (b * tps + j, 0)),
            pl.BlockSpec((None, N_KV_HEADS, n_keys, HEAD_DIM), lambda b, j: (b, 0, 0, 0)),
            pl.BlockSpec((None, KV_WIDTH, n_keys), lambda b, j: (b, 0, 0)),
        ],
        out_specs=pl.BlockSpec((seq.tile_m, ATTN_WIDTH), lambda b, j: (b * tps + j, 0)),
        out_shape=jax.ShapeDtypeStruct((seq.n_rows, ATTN_WIDTH), BF16),
        compiler_params=_params(2),
        name="attention",
    )(q, k, vt)


def _even_out_kernel(seq, attn_ref, p_ref, pp_ref, pn_ref, x_ref, gate_ref, wp_ref, ps_ref, wo_ref,
                     g_ref, b_ref, o_ref):
    has_prev, has_next = seq.edge_flags()
    tile_m = seq.tile_m
    halo = pp_ref.shape[0]
    p = p_ref[...].astype(F32)
    ext = jnp.concatenate([jnp.where(has_prev, pp_ref[...].astype(F32), 0.0), p,
                           jnp.where(has_next, pn_ref[...].astype(F32), 0.0)], axis=0)
    n_ext = tile_m + 2 * halo
    pos = (pl.program_id(0) % seq.tiles_per_seq) * tile_m + lax.broadcasted_iota(jnp.int32, (tile_m, 1), 0)
    mixed = []
    for gi, w in enumerate(POOL_WINDOWS):
        half = w // 2
        lanes = slice(gi * POOL_GROUP, (gi + 1) * POOL_GROUP)
        run = ext[:, lanes]
        span = 1
        while span < w:
            run = run + pltpu.roll(run, span, 0)
            span *= 2
        if half > 1:
            run = pltpu.roll(run, n_ext - (half - 1), 0)
        total = run[halo:halo + tile_m]
        cnt = jnp.minimum(pos + half, seq.seq_len) - jnp.maximum(pos - half, 0)
        centred = total / cnt.astype(F32) - p[:, lanes]
        mixed.append(_dot(centred.astype(BF16), wp_ref[gi]))
    pool = jnp.concatenate(mixed, axis=1) * ps_ref[...]
    m = jnp.concatenate([attn_ref[...], pool.astype(BF16)], axis=1)
    y = _dot(m, wo_ref[...])
    o_ref[...] = _post_norm(x_ref[...], gate_ref[...], y, g_ref[...], b_ref[...])


def _even_out(seq, layer, e, attn, p, x, ada5, w_pool, pool_scale, w_out, ln_g, ln_b):
    prev_spec, next_spec = seq.halo_specs(POOL_WIDTH, HALO_BF16)
    return pl.pallas_call(
        functools.partial(_even_out_kernel, seq),
        grid=(seq.n_tiles,),
        in_specs=[
            seq.tile_spec(ATTN_WIDTH),
            seq.tile_spec(POOL_WIDTH), prev_spec, next_spec,
            seq.tile_spec(D_MODEL),
            seq.mod_spec(layer, 2),
            _resident((None, len(POOL_WINDOWS), POOL_GROUP, POOL_GROUP), lambda i: (e, 0, 0, 0)),
            _row_spec(e, POOL_WIDTH),
            _resident((None, D_MODEL, D_MODEL), lambda i: (layer, 0, 0)),
            _row_spec((layer, 0), D_MODEL, 2),
            _row_spec((layer, 0), D_MODEL, 2),
        ],
        out_specs=seq.tile_spec(D_MODEL),
        out_shape=jax.ShapeDtypeStruct((seq.n_rows, D_MODEL), F32),
        compiler_params=_params(1),
        name="even_out",
    )(attn, p, p, p, x, ada5, w_pool, pool_scale, w_out, ln_g, ln_b)


def _odd_in_kernel(x_ref, sc_ref, sh_ref, w_ref, ch_out, bg_out, f_out):
    u = (_layer_norm(x_ref[...]) * (1.0 + sc_ref[...]) + sh_ref[...]).astype(BF16)
    proj = _dot(u, w_ref[...])
    c = CONV_WIDTH
    ch_out[...] = (proj[:, 2 * c:3 * c] * proj[:, :c]).astype(BF16)
    bg_out[...] = proj[:, c:2 * c].astype(BF16)
    f_out[...] = proj[:, 3 * c:].astype(BF16)


def _odd_in(seq, layer, o, x, ada5, w_in):
    n = seq.n_rows
    return pl.pallas_call(
        _odd_in_kernel,
        grid=(seq.n_tiles,),
        in_specs=[
            seq.tile_spec(D_MODEL),
            seq.mod_spec(layer, 1),
            seq.mod_spec(layer, 0),
            _resident((None, D_MODEL, IN_ODD), lambda i: (o, 0, 0)),
        ],
        out_specs=[seq.tile_spec(CONV_WIDTH), seq.tile_spec(CONV_WIDTH), seq.tile_spec(FOURIER_WIDTH)],
        out_shape=[jax.ShapeDtypeStruct((n, CONV_WIDTH), BF16), jax.ShapeDtypeStruct((n, CONV_WIDTH), BF16),
                   jax.ShapeDtypeStruct((n, FOURIER_WIDTH), BF16)],
        compiler_params=_params(1),
        name="odd_in",
    )(x, ada5, ada5, w_in)


def _fourier_kernel(scale, f_ref, cl_ref, sl_ref, cc_ref, sc_ref, o_ref):
    fb = f_ref[...]
    g_cos = (_dot(fb, cc_ref[...]) * scale).astype(BF16)
    g_sin = (_dot(fb, sc_ref[...]) * scale).astype(BF16)
    o_ref[...] = (_dot(cl_ref[...], g_cos) - _dot(sl_ref[...], g_sin)).astype(BF16)


def _dft_tables(n):
    split = FOURIER_GROUP
    assert n % split == 0
    j = lax.iota(jnp.int32, n)[:, None]
    unit = 2.0 * math.pi / n
    ang_hi = ((j * (lax.iota(jnp.int32, n // split)[None, :] * split)) % n).astype(F32) * unit
    ang_lo = ((j * lax.iota(jnp.int32, split)[None, :]) % n).astype(F32) * unit
    ch, sh = jnp.cos(ang_hi)[:, :, None], jnp.sin(ang_hi)[:, :, None]
    cl, sl = jnp.cos(ang_lo)[:, None, :], jnp.sin(ang_lo)[:, None, :]
    return (ch * cl - sh * sl).reshape(n, n), (sh * cl + ch * sl).reshape(n, n)


def _fourier(seq, f, tabs):
    L = seq.seq_len
    cl, sl, cc, sc = tabs
    scale = 1.0 / math.sqrt(L * FOURIER_GROUP)
    return pl.pallas_call(
        functools.partial(_fourier_kernel, scale),
        grid=(seq.n_batch,),
        in_specs=[
            pl.BlockSpec((L, FOURIER_WIDTH), lambda b: (b, 0)),
            _resident((L, L), lambda b: (0, 0)),
            _resident((L, L), lambda b: (0, 0)),
            _resident((FOURIER_WIDTH, FOURIER_WIDTH), lambda b: (0, 0)),
            _resident((FOURIER_WIDTH, FOURIER_WIDTH), lambda b: (0, 0)),
        ],
        out_specs=pl.BlockSpec((L, FOURIER_WIDTH), lambda b: (b, 0)),
        out_shape=jax.ShapeDtypeStruct((seq.n_rows, FOURIER_WIDTH), BF16),
        compiler_params=_params(1),
        name="fourier",
    )(f, cl, sl, cc, sc)


def _odd_out_kernel(seq, ch_ref, chp_ref, chn_ref, bg_ref, fo_ref, x_ref, gate_ref, cw_ref, cb_ref, wo_ref,
                    g_ref, b_ref, o_ref):
    has_prev, has_next = seq.edge_flags()
    halo = chp_ref.shape[0]
    ch = ch_ref[...].astype(F32)
    prev_row = jnp.where(has_prev, chp_ref[...].astype(F32)[halo - 1:halo, :], 0.0)
    next_row = jnp.where(has_next, chn_ref[...].astype(F32)[0:1, :], 0.0)
    up, down = _shift_rows(ch, prev_row, next_row)
    conv = cb_ref[...] + up * cw_ref[0:1, :] + ch * cw_ref[1:2, :] + down * cw_ref[2:3, :]
    conv_out = bg_ref[...].astype(F32) * conv
    m = jnp.concatenate([conv_out.astype(BF16), fo_ref[...]], axis=1)
    y = _dot(m, wo_ref[...])
    o_ref[...] = _post_norm(x_ref[...], gate_ref[...], y, g_ref[...], b_ref[...])


def _odd_out(seq, layer, o, ch, bg, fo, x, ada5, conv_w, conv_b, w_out, ln_g, ln_b):
    prev_spec, next_spec = seq.halo_specs(CONV_WIDTH, HALO_BF16)
    return pl.pallas_call(
        functools.partial(_odd_out_kernel, seq),
        grid=(seq.n_tiles,),
        in_specs=[
            seq.tile_spec(CONV_WIDTH), prev_spec, next_spec,
            seq.tile_spec(CONV_WIDTH),
            seq.tile_spec(FOURIER_WIDTH),
            seq.tile_spec(D_MODEL),
            seq.mod_spec(layer, 2),
            pl.BlockSpec((None, 3, CONV_WIDTH), lambda i: (o, 0, 0)),
            _row_spec(o, CONV_WIDTH),
            _resident((None, D_MODEL, D_MODEL), lambda i: (layer, 0, 0)),
            _row_spec((layer, 0), D_MODEL, 2),
            _row_spec((layer, 0), D_MODEL, 2),
        ],
        out_specs=seq.tile_spec(D_MODEL),
        out_shape=jax.ShapeDtypeStruct((seq.n_rows, D_MODEL), F32),
        compiler_params=_params(1),
        name="odd_out",
    )(ch, ch, ch, bg, fo, x, ada5, conv_w, conv_b, w_out, ln_g, ln_b)


def _ffn_kernel(seq, x_ref, xp_ref, xn_ref, sc_ref, sh_ref, gate_ref, wa_ref, wg_ref, cw_ref, cb_ref, wd_ref,
                g_ref, b_ref, o_ref):
    has_prev, has_next = seq.edge_flags()
    tile_m = seq.tile_m
    halo = xp_ref.shape[0]
    x = x_ref[...]
    n_ext = tile_m + 2 * halo
    x_ext = jnp.concatenate([xp_ref[...], x, xn_ref[...]], axis=0)
    u_ext = (_layer_norm(x_ext) * (1.0 + sc_ref[...]) + sh_ref[...]).astype(BF16)
    u = u_ext[halo:halo + tile_m]
    row = lax.broadcasted_iota(jnp.int32, (n_ext, 1), 0)
    inside = ((row >= halo) | has_prev) & ((row < halo + tile_m) | has_next)
    width = D_FF // FFN_CHUNKS
    y = None
    for c in range(FFN_CHUNKS):
        cols = slice(c * width, (c + 1) * width)
        a_ext = jnp.where(inside, _dot(u_ext, wa_ref[:, cols]), 0.0)
        up = pltpu.roll(a_ext, 1, 0)[halo:halo + tile_m]
        mid = a_ext[halo:halo + tile_m]
        down = pltpu.roll(a_ext, n_ext - 1, 0)[halo:halo + tile_m]
        conv = cb_ref[:, cols] + up * cw_ref[0:1, cols] + mid * cw_ref[1:2, cols] + down * cw_ref[2:3, cols]
        hidden = (conv * jax.nn.sigmoid(conv) * _dot(u, wg_ref[:, cols])).astype(BF16)
        part = _dot(hidden, wd_ref[cols, :])
        y = part if y is None else y + part
    o_ref[...] = _post_norm(x, gate_ref[...], y, g_ref[...], b_ref[...])


def _ffn(seq, layer, x, ada5, w_up, ffn_conv_w, ffn_conv_b, w_down, ln_g, ln_b):
    prev_spec, next_spec = seq.halo_specs(D_MODEL, HALO_F32)
    return pl.pallas_call(
        functools.partial(_ffn_kernel, seq),
        grid=(seq.n_tiles,),
        in_specs=[
            seq.tile_spec(D_MODEL), prev_spec, next_spec,
            seq.mod_spec(layer, 4),
            seq.mod_spec(layer, 3),
            seq.mod_spec(layer, 5),
            _resident((None, D_MODEL, D_FF), lambda i: (layer, 0, 0)),
            _resident((None, D_MODEL, D_FF), lambda i: (layer, 0, 1)),
            pl.BlockSpec((None, 3, D_FF), lambda i: (layer, 0, 0)),
            _row_spec(layer, D_FF),
            _resident((None, D_FF, D_MODEL), lambda i: (layer, 0, 0)),
            _row_spec((layer, 1), D_MODEL, 2),
            _row_spec((layer, 1), D_MODEL, 2),
        ],
        out_specs=seq.tile_spec(D_MODEL),
        out_shape=jax.ShapeDtypeStruct((seq.n_rows, D_MODEL), F32),
        compiler_params=_params(1),
        name="conv_ffn",
    )(x, x, x, ada5, ada5, ada5, w_up, w_up, ffn_conv_w, ffn_conv_b, w_down, ln_g, ln_b)


def _rope_tables(seq_len):
    t = lax.iota(jnp.int32, seq_len)
    row = (t // GRID_W).astype(F32)
    col = (t % GRID_W).astype(F32)
    n_freq = HEAD_DIM // 4
    inv = 1.0 / (ROPE_THETA ** (jnp.arange(n_freq, dtype=F32) / n_freq))
    ang = jnp.concatenate([row[:, None] * inv, col[:, None] * inv], -1)
    cos = jnp.repeat(jnp.cos(ang), 2, axis=1)
    sin = jnp.repeat(jnp.sin(ang), 2, axis=1) * jnp.tile(jnp.array([-1.0, 1.0], F32), HEAD_DIM // 2)
    reps = LANES // HEAD_DIM
    return jnp.tile(cos, (1, reps)), jnp.tile(sin, (1, reps))


def _fourier_tables(seq_len):
    cl, sl = _dft_tables(seq_len)
    cc, sc = _dft_tables(FOURIER_GROUP)
    eye = jnp.eye(N_FOURIER_GROUPS, dtype=F32)
    return (cl.astype(BF16), sl.astype(BF16), jnp.kron(eye, cc).astype(BF16), jnp.kron(eye, sc).astype(BF16))


def _run_trunk(seq, x, ada5, ctx_k, ctx_v, wts):
    (w_in_even, gq, gk, bd, w_pool, pool_scale, w_in_odd, conv_w, conv_b, w_out, w_up, ffn_conv_w,
     ffn_conv_b, w_down, ln_g, ln_b) = wts
    rope_tabs = _rope_tables(seq.seq_len) if seq.latent else None
    four_tabs = _fourier_tables(seq.seq_len)
    ffn_seq = seq.with_tile(FFN_TILE_M) if seq.seq_len % FFN_TILE_M == 0 else seq
    new_k, new_v = [], []
    for layer in range(DEPTH):
        if layer % 2 == 0:
            e = layer // 2
            outs = _even_in(seq, layer, e, x, ada5, w_in_even, gq, gk, bd, rope_tabs)
            q, k, v, p = outs[:4]
            k = k.reshape(seq.n_batch, seq.seq_len, KV_WIDTH)
            v = v.reshape(seq.n_batch, seq.seq_len, KV_WIDTH)
            if seq.latent:
                k = jnp.concatenate([k, ctx_k[:, e]], axis=1)
                v = jnp.concatenate([v, ctx_v[:, e]], axis=1)
            else:
                new_k.append(outs[4])
                new_v.append(outs[5])
            n_keys = k.shape[1]
            k = k.reshape(seq.n_batch, n_keys, N_KV_HEADS, HEAD_DIM).transpose(0, 2, 1, 3)
            attn = _attention(seq, q, k, v.transpose(0, 2, 1))
            x = _even_out(seq, layer, e, attn, p, x, ada5, w_pool, pool_scale, w_out, ln_g, ln_b)
        else:
            o = layer // 2
            ch, bg, f = _odd_in(seq, layer, o, x, ada5, w_in_odd)
            fo = _fourier(seq, f, four_tabs)
            x = _odd_out(seq, layer, o, ch, bg, fo, x, ada5, conv_w, conv_b, w_out, ln_g, ln_b)
        x = _ffn(ffn_seq, layer, x, ada5, w_up, ffn_conv_w, ffn_conv_b, w_down, ln_g, ln_b)
    return x, new_k, new_v


def kernel(x_prompt, x_sample, cache_k, cache_v, c, c_ctx, w_ada, b_ada, w_in_even, q_norm_g, k_norm_g,
           w_pool, pool_scale, w_in_odd, conv_w, conv_b, w_out, w_up, ffn_conv_w, ffn_conv_b, w_down,
           ln_g, ln_b):
    n_prompt, prompt_len, _ = x_prompt.shape
    n_sample, sample_len, _ = x_sample.shape
    n_even = w_in_even.shape[0]
    n_odd = w_in_odd.shape[0]
    assert n_sample <= CTX_ROW

    cond = jnp.zeros((COND_ROWS, D_MODEL), F32).at[:n_sample].set(c).at[CTX_ROW].set(c_ctx)
    ada = _ada_all(cond, w_ada, b_ada)
    ada5 = ada.reshape(DEPTH, COND_ROWS, 6, 1, D_MODEL)

    head_of = lax.iota(jnp.int32, ATTN_WIDTH) // HEAD_DIM
    bd = (head_of[:, None] == head_of[None, :]).astype(BF16)
    wts = (
        w_in_even.astype(BF16),
        jnp.tile(q_norm_g, (1, N_HEADS)).reshape(n_even, 1, ATTN_WIDTH),
        jnp.tile(k_norm_g, (1, N_KV_HEADS)).reshape(n_even, 1, KV_WIDTH),
        bd,
        w_pool.astype(BF16),
        pool_scale.reshape(n_even, 1, POOL_WIDTH),
        w_in_odd.astype(BF16),
        conv_w,
        conv_b.reshape(n_odd, 1, CONV_WIDTH),
        w_out.astype(BF16),
        w_up.astype(BF16),
        ffn_conv_w,
        ffn_conv_b.reshape(DEPTH, 1, D_FF),
        w_down.astype(BF16),
        ln_g.reshape(DEPTH, 2, 1, D_MODEL),
        ln_b.reshape(DEPTH, 2, 1, D_MODEL),
    )

    prompt = _Seq(n_prompt, prompt_len, latent=False)
    y_prompt, ks, vs = _run_trunk(prompt, x_prompt.reshape(-1, D_MODEL), ada5, None, None, wts)
    cache_shape = (n_prompt, prompt_len, N_KV_HEADS, HEAD_DIM)
    new_cache_k = jnp.stack([k.reshape(cache_shape) for k in ks], 1)
    new_cache_v = jnp.stack([v.reshape(cache_shape) for v in vs], 1)

    sample = _Seq(n_sample, sample_len, latent=True)
    past_len = cache_k.shape[2]
    ctx_k = cache_k.reshape(n_sample, n_even, past_len, KV_WIDTH).astype(BF16)
    ctx_v = cache_v.reshape(n_sample, n_even, past_len, KV_WIDTH).astype(BF16)
    y_sample, _, _ = _run_trunk(sample, x_sample.reshape(-1, D_MODEL), ada5, ctx_k, ctx_v, wts)

    return (y_prompt.reshape(x_prompt.shape), y_sample.reshape(x_sample.shape), new_cache_k, new_cache_v)
```

```python
import functools
import math

import jax
import jax.numpy as jnp
from jax import lax
from jax.experimental import pallas as pl
from jax.experimental.pallas import tpu as pltpu

D_MODEL = 1024
DEPTH = 4
GRID_W = 64
N_HEADS = 8
N_KV_HEADS = 2
HEAD_DIM = 64
ATTN_WIDTH = N_HEADS * HEAD_DIM
KV_WIDTH = N_KV_HEADS * HEAD_DIM
POOL_WIDTH = D_MODEL - ATTN_WIDTH
POOL_WINDOWS = (2, 4, 8, 16)
POOL_GROUP = POOL_WIDTH // len(POOL_WINDOWS)
FOURIER_WIDTH = D_MODEL // 4
N_FOURIER_GROUPS = 4
FOURIER_GROUP = FOURIER_WIDTH // N_FOURIER_GROUPS
CONV_WIDTH = D_MODEL - FOURIER_WIDTH
D_FF = 2816
ROPE_THETA = 10000.0
LN_EPS = 1e-6
RMS_EPS = 1e-6
IN_EVEN = ATTN_WIDTH + 2 * KV_WIDTH + POOL_WIDTH
IN_ODD = 3 * CONV_WIDTH + FOURIER_WIDTH
DEEPNORM_ALPHA = (2 * DEPTH) ** 0.25

SUBLANES = 8
LANES = 128
VMEM_LIMIT_BYTES = 56 * 1024 * 1024

TILE_M = 1024
FFN_TILE_M = 512
SUB_M = 256
ATTN_TILE_M = 512
ATTN_SUB_M = 256
ATTN_ONES_ROWS = 16
HALO_F32 = SUBLANES
HALO_BF16 = 2 * SUBLANES
COND_ROWS = 16
CTX_ROW = 8

F32 = jnp.float32
BF16 = jnp.bfloat16


def _params(n_axes):
    return pltpu.CompilerParams(dimension_semantics=("arbitrary",) * n_axes,
                                vmem_limit_bytes=VMEM_LIMIT_BYTES)


def _resident(block_shape, index_map):
    return pl.BlockSpec(block_shape, index_map, pipeline_mode=pl.Buffered(1))


def _dot(a, b):
    return jnp.dot(a, b, preferred_element_type=F32)


def _layer_norm(x):
    mu = jnp.mean(x, axis=-1, keepdims=True)
    xc = x - mu
    var = jnp.mean(xc * xc, axis=-1, keepdims=True)
    return xc * lax.rsqrt(var + LN_EPS)


def _split_dot(t, w):
    hi = t.astype(BF16)
    lo = (t - hi.astype(F32)).astype(BF16)
    return _dot(hi, w) + _dot(lo, w)


def _staggered(n, pre, mid, post):
    state = pre(0)
    done = None
    for s in range(n):
        cur = mid(s, state)
        if s + 1 < n:
            state = pre(s + 1)
        if done is not None:
            post(s - 1, done)
        done = cur
    post(n - 1, done)


def _ada_kernel(cond_ref, w_ref, b_ref, o_ref):
    cnd = cond_ref[...]
    act = (cnd * jax.nn.sigmoid(cnd)).astype(BF16)
    o_ref[...] = _dot(act, w_ref[...].astype(BF16)) + b_ref[...]


def _ada_all(cond, w_ada, b_ada):
    tn = 1536
    n_out = 6 * D_MODEL
    return pl.pallas_call(
        _ada_kernel,
        grid=(DEPTH, n_out // tn),
        in_specs=[
            pl.BlockSpec((COND_ROWS, D_MODEL), lambda l, j: (0, 0)),
            pl.BlockSpec((None, D_MODEL, tn), lambda l, j: (l, 0, j)),
            pl.BlockSpec((None, 1, tn), lambda l, j: (l, 0, j)),
        ],
        out_specs=pl.BlockSpec((None, COND_ROWS, tn), lambda l, j: (l, 0, j)),
        out_shape=jax.ShapeDtypeStruct((DEPTH, COND_ROWS, n_out), F32),
        compiler_params=_params(2),
        name="ada",
    )(cond, w_ada, b_ada.reshape(DEPTH, 1, n_out))


class _Seq:
    def __init__(self, n_batch, seq_len, latent, tile_m=TILE_M, sub_m=SUB_M):
        self.n_rows = n_batch * seq_len
        assert seq_len % sub_m == 0 and tile_m % sub_m == 0 and self.n_rows % tile_m == 0
        assert seq_len % tile_m == 0 or not latent
        self.n_batch = n_batch
        self.seq_len = seq_len
        self.latent = latent
        self.tile_m = tile_m
        self.sub_m = sub_m
        self.n_sub = tile_m // sub_m
        self.subs_per_seq = seq_len // sub_m
        self.n_tiles = self.n_rows // tile_m

    def with_tile(self, tile_m, sub_m):
        return _Seq(self.n_batch, self.seq_len, self.latent, tile_m, sub_m)

    def cond_row(self, i):
        return (i * self.tile_m) // self.seq_len if self.latent else CTX_ROW

    def tile_spec(self, width):
        return pl.BlockSpec((self.tile_m, width), lambda i: (i, 0))

    def halo_specs(self, width, halo):
        per_tile = self.tile_m // halo
        last = self.n_rows // halo - 1
        prev = pl.BlockSpec((halo, width), lambda i: (jnp.maximum(i * per_tile - 1, 0), 0))
        nxt = pl.BlockSpec((halo, width), lambda i: (jnp.minimum((i + 1) * per_tile, last), 0))
        return prev, nxt

    def mod_spec(self, layer, which):
        return pl.BlockSpec((None, None, None, 1, D_MODEL),
                            lambda i: (layer, self.cond_row(i), which, 0, 0))

    def sub_index(self, s):
        return (pl.program_id(0) * self.n_sub + s) % self.subs_per_seq

    def sub_edges(self, s):
        if self.subs_per_seq == 1:
            return False, False
        j = self.sub_index(s)
        return j > 0, j < self.subs_per_seq - 1

    def sub_rows(self, s):
        return slice(s * self.sub_m, (s + 1) * self.sub_m)

    def neighbours(self, s, ref, prev_ref, next_ref):
        halo = prev_ref.shape[0]
        lo = ref[s * self.sub_m - halo:s * self.sub_m, :] if s > 0 else prev_ref[...]
        hi = (ref[(s + 1) * self.sub_m:(s + 1) * self.sub_m + halo, :] if s < self.n_sub - 1
              else next_ref[...])
        return lo, hi


def _row_spec(layer, width, n_lead=1):
    if n_lead == 1:
        return pl.BlockSpec((None, 1, width), lambda i: (layer, 0, 0))
    return pl.BlockSpec((None, None, 1, width), lambda i: (layer[0], layer[1], 0, 0))


def _shift_rows(cur, prev_row, next_row):
    n = cur.shape[0]
    row = lax.broadcasted_iota(jnp.int32, (n, 1), 0)
    up = jnp.where(row == 0, prev_row, pltpu.roll(cur, 1, 0))
    down = jnp.where(row == n - 1, next_row, pltpu.roll(cur, n - 1, 0))
    return up, down


def _post_norm(x, gate, y, g, b):
    return _layer_norm(DEEPNORM_ALPHA * x + gate * y) * g + b


def _even_in_kernel(seq, x_ref, sc_ref, sh_ref, w_ref, gq_ref, gk_ref, bd_ref, *rest):
    if seq.latent:
        cos_ref, sin_ref, q_out, k_out, v_out, p_out = rest
    else:
        q_out, k_out, v_out, p_out, kraw_out, vraw_out = rest
    o1 = ATTN_WIDTH
    o2 = o1 + KV_WIDTH
    o3 = o2 + KV_WIDTH
    even_lane = (lax.broadcasted_iota(jnp.int32, (seq.sub_m, LANES), 1) & 1) == 0

    def rope(t, cos, sin):
        outs = []
        for j in range(t.shape[1] // LANES):
            slab = t[:, j * LANES:(j + 1) * LANES]
            partner = jnp.where(even_lane, pltpu.roll(slab, LANES - 1, 1), pltpu.roll(slab, 1, 1))
            outs.append(slab * cos + partner * sin)
        return outs[0] if len(outs) == 1 else jnp.concatenate(outs, axis=1)

    def pre(s):
        return (_layer_norm(x_ref[seq.sub_rows(s), :]) * (1.0 + sc_ref[...]) + sh_ref[...]).astype(BF16)

    def mid(s, u):
        return _dot(u, w_ref[...])

    def post(s, proj):
        rows = seq.sub_rows(s)
        q = proj[:, :o1]
        k = proj[:, o1:o2]
        v = proj[:, o2:o3]
        p_out[rows, :] = proj[:, o3:].astype(BF16)
        bd = bd_ref[...]
        q = q * lax.rsqrt(_split_dot(q * q, bd) * (1.0 / HEAD_DIM) + RMS_EPS) * gq_ref[...]
        k = k * lax.rsqrt(_split_dot(k * k, bd[:KV_WIDTH, :KV_WIDTH]) * (1.0 / HEAD_DIM) + RMS_EPS) * gk_ref[...]
        if seq.latent:
            cos = cos_ref[rows, :]
            sin = sin_ref[rows, :]
            q = rope(q, cos, sin)
            k = rope(k, cos, sin)
        else:
            kraw_out[rows, :] = k
            vraw_out[rows, :] = v
        q_out[rows, :] = (q * (HEAD_DIM ** -0.5 * math.log2(math.e))).astype(BF16)
        k_out[rows, :] = k.astype(BF16)
        v_out[rows, :] = v.astype(BF16)

    _staggered(seq.n_sub, pre, mid, post)


def _even_in(seq, layer, e, x, ada5, w_in, gq, gk, bd, rope_tabs):
    n = seq.n_rows
    in_specs = [
        seq.tile_spec(D_MODEL),
        seq.mod_spec(layer, 1),
        seq.mod_spec(layer, 0),
        _resident((None, D_MODEL, IN_EVEN), lambda i: (e, 0, 0)),
        _row_spec(e, ATTN_WIDTH),
        _row_spec(e, KV_WIDTH),
        _resident((ATTN_WIDTH, ATTN_WIDTH), lambda i: (0, 0)),
    ]
    args = [x, ada5, ada5, w_in, gq, gk, bd]
    out_specs = [seq.tile_spec(ATTN_WIDTH), seq.tile_spec(KV_WIDTH), seq.tile_spec(KV_WIDTH),
                 seq.tile_spec(POOL_WIDTH)]
    out_shape = [jax.ShapeDtypeStruct((n, ATTN_WIDTH), BF16), jax.ShapeDtypeStruct((n, KV_WIDTH), BF16),
                 jax.ShapeDtypeStruct((n, KV_WIDTH), BF16), jax.ShapeDtypeStruct((n, POOL_WIDTH), BF16)]
    if seq.latent:
        tiles_per_seq = seq.seq_len // seq.tile_m
        in_specs += [pl.BlockSpec((seq.tile_m, LANES), lambda i: (i % tiles_per_seq, 0))] * 2
        args += list(rope_tabs)
    else:
        out_specs += [seq.tile_spec(KV_WIDTH)] * 2
        out_shape += [jax.ShapeDtypeStruct((n, KV_WIDTH), F32)] * 2
    return pl.pallas_call(
        functools.partial(_even_in_kernel, seq),
        grid=(seq.n_tiles,),
        in_specs=in_specs,
        out_specs=out_specs,
        out_shape=out_shape,
        compiler_params=_params(1),
        name="even_in",
    )(*args)


def _reduce_rows(x, op, final, chunk=256):
    n = x.shape[0]
    if n > chunk and n % chunk == 0:
        parts = [x[i:i + chunk] for i in range(0, n, chunk)]
        while len(parts) > 1:
            parts = [op(parts[i], parts[i + 1]) if i + 1 < len(parts) else parts[i]
                     for i in range(0, len(parts), 2)]
        x = parts[0]
        n = chunk
    while n > SUBLANES and n % (2 * SUBLANES) == 0:
        n //= 2
        x = op(x[:n], x[n:])
    return final(x, axis=0, keepdims=True)


def _attn_kernel(q_ref, k_ref, vt_ref, o_ref):
    group = N_HEADS // N_KV_HEADS
    heads_per_dot = 2
    tq = ATTN_SUB_M
    stages = [(r, h0) for r in range(q_ref.shape[0] // tq) for h0 in range(0, N_HEADS, heads_per_dot)]

    def scores(stage):
        r, h0 = stage
        qs = jnp.concatenate([q_ref[r * tq:(r + 1) * tq, h * HEAD_DIM:(h + 1) * HEAD_DIM]
                              for h in range(h0, h0 + heads_per_dot)], axis=0)
        return lax.dot_general(k_ref[h0 // group], qs, (((1,), (1,)), ((), ())), preferred_element_type=F32)

    outs = []
    st_next = scores(stages[0])
    for i, (r, h0) in enumerate(stages):
        st = st_next
        if i + 1 < len(stages):
            st_next = scores(stages[i + 1])
        g = h0 // group
        m = _reduce_rows(st, jnp.maximum, jnp.max)
        p = jnp.exp2(st - m).astype(BF16)
        ot = _dot(vt_ref[g], p)
        ot = ot[:HEAD_DIM] / ot[HEAD_DIM:HEAD_DIM + 1]
        outs += [ot[:, j * tq:(j + 1) * tq] for j in range(heads_per_dot)]
        if h0 + heads_per_dot == N_HEADS:
            o_ref[r * tq:(r + 1) * tq, :] = jnp.concatenate(outs, axis=0).T.astype(BF16)
            outs = []


def _attention(seq, q, k, v):
    n_keys = k.shape[2]
    vt = v.transpose(0, 2, 1).reshape(seq.n_batch, N_KV_HEADS, HEAD_DIM, n_keys)
    ones = jnp.ones((seq.n_batch, N_KV_HEADS, ATTN_ONES_ROWS, n_keys), BF16)
    vt = jnp.concatenate([vt, ones], axis=2)
    tq = ATTN_TILE_M if seq.seq_len % ATTN_TILE_M == 0 else ATTN_SUB_M
    tps = seq.seq_len // tq
    return pl.pallas_call(
        _attn_kernel,
        grid=(seq.n_batch, tps),
        in_specs=[
            pl.BlockSpec((tq, ATTN_WIDTH), lambda b, j: (b * tps + j, 0)),
            pl.BlockSpec((None, N_KV_HEADS, n_keys, HEAD_DIM), lambda b, j: (b, 0, 0, 0)),
            pl.BlockSpec((None, N_KV_HEADS, HEAD_DIM + ATTN_ONES_ROWS, n_keys), lambda b, j: (b, 0, 0, 0)),
        ],
        out_specs=pl.BlockSpec((tq, ATTN_WIDTH), lambda b, j: (b * tps + j, 0)),
        out_shape=jax.ShapeDtypeStruct((seq.n_rows, ATTN_WIDTH), BF16),
        compiler_params=_params(2),
        name="attention",
    )(q, k, vt)


def _even_out_kernel(seq, attn_ref, p_ref, pp_ref, pn_ref, x_ref, gate_ref, wp_ref, ps_ref, wo_ref,
                     g_ref, b_ref, o_ref):
    sub = seq.sub_m
    halo = pp_ref.shape[0]
    n_ext = sub + 2 * halo
    row = lax.broadcasted_iota(jnp.int32, (sub, 1), 0)

    def pre(s):
        has_prev, has_next = seq.sub_edges(s)
        lo, hi = seq.neighbours(s, p_ref, pp_ref, pn_ref)
        p = p_ref[seq.sub_rows(s), :].astype(F32)
        ext = jnp.concatenate([jnp.where(has_prev, lo.astype(F32), 0.0), p,
                               jnp.where(has_next, hi.astype(F32), 0.0)], axis=0)
        pos = seq.sub_index(s) * sub + row
        mixed = []
        for gi, w in enumerate(POOL_WINDOWS):
            half = w // 2
            lanes = slice(gi * POOL_GROUP, (gi + 1) * POOL_GROUP)
            run = ext[:, lanes]
            span = 1
            while span < w:
                run = run + pltpu.roll(run, span, 0)
                span *= 2
            if half > 1:
                run = pltpu.roll(run, n_ext - (half - 1), 0)
            total = run[halo:halo + sub]
            cnt = jnp.minimum(pos + half, seq.seq_len) - jnp.maximum(pos - half, 0)
            centred = total / cnt.astype(F32) - p[:, lanes]
            mixed.append(_dot(centred.astype(BF16), wp_ref[gi]))
        pool = jnp.concatenate(mixed, axis=1) * ps_ref[...]
        return jnp.concatenate([attn_ref[seq.sub_rows(s), :], pool.astype(BF16)], axis=1)

    def mid(s, m):
        return _dot(m, wo_ref[...])

    def post(s, y):
        rows = seq.sub_rows(s)
        o_ref[rows, :] = _post_norm(x_ref[rows, :], gate_ref[...], y, g_ref[...], b_ref[...])

    _staggered(seq.n_sub, pre, mid, post)


def _even_out(seq, layer, e, attn, p, x, ada5, w_pool, pool_scale, w_out, ln_g, ln_b):
    prev_spec, next_spec = seq.halo_specs(POOL_WIDTH, HALO_BF16)
    return pl.pallas_call(
        functools.partial(_even_out_kernel, seq),
        grid=(seq.n_tiles,),
        in_specs=[
            seq.tile_spec(ATTN_WIDTH),
            seq.tile_spec(POOL_WIDTH), prev_spec, next_spec,
            seq.tile_spec(D_MODEL),
            seq.mod_spec(layer, 2),
            _resident((None, len(POOL_WINDOWS), POOL_GROUP, POOL_GROUP), lambda i: (e, 0, 0, 0)),
            _row_spec(e, POOL_WIDTH),
            _resident((None, D_MODEL, D_MODEL), lambda i: (layer, 0, 0)),
            _row_spec((layer, 0), D_MODEL, 2),
            _row_spec((layer, 0), D_MODEL, 2),
        ],
        out_specs=seq.tile_spec(D_MODEL),
        out_shape=jax.ShapeDtypeStruct((seq.n_rows, D_MODEL), F32),
        compiler_params=_params(1),
        name="even_out",
    )(attn, p, p, p, x, ada5, w_pool, pool_scale, w_out, ln_g, ln_b)


def _odd_in_kernel(seq, x_ref, sc_ref, sh_ref, w_ref, ch_out, bg_out, f_out):
    c = CONV_WIDTH

    def pre(s):
        return (_layer_norm(x_ref[seq.sub_rows(s), :]) * (1.0 + sc_ref[...]) + sh_ref[...]).astype(BF16)

    def mid(s, u):
        return _dot(u, w_ref[...])

    def post(s, proj):
        rows = seq.sub_rows(s)
        ch_out[rows, :] = (proj[:, 2 * c:3 * c] * proj[:, :c]).astype(BF16)
        bg_out[rows, :] = proj[:, c:2 * c].astype(BF16)
        f_out[rows, :] = proj[:, 3 * c:].astype(BF16)

    _staggered(seq.n_sub, pre, mid, post)


def _odd_in(seq, layer, o, x, ada5, w_in):
    n = seq.n_rows
    return pl.pallas_call(
        functools.partial(_odd_in_kernel, seq),
        grid=(seq.n_tiles,),
        in_specs=[
            seq.tile_spec(D_MODEL),
            seq.mod_spec(layer, 1),
            seq.mod_spec(layer, 0),
            _resident((None, D_MODEL, IN_ODD), lambda i: (o, 0, 0)),
        ],
        out_specs=[seq.tile_spec(CONV_WIDTH), seq.tile_spec(CONV_WIDTH), seq.tile_spec(FOURIER_WIDTH)],
        out_shape=[jax.ShapeDtypeStruct((n, CONV_WIDTH), BF16), jax.ShapeDtypeStruct((n, CONV_WIDTH), BF16),
                   jax.ShapeDtypeStruct((n, FOURIER_WIDTH), BF16)],
        compiler_params=_params(1),
        name="odd_in",
    )(x, ada5, ada5, w_in)


def _fourier_kernel(scale, f_ref, cl_ref, sl_ref, cc_ref, sc_ref, o_ref):
    fb = f_ref[...]
    g_cos = (_dot(fb, cc_ref[...]) * scale).astype(BF16)
    g_sin = (_dot(fb, sc_ref[...]) * scale).astype(BF16)
    o_ref[...] = (_dot(cl_ref[...], g_cos) - _dot(sl_ref[...], g_sin)).astype(BF16)


def _dft_tables(n):
    split = FOURIER_GROUP
    assert n % split == 0
    k = lax.iota(jnp.int32, n)[None, :]
    unit = 2.0 * math.pi / n
    ang_hi = (((lax.iota(jnp.int32, n // split)[:, None] * split) * k) % n).astype(F32) * unit
    ang_lo = ((lax.iota(jnp.int32, split)[:, None] * k) % n).astype(F32) * unit
    ch, sh = jnp.cos(ang_hi)[:, None, :], jnp.sin(ang_hi)[:, None, :]
    cl, sl = jnp.cos(ang_lo)[None, :, :], jnp.sin(ang_lo)[None, :, :]
    return (ch * cl - sh * sl).reshape(n, n), (sh * cl + ch * sl).reshape(n, n)


def _fourier(seq, f, tabs):
    L = seq.seq_len
    cl, sl, cc, sc = tabs
    scale = 1.0 / math.sqrt(L * FOURIER_GROUP)
    return pl.pallas_call(
        functools.partial(_fourier_kernel, scale),
        grid=(seq.n_batch,),
        in_specs=[
            pl.BlockSpec((L, FOURIER_WIDTH), lambda b: (b, 0)),
            _resident((L, L), lambda b: (0, 0)),
            _resident((L, L), lambda b: (0, 0)),
            _resident((FOURIER_WIDTH, FOURIER_WIDTH), lambda b: (0, 0)),
            _resident((FOURIER_WIDTH, FOURIER_WIDTH), lambda b: (0, 0)),
        ],
        out_specs=pl.BlockSpec((L, FOURIER_WIDTH), lambda b: (b, 0)),
        out_shape=jax.ShapeDtypeStruct((seq.n_rows, FOURIER_WIDTH), BF16),
        compiler_params=_params(1),
        name="fourier",
    )(f, cl, sl, cc, sc)


def _odd_out_kernel(seq, ch_ref, chp_ref, chn_ref, bg_ref, fo_ref, x_ref, gate_ref, cw_ref, cb_ref, wo_ref,
                    g_ref, b_ref, o_ref):
    halo = chp_ref.shape[0]

    def pre(s):
        has_prev, has_next = seq.sub_edges(s)
        rows = seq.sub_rows(s)
        lo, hi = seq.neighbours(s, ch_ref, chp_ref, chn_ref)
        ch = ch_ref[rows, :].astype(F32)
        prev_row = jnp.where(has_prev, lo.astype(F32)[halo - 1:halo, :], 0.0)
        next_row = jnp.where(has_next, hi.astype(F32)[0:1, :], 0.0)
        up, down = _shift_rows(ch, prev_row, next_row)
        conv = cb_ref[...] + up * cw_ref[0:1, :] + ch * cw_ref[1:2, :] + down * cw_ref[2:3, :]
        conv_out = bg_ref[rows, :].astype(F32) * conv
        return jnp.concatenate([conv_out.astype(BF16), fo_ref[rows, :]], axis=1)

    def mid(s, m):
        return _dot(m, wo_ref[...])

    def post(s, y):
        rows = seq.sub_rows(s)
        o_ref[rows, :] = _post_norm(x_ref[rows, :], gate_ref[...], y, g_ref[...], b_ref[...])

    _staggered(seq.n_sub, pre, mid, post)


def _odd_out(seq, layer, o, ch, bg, fo, x, ada5, conv_w, conv_b, w_out, ln_g, ln_b):
    prev_spec, next_spec = seq.halo_specs(CONV_WIDTH, HALO_BF16)
    return pl.pallas_call(
        functools.partial(_odd_out_kernel, seq),
        grid=(seq.n_tiles,),
        in_specs=[
            seq.tile_spec(CONV_WIDTH), prev_spec, next_spec,
            seq.tile_spec(CONV_WIDTH),
            seq.tile_spec(FOURIER_WIDTH),
            seq.tile_spec(D_MODEL),
            seq.mod_spec(layer, 2),
            pl.BlockSpec((None, 3, CONV_WIDTH), lambda i: (o, 0, 0)),
            _row_spec(o, CONV_WIDTH),
            _resident((None, D_MODEL, D_MODEL), lambda i: (layer, 0, 0)),
            _row_spec((layer, 0), D_MODEL, 2),
            _row_spec((layer, 0), D_MODEL, 2),
        ],
        out_specs=seq.tile_spec(D_MODEL),
        out_shape=jax.ShapeDtypeStruct((seq.n_rows, D_MODEL), F32),
        compiler_params=_params(1),
        name="odd_out",
    )(ch, ch, ch, bg, fo, x, ada5, conv_w, conv_b, w_out, ln_g, ln_b)


def _ffn_kernel(seq, x_ref, xp_ref, xn_ref, sc_ref, sh_ref, gate_ref, wa_ref, wg_ref, cw_ref, cb_ref, wd_ref,
                g_ref, b_ref, o_ref):
    halo = xp_ref.shape[0]
    sub = seq.sub_m
    n_sub = seq.n_sub
    n_ext = sub + 2 * halo
    row = lax.broadcasted_iota(jnp.int32, (n_ext, 1), 0)

    def stage_up(s):
        has_prev, has_next = seq.sub_edges(s)
        lo, hi = seq.neighbours(s, x_ref, xp_ref, xn_ref)
        x = x_ref[seq.sub_rows(s), :]
        u_ext = (_layer_norm(jnp.concatenate([lo, x, hi], axis=0)) * (1.0 + sc_ref[...]) + sh_ref[...]).astype(BF16)
        a_ext = _dot(u_ext, wa_ref[...])
        gate_lin = _dot(u_ext[halo:halo + sub], wg_ref[...])
        inside = ((row >= halo) | has_prev) & ((row < halo + sub) | has_next)
        return x, jnp.where(inside, a_ext, 0.0), gate_lin

    def stage_hidden(a_ext, gate_lin):
        up = pltpu.roll(a_ext, 1, 0)[halo:halo + sub]
        mid = a_ext[halo:halo + sub]
        down = pltpu.roll(a_ext, n_ext - 1, 0)[halo:halo + sub]
        conv = cb_ref[...] + up * cw_ref[0:1, :] + mid * cw_ref[1:2, :] + down * cw_ref[2:3, :]
        return (conv * jax.nn.sigmoid(conv) * gate_lin).astype(BF16)

    def stage_out(s, x, y):
        o_ref[seq.sub_rows(s), :] = _post_norm(x, gate_ref[...], y, g_ref[...], b_ref[...])

    ups = {0: stage_up(0)}
    ys = {}
    for s in range(n_sub):
        if s + 1 < n_sub:
            ups[s + 1] = stage_up(s + 1)
        x, a_ext, gate_lin = ups.pop(s)
        h = stage_hidden(a_ext, gate_lin)
        if s > 0:
            stage_out(s - 1, *ys.pop(s - 1))
        ys[s] = (x, _dot(h, wd_ref[...]))
    stage_out(n_sub - 1, *ys.pop(n_sub - 1))


def _ffn(seq, layer, x, ada5, w_up, ffn_conv_w, ffn_conv_b, w_down, ln_g, ln_b):
    prev_spec, next_spec = seq.halo_specs(D_MODEL, HALO_F32)
    return pl.pallas_call(
        functools.partial(_ffn_kernel, seq),
        grid=(seq.n_tiles,),
        in_specs=[
            seq.tile_spec(D_MODEL), prev_spec, next_spec,
            seq.mod_spec(layer, 4),
            seq.mod_spec(layer, 3),
            seq.mod_spec(layer, 5),
            _resident((None, D_MODEL, D_FF), lambda i: (layer, 0, 0)),
            _resident((None, D_MODEL, D_FF), lambda i: (layer, 0, 1)),
            pl.BlockSpec((None, 3, D_FF), lambda i: (layer, 0, 0)),
            _row_spec(layer, D_FF),
            _resident((None, D_FF, D_MODEL), lambda i: (layer, 0, 0)),
            _row_spec((layer, 1), D_MODEL, 2),
            _row_spec((layer, 1), D_MODEL, 2),
        ],
        out_specs=seq.tile_spec(D_MODEL),
        out_shape=jax.ShapeDtypeStruct((seq.n_rows, D_MODEL), F32),
        compiler_params=_params(1),
        name="conv_ffn",
    )(x, x, x, ada5, ada5, ada5, w_up, w_up, ffn_conv_w, ffn_conv_b, w_down, ln_g, ln_b)


def _rope_tables(seq_len):
    t = lax.iota(jnp.int32, seq_len)
    row = (t // GRID_W).astype(F32)
    col = (t % GRID_W).astype(F32)
    n_freq = HEAD_DIM // 4
    inv = 1.0 / (ROPE_THETA ** (jnp.arange(n_freq, dtype=F32) / n_freq))
    ang = jnp.concatenate([row[:, None] * inv, col[:, None] * inv], -1)
    cos = jnp.repeat(jnp.cos(ang), 2, axis=1)
    sin = jnp.repeat(jnp.sin(ang), 2, axis=1) * jnp.tile(jnp.array([-1.0, 1.0], F32), HEAD_DIM // 2)
    reps = LANES // HEAD_DIM
    return jnp.tile(cos, (1, reps)), jnp.tile(sin, (1, reps))


def _fourier_tables(seq_len):
    cl, sl = _dft_tables(seq_len)
    cc, sc = _dft_tables(FOURIER_GROUP)
    eye = jnp.eye(N_FOURIER_GROUPS, dtype=F32)
    return (cl.astype(BF16), sl.astype(BF16), jnp.kron(eye, cc).astype(BF16), jnp.kron(eye, sc).astype(BF16))


def _run_trunk(seq, x, ada5, ctx_k, ctx_v, wts):
    (w_in_even, gq, gk, bd, w_pool, pool_scale, w_in_odd, conv_w, conv_b, w_out, w_up, ffn_conv_w,
     ffn_conv_b, w_down, ln_g, ln_b) = wts
    rope_tabs = _rope_tables(seq.seq_len) if seq.latent else None
    four_tabs = _fourier_tables(seq.seq_len)
    new_k, new_v = [], []
    for layer in range(DEPTH):
        if layer % 2 == 0:
            e = layer // 2
            outs = _even_in(seq, layer, e, x, ada5, w_in_even, gq, gk, bd, rope_tabs)
            q, k, v, p = outs[:4]
            k = k.reshape(seq.n_batch, seq.seq_len, KV_WIDTH)
            v = v.reshape(seq.n_batch, seq.seq_len, KV_WIDTH)
            if seq.latent:
                k = jnp.concatenate([k, ctx_k[:, e]], axis=1)
                v = jnp.concatenate([v, ctx_v[:, e]], axis=1)
            else:
                new_k.append(outs[4])
                new_v.append(outs[5])
            n_keys = k.shape[1]
            k = k.reshape(seq.n_batch, n_keys, N_KV_HEADS, HEAD_DIM).transpose(0, 2, 1, 3)
            attn = _attention(seq, q, k, v)
            x = _even_out(seq, layer, e, attn, p, x, ada5, w_pool, pool_scale, w_out, ln_g, ln_b)
        else:
            o = layer // 2
            ch, bg, f = _odd_in(seq, layer, o, x, ada5, w_in_odd)
            fo = _fourier(seq, f, four_tabs)
            x = _odd_out(seq, layer, o, ch, bg, fo, x, ada5, conv_w, conv_b, w_out, ln_g, ln_b)
        x = _ffn(seq.with_tile(FFN_TILE_M, SUB_M), layer, x, ada5, w_up, ffn_conv_w, ffn_conv_b, w_down,
                 ln_g, ln_b)
    return x, new_k, new_v


def kernel(x_prompt, x_sample, cache_k, cache_v, c, c_ctx, w_ada, b_ada, w_in_even, q_norm_g, k_norm_g,
           w_pool, pool_scale, w_in_odd, conv_w, conv_b, w_out, w_up, ffn_conv_w, ffn_conv_b, w_down,
           ln_g, ln_b):
    n_prompt, prompt_len, _ = x_prompt.shape
    n_sample, sample_len, _ = x_sample.shape
    n_even = w_in_even.shape[0]
    n_odd = w_in_odd.shape[0]
    assert n_sample <= CTX_ROW

    cond = jnp.zeros((COND_ROWS, D_MODEL), F32).at[:n_sample].set(c).at[CTX_ROW].set(c_ctx)
    ada = _ada_all(cond, w_ada, b_ada)
    ada5 = ada.reshape(DEPTH, COND_ROWS, 6, 1, D_MODEL)

    head_of = lax.iota(jnp.int32, ATTN_WIDTH) // HEAD_DIM
    bd = (head_of[:, None] == head_of[None, :]).astype(BF16)
    wts = (
        w_in_even.astype(BF16),
        jnp.tile(q_norm_g, (1, N_HEADS)).reshape(n_even, 1, ATTN_WIDTH),
        jnp.tile(k_norm_g, (1, N_KV_HEADS)).reshape(n_even, 1, KV_WIDTH),
        bd,
        w_pool.astype(BF16),
        pool_scale.reshape(n_even, 1, POOL_WIDTH),
        w_in_odd.astype(BF16),
        conv_w,
        conv_b.reshape(n_odd, 1, CONV_WIDTH),
        w_out.astype(BF16),
        w_up.astype(BF16),
        ffn_conv_w,
        ffn_conv_b.reshape(DEPTH, 1, D_FF),
        w_down.astype(BF16),
        ln_g.reshape(DEPTH, 2, 1, D_MODEL),
        ln_b.reshape(DEPTH, 2, 1, D_MODEL),
    )

    prompt = _Seq(n_prompt, prompt_len, latent=False)
    y_prompt, ks, vs = _run_trunk(prompt, x_prompt.reshape(-1, D_MODEL), ada5, None, None, wts)
    cache_shape = (n_prompt, prompt_len, N_KV_HEADS, HEAD_DIM)
    new_cache_k = jnp.stack([k.reshape(cache_shape) for k in ks], 1)
    new_cache_v = jnp.stack([v.reshape(cache_shape) for v in vs], 1)

    sample = _Seq(n_sample, sample_len, latent=True)
    past_len = cache_k.shape[2]
    ctx_k = cache_k.reshape(n_sample, n_even, past_len, KV_WIDTH).astype(BF16)
    ctx_v = cache_v.reshape(n_sample, n_even, past_len, KV_WIDTH).astype(BF16)
    y_sample, _, _ = _run_trunk(sample, x_sample.reshape(-1, D_MODEL), ada5, ctx_k, ctx_v, wts)

    return (y_prompt.reshape(x_prompt.shape), y_sample.reshape(x_sample.shape), new_cache_k, new_cache_v)
```

```python
import functools
import math

import jax
import jax.numpy as jnp
from jax import lax
from jax.experimental import pallas as pl
from jax.experimental.pallas import tpu as pltpu

D_MODEL = 1024
DEPTH = 4
GRID_W = 64
N_HEADS = 8
N_KV_HEADS = 2
HEAD_DIM = 64
ATTN_WIDTH = N_HEADS * HEAD_DIM
KV_WIDTH = N_KV_HEADS * HEAD_DIM
POOL_WIDTH = D_MODEL - ATTN_WIDTH
POOL_WINDOWS = (2, 4, 8, 16)
POOL_GROUP = POOL_WIDTH // len(POOL_WINDOWS)
FOURIER_WIDTH = D_MODEL // 4
N_FOURIER_GROUPS = 4
FOURIER_GROUP = FOURIER_WIDTH // N_FOURIER_GROUPS
CONV_WIDTH = D_MODEL - FOURIER_WIDTH
D_FF = 2816
ROPE_THETA = 10000.0
LN_EPS = 1e-6
RMS_EPS = 1e-6
IN_EVEN = ATTN_WIDTH + 2 * KV_WIDTH + POOL_WIDTH
IN_ODD = 3 * CONV_WIDTH + FOURIER_WIDTH
DEEPNORM_ALPHA = (2 * DEPTH) ** 0.25
Q_SCALE = HEAD_DIM ** -0.5 * math.log2(math.e)

SUBLANES = 8
LANES = 128
VMEM_LIMIT_BYTES = 56 * 1024 * 1024

TILE_M = 1024
FFN_TILE_M = 512
SUB_M = 256
ATTN_TILE_M = 512
ATTN_SUB_M = 256
ATTN_ONES_ROWS = 16
ATTN_HEADS_PER_DOT = 2
ATTN_KEY_CHUNK = 256
ATTN_SCORE_BOUND = 80.0
HALO_F32 = SUBLANES
HALO_BF16 = 2 * SUBLANES
COND_ROWS = 16
CTX_ROW = 8

F32 = jnp.float32
BF16 = jnp.bfloat16


def _params(n_axes):
    return pltpu.CompilerParams(dimension_semantics=("arbitrary",) * n_axes,
                                vmem_limit_bytes=VMEM_LIMIT_BYTES)


def _resident(block_shape, index_map):
    return pl.BlockSpec(block_shape, index_map, pipeline_mode=pl.Buffered(1))


def _dot(a, b):
    return jnp.dot(a, b, preferred_element_type=F32)


def _layer_norm(x):
    mu = jnp.mean(x, axis=-1, keepdims=True)
    xc = x - mu
    var = jnp.mean(xc * xc, axis=-1, keepdims=True)
    return xc * lax.rsqrt(var + LN_EPS)


def _split_dot(t, w):
    hi = t.astype(BF16)
    lo = (t - hi.astype(F32)).astype(BF16)
    return _dot(hi, w) + _dot(lo, w)


def _staggered(n, pre, mid, post):
    state = pre(0)
    done = None
    for s in range(n):
        cur = mid(s, state)
        if s + 1 < n:
            state = pre(s + 1)
        if done is not None:
            post(s - 1, done)
        done = cur
    post(n - 1, done)


def _ada_kernel(cond_ref, w_ref, b_ref, o_ref):
    cnd = cond_ref[...]
    act = (cnd * jax.nn.sigmoid(cnd)).astype(BF16)
    o_ref[...] = _dot(act, w_ref[...].astype(BF16)) + b_ref[...]


def _ada_all(cond, w_ada, b_ada):
    tn = 1536
    n_out = 6 * D_MODEL
    return pl.pallas_call(
        _ada_kernel,
        grid=(DEPTH, n_out // tn),
        in_specs=[
            pl.BlockSpec((COND_ROWS, D_MODEL), lambda l, j: (0, 0)),
            pl.BlockSpec((None, D_MODEL, tn), lambda l, j: (l, 0, j)),
            pl.BlockSpec((None, 1, tn), lambda l, j: (l, 0, j)),
        ],
        out_specs=pl.BlockSpec((None, COND_ROWS, tn), lambda l, j: (l, 0, j)),
        out_shape=jax.ShapeDtypeStruct((DEPTH, COND_ROWS, n_out), F32),
        compiler_params=_params(2),
        name="ada",
    )(cond, w_ada, b_ada.reshape(DEPTH, 1, n_out))


class _Seq:
    def __init__(self, n_batch, seq_len, latent, tile_m=TILE_M, sub_m=SUB_M):
        self.n_rows = n_batch * seq_len
        assert seq_len % sub_m == 0 and tile_m % sub_m == 0 and self.n_rows % tile_m == 0
        assert seq_len % tile_m == 0 or not latent
        self.n_batch = n_batch
        self.seq_len = seq_len
        self.latent = latent
        self.tile_m = tile_m
        self.sub_m = sub_m
        self.n_sub = tile_m // sub_m
        self.subs_per_seq = seq_len // sub_m
        self.n_tiles = self.n_rows // tile_m

    def with_tile(self, tile_m, sub_m):
        return _Seq(self.n_batch, self.seq_len, self.latent, tile_m, sub_m)

    def cond_row(self, i):
        return (i * self.tile_m) // self.seq_len if self.latent else CTX_ROW

    def tile_spec(self, width):
        return pl.BlockSpec((self.tile_m, width), lambda i: (i, 0))

    def halo_specs(self, width, halo):
        per_tile = self.tile_m // halo
        last = self.n_rows // halo - 1
        prev = pl.BlockSpec((halo, width), lambda i: (jnp.maximum(i * per_tile - 1, 0), 0))
        nxt = pl.BlockSpec((halo, width), lambda i: (jnp.minimum((i + 1) * per_tile, last), 0))
        return prev, nxt

    def mod_spec(self, layer, which):
        return pl.BlockSpec((None, None, None, 1, D_MODEL),
                            lambda i: (layer, self.cond_row(i), which, 0, 0))

    def sub_index(self, s):
        return (pl.program_id(0) * self.n_sub + s) % self.subs_per_seq

    def sub_edges(self, s):
        if self.subs_per_seq == 1:
            return False, False
        j = self.sub_index(s)
        return j > 0, j < self.subs_per_seq - 1

    def sub_rows(self, s):
        return slice(s * self.sub_m, (s + 1) * self.sub_m)

    def neighbours(self, s, ref, prev_ref, next_ref):
        halo = prev_ref.shape[0]
        lo = ref[s * self.sub_m - halo:s * self.sub_m, :] if s > 0 else prev_ref[...]
        hi = (ref[(s + 1) * self.sub_m:(s + 1) * self.sub_m + halo, :] if s < self.n_sub - 1
              else next_ref[...])
        return lo, hi


def _row_spec(layer, width, n_lead=1):
    if n_lead == 1:
        return pl.BlockSpec((None, 1, width), lambda i: (layer, 0, 0))
    return pl.BlockSpec((None, None, 1, width), lambda i: (layer[0], layer[1], 0, 0))


def _shift_rows(cur, prev_row, next_row):
    n = cur.shape[0]
    row = lax.broadcasted_iota(jnp.int32, (n, 1), 0)
    up = jnp.where(row == 0, prev_row, pltpu.roll(cur, 1, 0))
    down = jnp.where(row == n - 1, next_row, pltpu.roll(cur, n - 1, 0))
    return up, down


def _post_norm(x, gate, y, g, b):
    return _layer_norm(DEEPNORM_ALPHA * x + gate * y) * g + b


def _even_in_kernel(seq, x_ref, sc_ref, sh_ref, w_ref, gq_ref, gk_ref, bd_ref, *rest):
    if seq.latent:
        cos_ref, sin_ref, q_out, k_out, v_out, p_out = rest
    else:
        q_out, k_out, v_out, p_out, kraw_out, vraw_out = rest
    o1 = ATTN_WIDTH
    o2 = o1 + KV_WIDTH
    o3 = o2 + KV_WIDTH
    even_lane = (lax.broadcasted_iota(jnp.int32, (seq.sub_m, LANES), 1) & 1) == 0

    def rope(t, cos, sin):
        outs = []
        for j in range(t.shape[1] // LANES):
            slab = t[:, j * LANES:(j + 1) * LANES]
            partner = jnp.where(even_lane, pltpu.roll(slab, LANES - 1, 1), pltpu.roll(slab, 1, 1))
            outs.append(slab * cos + partner * sin)
        return outs[0] if len(outs) == 1 else jnp.concatenate(outs, axis=1)

    def pre(s):
        return (_layer_norm(x_ref[seq.sub_rows(s), :]) * (1.0 + sc_ref[...]) + sh_ref[...]).astype(BF16)

    def mid(s, u):
        return _dot(u, w_ref[...])

    def post(s, proj):
        rows = seq.sub_rows(s)
        q = proj[:, :o1]
        k = proj[:, o1:o2]
        v = proj[:, o2:o3]
        p_out[rows, :] = proj[:, o3:].astype(BF16)
        bd = bd_ref[...]
        q = q * lax.rsqrt(_split_dot(q * q, bd) * (1.0 / HEAD_DIM) + RMS_EPS) * gq_ref[...]
        k = k * lax.rsqrt(_split_dot(k * k, bd[:KV_WIDTH, :KV_WIDTH]) * (1.0 / HEAD_DIM) + RMS_EPS) * gk_ref[...]
        if seq.latent:
            cos = cos_ref[rows, :]
            sin = sin_ref[rows, :]
            q = rope(q, cos, sin)
            k = rope(k, cos, sin)
        else:
            kraw_out[rows, :] = k
            vraw_out[rows, :] = v
        q_out[rows, :] = (q * Q_SCALE).astype(BF16)
        k_out[rows, :] = k.astype(BF16)
        v_out[rows, :] = v.astype(BF16)

    _staggered(seq.n_sub, pre, mid, post)


def _even_in(seq, layer, e, x, ada5, w_in, gq, gk, bd, rope_tabs):
    n = seq.n_rows
    in_specs = [
        seq.tile_spec(D_MODEL),
        seq.mod_spec(layer, 1),
        seq.mod_spec(layer, 0),
        _resident((None, D_MODEL, IN_EVEN), lambda i: (e, 0, 0)),
        _row_spec(e, ATTN_WIDTH),
        _row_spec(e, KV_WIDTH),
        _resident((ATTN_WIDTH, ATTN_WIDTH), lambda i: (0, 0)),
    ]
    args = [x, ada5, ada5, w_in, gq, gk, bd]
    out_specs = [seq.tile_spec(ATTN_WIDTH), seq.tile_spec(KV_WIDTH), seq.tile_spec(KV_WIDTH),
                 seq.tile_spec(POOL_WIDTH)]
    out_shape = [jax.ShapeDtypeStruct((n, ATTN_WIDTH), BF16), jax.ShapeDtypeStruct((n, KV_WIDTH), BF16),
                 jax.ShapeDtypeStruct((n, KV_WIDTH), BF16), jax.ShapeDtypeStruct((n, POOL_WIDTH), BF16)]
    if seq.latent:
        tiles_per_seq = seq.seq_len // seq.tile_m
        in_specs += [pl.BlockSpec((seq.tile_m, LANES), lambda i: (i % tiles_per_seq, 0))] * 2
        args += list(rope_tabs)
    else:
        out_specs += [seq.tile_spec(KV_WIDTH)] * 2
        out_shape += [jax.ShapeDtypeStruct((n, KV_WIDTH), F32)] * 2
    return pl.pallas_call(
        functools.partial(_even_in_kernel, seq),
        grid=(seq.n_tiles,),
        in_specs=in_specs,
        out_specs=out_specs,
        out_shape=out_shape,
        compiler_params=_params(1),
        name="even_in",
    )(*args)


def _reduce_rows(x, op, final, chunk=256):
    n = x.shape[0]
    if n > chunk and n % chunk == 0:
        parts = [x[i:i + chunk] for i in range(0, n, chunk)]
        while len(parts) > 1:
            parts = [op(parts[i], parts[i + 1]) if i + 1 < len(parts) else parts[i]
                     for i in range(0, len(parts), 2)]
        x = parts[0]
        n = chunk
    while n > SUBLANES and n % (2 * SUBLANES) == 0:
        n //= 2
        x = op(x[:n], x[n:])
    return final(x, axis=0, keepdims=True)


def _attn_kernel(q_ref, k_ref, vt_ref, qmax_ref, o_ref, st_a, st_b, p_a, p_b, kmax_ref):
    group = N_HEADS // N_KV_HEADS
    heads_per_dot = ATTN_HEADS_PER_DOT
    tq = ATTN_SUB_M
    stages = [(r, h0) for r in range(q_ref.shape[0] // tq) for h0 in range(0, N_HEADS, heads_per_dot)]
    st_bufs = (st_a, st_b)
    p_bufs = (p_a, p_b)
    n_keys = st_a.shape[0]
    chunk = ATTN_KEY_CHUNK if n_keys % ATTN_KEY_CHUNK == 0 else n_keys

    @pl.when(pl.program_id(1) == 0)
    def _():
        kmax = None
        for g in range(N_KV_HEADS):
            kk = k_ref[g].astype(F32)
            part = jnp.max(jnp.sum(kk * kk, axis=-1, keepdims=True))
            kmax = part if kmax is None else jnp.maximum(kmax, part)
        kmax_ref[0] = kmax

    bounded = qmax_ref[0] * kmax_ref[0] <= ATTN_SCORE_BOUND ** 2

    def queries(i):
        r, h0 = stages[i]
        return jnp.concatenate([q_ref[r * tq:(r + 1) * tq, h * HEAD_DIM:(h + 1) * HEAD_DIM]
                                for h in range(h0, h0 + heads_per_dot)], axis=0)

    def scores_bounded(i):
        qs = queries(i)
        for c in range(0, n_keys, chunk):
            st = lax.dot_general(k_ref[stages[i][1] // group, c:c + chunk, :], qs, (((1,), (1,)), ((), ())),
                                 preferred_element_type=F32)
            p_bufs[i % 2][c:c + chunk, :] = jnp.exp2(st).astype(BF16)

    def scores_general(i):
        st_bufs[i % 2][...] = lax.dot_general(k_ref[stages[i][1] // group], queries(i),
                                              (((1,), (1,)), ((), ())), preferred_element_type=F32)

    def softmax_general(i):
        st_ref, p_ref = st_bufs[i % 2], p_bufs[i % 2]
        m = None
        for c in range(0, n_keys, chunk):
            part = _reduce_rows(st_ref[c:c + chunk, :], jnp.maximum, jnp.max)
            m = part if m is None else jnp.maximum(m, part)
        for c in range(0, n_keys, chunk):
            p_ref[c:c + chunk, :] = jnp.exp2(st_ref[c:c + chunk, :] - m).astype(BF16)

    def run(scores, softmax):
        outs = []
        scores(0)
        for i, (r, h0) in enumerate(stages):
            if i + 1 < len(stages):
                scores(i + 1)
            softmax(i)
            ot = _dot(vt_ref[h0 // group], p_bufs[i % 2][...])
            ot = ot[:HEAD_DIM] / ot[HEAD_DIM:HEAD_DIM + 1]
            outs += [ot[:, j * tq:(j + 1) * tq] for j in range(heads_per_dot)]
            if h0 + heads_per_dot == N_HEADS:
                o_ref[r * tq:(r + 1) * tq, :] = jnp.concatenate(outs, axis=0).T.astype(BF16)
                outs = []

    @pl.when(bounded)
    def _():
        run(scores_bounded, lambda i: None)

    @pl.when(jnp.logical_not(bounded))
    def _():
        run(scores_general, softmax_general)


def _attention(seq, q, k, v, q_gain):
    qmax = (jnp.max(q_gain * q_gain) * (HEAD_DIM * Q_SCALE ** 2 * 1.02)).reshape(1)
    n_keys = k.shape[2]
    vt = v.transpose(0, 2, 1).reshape(seq.n_batch, N_KV_HEADS, HEAD_DIM, n_keys)
    ones = jnp.ones((seq.n_batch, N_KV_HEADS, ATTN_ONES_ROWS, n_keys), BF16)
    vt = jnp.concatenate([vt, ones], axis=2)
    tq = ATTN_TILE_M if seq.seq_len % ATTN_TILE_M == 0 else ATTN_SUB_M
    tps = seq.seq_len // tq
    cols = ATTN_HEADS_PER_DOT * ATTN_SUB_M
    return pl.pallas_call(
        _attn_kernel,
        grid=(seq.n_batch, tps),
        in_specs=[
            pl.BlockSpec((tq, ATTN_WIDTH), lambda b, j: (b * tps + j, 0)),
            pl.BlockSpec((None, N_KV_HEADS, n_keys, HEAD_DIM), lambda b, j: (b, 0, 0, 0)),
            pl.BlockSpec((None, N_KV_HEADS, HEAD_DIM + ATTN_ONES_ROWS, n_keys), lambda b, j: (b, 0, 0, 0)),
            pl.BlockSpec(memory_space=pltpu.SMEM),
        ],
        out_specs=pl.BlockSpec((tq, ATTN_WIDTH), lambda b, j: (b * tps + j, 0)),
        out_shape=jax.ShapeDtypeStruct((seq.n_rows, ATTN_WIDTH), BF16),
        scratch_shapes=[pltpu.VMEM((n_keys, cols), F32), pltpu.VMEM((n_keys, cols), F32),
                        pltpu.VMEM((n_keys, cols), BF16), pltpu.VMEM((n_keys, cols), BF16),
                        pltpu.SMEM((1,), F32)],
        compiler_params=_params(2),
        name="attention",
    )(q, k, vt, qmax)


def _even_out_kernel(seq, attn_ref, p_ref, pp_ref, pn_ref, x_ref, gate_ref, wp_ref, ps_ref, wo_ref,
                     g_ref, b_ref, o_ref):
    sub = seq.sub_m
    halo = pp_ref.shape[0]
    n_ext = sub + 2 * halo
    row = lax.broadcasted_iota(jnp.int32, (sub, 1), 0)

    def pre(s):
        has_prev, has_next = seq.sub_edges(s)
        lo, hi = seq.neighbours(s, p_ref, pp_ref, pn_ref)
        p = p_ref[seq.sub_rows(s), :].astype(F32)
        ext = jnp.concatenate([jnp.where(has_prev, lo.astype(F32), 0.0), p,
                               jnp.where(has_next, hi.astype(F32), 0.0)], axis=0)
        pos = seq.sub_index(s) * sub + row
        mixed = []
        for gi, w in enumerate(POOL_WINDOWS):
            half = w // 2
            lanes = slice(gi * POOL_GROUP, (gi + 1) * POOL_GROUP)
            run = ext[:, lanes]
            span = 1
            while span < w:
                run = run + pltpu.roll(run, span, 0)
                span *= 2
            if half > 1:
                run = pltpu.roll(run, n_ext - (half - 1), 0)
            total = run[halo:halo + sub]
            cnt = jnp.minimum(pos + half, seq.seq_len) - jnp.maximum(pos - half, 0)
            centred = total / cnt.astype(F32) - p[:, lanes]
            mixed.append(_dot(centred.astype(BF16), wp_ref[gi]))
        pool = jnp.concatenate(mixed, axis=1) * ps_ref[...]
        return jnp.concatenate([attn_ref[seq.sub_rows(s), :], pool.astype(BF16)], axis=1)

    def mid(s, m):
        return _dot(m, wo_ref[...])

    def post(s, y):
        rows = seq.sub_rows(s)
        o_ref[rows, :] = _post_norm(x_ref[rows, :], gate_ref[...], y, g_ref[...], b_ref[...])

    _staggered(seq.n_sub, pre, mid, post)


def _even_out(seq, layer, e, attn, p, x, ada5, w_pool, pool_scale, w_out, ln_g, ln_b):
    prev_spec, next_spec = seq.halo_specs(POOL_WIDTH, HALO_BF16)
    return pl.pallas_call(
        functools.partial(_even_out_kernel, seq),
        grid=(seq.n_tiles,),
        in_specs=[
            seq.tile_spec(ATTN_WIDTH),
            seq.tile_spec(POOL_WIDTH), prev_spec, next_spec,
            seq.tile_spec(D_MODEL),
            seq.mod_spec(layer, 2),
            _resident((None, len(POOL_WINDOWS), POOL_GROUP, POOL_GROUP), lambda i: (e, 0, 0, 0)),
            _row_spec(e, POOL_WIDTH),
            _resident((None, D_MODEL, D_MODEL), lambda i: (layer, 0, 0)),
            _row_spec((layer, 0), D_MODEL, 2),
            _row_spec((layer, 0), D_MODEL, 2),
        ],
        out_specs=seq.tile_spec(D_MODEL),
        out_shape=jax.ShapeDtypeStruct((seq.n_rows, D_MODEL), F32),
        compiler_params=_params(1),
        name="even_out",
    )(attn, p, p, p, x, ada5, w_pool, pool_scale, w_out, ln_g, ln_b)


def _odd_in_kernel(seq, x_ref, sc_ref, sh_ref, w_ref, ch_out, bg_out, f_out):
    c = CONV_WIDTH

    def pre(s):
        return (_layer_norm(x_ref[seq.sub_rows(s), :]) * (1.0 + sc_ref[...]) + sh_ref[...]).astype(BF16)

    def mid(s, u):
        return _dot(u, w_ref[...])

    def post(s, proj):
        rows = seq.sub_rows(s)
        ch_out[rows, :] = (proj[:, 2 * c:3 * c] * proj[:, :c]).astype(BF16)
        bg_out[rows, :] = proj[:, c:2 * c].astype(BF16)
        f_out[rows, :] = proj[:, 3 * c:].astype(BF16)

    _staggered(seq.n_sub, pre, mid, post)


def _odd_in(seq, layer, o, x, ada5, w_in):
    n = seq.n_rows
    return pl.pallas_call(
        functools.partial(_odd_in_kernel, seq),
        grid=(seq.n_tiles,),
        in_specs=[
            seq.tile_spec(D_MODEL),
            seq.mod_spec(layer, 1),
            seq.mod_spec(layer, 0),
            _resident((None, D_MODEL, IN_ODD), lambda i: (o, 0, 0)),
        ],
        out_specs=[seq.tile_spec(CONV_WIDTH), seq.tile_spec(CONV_WIDTH), seq.tile_spec(FOURIER_WIDTH)],
        out_shape=[jax.ShapeDtypeStruct((n, CONV_WIDTH), BF16), jax.ShapeDtypeStruct((n, CONV_WIDTH), BF16),
                   jax.ShapeDtypeStruct((n, FOURIER_WIDTH), BF16)],
        compiler_params=_params(1),
        name="odd_in",
    )(x, ada5, ada5, w_in)


def _fourier_kernel(scale, f_ref, cl_ref, sl_ref, cc_ref, sc_ref, o_ref):
    fb = f_ref[...]
    g_cos = (_dot(fb, cc_ref[...]) * scale).astype(BF16)
    g_sin = (_dot(fb, sc_ref[...]) * scale).astype(BF16)
    o_ref[...] = (_dot(cl_ref[...], g_cos) - _dot(sl_ref[...], g_sin)).astype(BF16)


def _dft_angles(rows, n):
    k = lax.iota(jnp.int32, n)[None, :]
    return ((rows[:, None] * k) % n).astype(F32) * (2.0 * math.pi / n)


def _dft_expand_kernel(hc_ref, hs_ref, lc_ref, ls_ref, c_out, s_out):
    hc, hs = hc_ref[...], hs_ref[...]
    lc, ls = lc_ref[...], ls_ref[...]
    c_out[...] = (hc * lc - hs * ls).astype(BF16)
    s_out[...] = (hs * lc + hc * ls).astype(BF16)


def _dft_tables(n):
    split = FOURIER_GROUP
    assert n % split == 0
    ang_hi = _dft_angles(lax.iota(jnp.int32, n // split) * split, n).reshape(n // split, 1, n)
    ang_lo = _dft_angles(lax.iota(jnp.int32, split), n)
    hi_spec = pl.BlockSpec((None, 1, n), lambda i: (i, 0, 0))
    lo_spec = pl.BlockSpec((split, n), lambda i: (0, 0))
    out_spec = pl.BlockSpec((split, n), lambda i: (i, 0))
    return pl.pallas_call(
        _dft_expand_kernel,
        grid=(n // split,),
        in_specs=[hi_spec, hi_spec, lo_spec, lo_spec],
        out_specs=[out_spec, out_spec],
        out_shape=[jax.ShapeDtypeStruct((n, n), BF16)] * 2,
        compiler_params=_params(1),
        name="dft_tables",
    )(jnp.cos(ang_hi), jnp.sin(ang_hi), jnp.cos(ang_lo), jnp.sin(ang_lo))


def _fourier(seq, f, tabs):
    L = seq.seq_len
    cl, sl, cc, sc = tabs
    scale = 1.0 / math.sqrt(L * FOURIER_GROUP)
    return pl.pallas_call(
        functools.partial(_fourier_kernel, scale),
        grid=(seq.n_batch,),
        in_specs=[
            pl.BlockSpec((L, FOURIER_WIDTH), lambda b: (b, 0)),
            _resident((L, L), lambda b: (0, 0)),
            _resident((L, L), lambda b: (0, 0)),
            _resident((FOURIER_WIDTH, FOURIER_WIDTH), lambda b: (0, 0)),
            _resident((FOURIER_WIDTH, FOURIER_WIDTH), lambda b: (0, 0)),
        ],
        out_specs=pl.BlockSpec((L, FOURIER_WIDTH), lambda b: (b, 0)),
        out_shape=jax.ShapeDtypeStruct((seq.n_rows, FOURIER_WIDTH), BF16),
        compiler_params=_params(1),
        name="fourier",
    )(f, cl, sl, cc, sc)


def _odd_out_kernel(seq, ch_ref, chp_ref, chn_ref, bg_ref, fo_ref, x_ref, gate_ref, cw_ref, cb_ref, wo_ref,
                    g_ref, b_ref, o_ref):
    halo = chp_ref.shape[0]

    def pre(s):
        has_prev, has_next = seq.sub_edges(s)
        rows = seq.sub_rows(s)
        lo, hi = seq.neighbours(s, ch_ref, chp_ref, chn_ref)
        ch = ch_ref[rows, :].astype(F32)
        prev_row = jnp.where(has_prev, lo.astype(F32)[halo - 1:halo, :], 0.0)
        next_row = jnp.where(has_next, hi.astype(F32)[0:1, :], 0.0)
        up, down = _shift_rows(ch, prev_row, next_row)
        conv = cb_ref[...] + up * cw_ref[0:1, :] + ch * cw_ref[1:2, :] + down * cw_ref[2:3, :]
        conv_out = bg_ref[rows, :].astype(F32) * conv
        return jnp.concatenate([conv_out.astype(BF16), fo_ref[rows, :]], axis=1)

    def mid(s, m):
        return _dot(m, wo_ref[...])

    def post(s, y):
        rows = seq.sub_rows(s)
        o_ref[rows, :] = _post_norm(x_ref[rows, :], gate_ref[...], y, g_ref[...], b_ref[...])

    _staggered(seq.n_sub, pre, mid, post)


def _odd_out(seq, layer, o, ch, bg, fo, x, ada5, conv_w, conv_b, w_out, ln_g, ln_b):
    prev_spec, next_spec = seq.halo_specs(CONV_WIDTH, HALO_BF16)
    return pl.pallas_call(
        functools.partial(_odd_out_kernel, seq),
        grid=(seq.n_tiles,),
        in_specs=[
            seq.tile_spec(CONV_WIDTH), prev_spec, next_spec,
            seq.tile_spec(CONV_WIDTH),
            seq.tile_spec(FOURIER_WIDTH),
            seq.tile_spec(D_MODEL),
            seq.mod_spec(layer, 2),
            pl.BlockSpec((None, 3, CONV_WIDTH), lambda i: (o, 0, 0)),
            _row_spec(o, CONV_WIDTH),
            _resident((None, D_MODEL, D_MODEL), lambda i: (layer, 0, 0)),
            _row_spec((layer, 0), D_MODEL, 2),
            _row_spec((layer, 0), D_MODEL, 2),
        ],
        out_specs=seq.tile_spec(D_MODEL),
        out_shape=jax.ShapeDtypeStruct((seq.n_rows, D_MODEL), F32),
        compiler_params=_params(1),
        name="odd_out",
    )(ch, ch, ch, bg, fo, x, ada5, conv_w, conv_b, w_out, ln_g, ln_b)


def _ffn_kernel(seq, x_ref, xp_ref, xn_ref, sc_ref, sh_ref, gate_ref, wa_ref, wg_ref, cw_ref, cb_ref, wd_ref,
                g_ref, b_ref, o_ref):
    halo = xp_ref.shape[0]
    sub = seq.sub_m
    n_sub = seq.n_sub
    n_ext = sub + 2 * halo
    row = lax.broadcasted_iota(jnp.int32, (n_ext, 1), 0)

    def stage_up(s):
        has_prev, has_next = seq.sub_edges(s)
        lo, hi = seq.neighbours(s, x_ref, xp_ref, xn_ref)
        x = x_ref[seq.sub_rows(s), :]
        u_ext = (_layer_norm(jnp.concatenate([lo, x, hi], axis=0)) * (1.0 + sc_ref[...]) + sh_ref[...]).astype(BF16)
        a_ext = _dot(u_ext, wa_ref[...])
        gate_lin = _dot(u_ext[halo:halo + sub], wg_ref[...])
        inside = ((row >= halo) | has_prev) & ((row < halo + sub) | has_next)
        return x, jnp.where(inside, a_ext, 0.0), gate_lin

    def stage_hidden(a_ext, gate_lin):
        up = pltpu.roll(a_ext, 1, 0)[halo:halo + sub]
        mid = a_ext[halo:halo + sub]
        down = pltpu.roll(a_ext, n_ext - 1, 0)[halo:halo + sub]
        conv = cb_ref[...] + up * cw_ref[0:1, :] + mid * cw_ref[1:2, :] + down * cw_ref[2:3, :]
        return (conv * jax.nn.sigmoid(conv) * gate_lin).astype(BF16)

    def stage_out(s, x, y):
        o_ref[seq.sub_rows(s), :] = _post_norm(x, gate_ref[...], y, g_ref[...], b_ref[...])

    ups = {0: stage_up(0)}
    ys = {}
    for s in range(n_sub):
        if s + 1 < n_sub:
            ups[s + 1] = stage_up(s + 1)
        x, a_ext, gate_lin = ups.pop(s)
        h = stage_hidden(a_ext, gate_lin)
        if s > 0:
            stage_out(s - 1, *ys.pop(s - 1))
        ys[s] = (x, _dot(h, wd_ref[...]))
    stage_out(n_sub - 1, *ys.pop(n_sub - 1))


def _ffn(seq, layer, x, ada5, w_up, ffn_conv_w, ffn_conv_b, w_down, ln_g, ln_b):
    prev_spec, next_spec = seq.halo_specs(D_MODEL, HALO_F32)
    return pl.pallas_call(
        functools.partial(_ffn_kernel, seq),
        grid=(seq.n_tiles,),
        in_specs=[
            seq.tile_spec(D_MODEL), prev_spec, next_spec,
            seq.mod_spec(layer, 4),
            seq.mod_spec(layer, 3),
            seq.mod_spec(layer, 5),
            _resident((None, D_MODEL, D_FF), lambda i: (layer, 0, 0)),
            _resident((None, D_MODEL, D_FF), lambda i: (layer, 0, 1)),
            pl.BlockSpec((None, 3, D_FF), lambda i: (layer, 0, 0)),
            _row_spec(layer, D_FF),
            _resident((None, D_FF, D_MODEL), lambda i: (layer, 0, 0)),
            _row_spec((layer, 1), D_MODEL, 2),
            _row_spec((layer, 1), D_MODEL, 2),
        ],
        out_specs=seq.tile_spec(D_MODEL),
        out_shape=jax.ShapeDtypeStruct((seq.n_rows, D_MODEL), F32),
        compiler_params=_params(1),
        name="conv_ffn",
    )(x, x, x, ada5, ada5, ada5, w_up, w_up, ffn_conv_w, ffn_conv_b, w_down, ln_g, ln_b)


def _rope_tables(seq_len):
    t = lax.iota(jnp.int32, seq_len)
    row = (t // GRID_W).astype(F32)
    col = (t % GRID_W).astype(F32)
    n_freq = HEAD_DIM // 4
    inv = 1.0 / (ROPE_THETA ** (jnp.arange(n_freq, dtype=F32) / n_freq))
    ang = jnp.concatenate([row[:, None] * inv, col[:, None] * inv], -1)
    cos = jnp.repeat(jnp.cos(ang), 2, axis=1)
    sin = jnp.repeat(jnp.sin(ang), 2, axis=1) * jnp.tile(jnp.array([-1.0, 1.0], F32), HEAD_DIM // 2)
    reps = LANES // HEAD_DIM
    return jnp.tile(cos, (1, reps)), jnp.tile(sin, (1, reps))


def _fourier_tables(seq_len):
    cl, sl = _dft_tables(seq_len)
    ang = _dft_angles(lax.iota(jnp.int32, FOURIER_GROUP), FOURIER_GROUP)
    eye = jnp.eye(N_FOURIER_GROUPS, dtype=F32)
    return cl, sl, jnp.kron(eye, jnp.cos(ang)).astype(BF16), jnp.kron(eye, jnp.sin(ang)).astype(BF16)


def _run_trunk(seq, x, ada5, ctx_k, ctx_v, wts):
    (w_in_even, gq, gk, bd, w_pool, pool_scale, w_in_odd, conv_w, conv_b, w_out, w_up, ffn_conv_w,
     ffn_conv_b, w_down, ln_g, ln_b) = wts
    rope_tabs = _rope_tables(seq.seq_len) if seq.latent else None
    four_tabs = _fourier_tables(seq.seq_len)
    new_k, new_v = [], []
    for layer in range(DEPTH):
        if layer % 2 == 0:
            e = layer // 2
            outs = _even_in(seq, layer, e, x, ada5, w_in_even, gq, gk, bd, rope_tabs)
            q, k, v, p = outs[:4]
            k = k.reshape(seq.n_batch, seq.seq_len, KV_WIDTH)
            v = v.reshape(seq.n_batch, seq.seq_len, KV_WIDTH)
            if seq.latent:
                k = jnp.concatenate([k, ctx_k[:, e]], axis=1)
                v = jnp.concatenate([v, ctx_v[:, e]], axis=1)
            else:
                new_k.append(outs[4])
                new_v.append(outs[5])
            n_keys = k.shape[1]
            k = k.reshape(seq.n_batch, n_keys, N_KV_HEADS, HEAD_DIM).transpose(0, 2, 1, 3)
            attn = _attention(seq, q, k, v, gq[e])
            x = _even_out(seq, layer, e, attn, p, x, ada5, w_pool, pool_scale, w_out, ln_g, ln_b)
        else:
            o = layer // 2
            ch, bg, f = _odd_in(seq, layer, o, x, ada5, w_in_odd)
            fo = _fourier(seq, f, four_tabs)
            x = _odd_out(seq, layer, o, ch, bg, fo, x, ada5, conv_w, conv_b, w_out, ln_g, ln_b)
        x = _ffn(seq.with_tile(FFN_TILE_M, SUB_M), layer, x, ada5, w_up, ffn_conv_w, ffn_conv_b, w_down,
                 ln_g, ln_b)
    return x, new_k, new_v


def kernel(x_prompt, x_sample, cache_k, cache_v, c, c_ctx, w_ada, b_ada, w_in_even, q_norm_g, k_norm_g,
           w_pool, pool_scale, w_in_odd, conv_w, conv_b, w_out, w_up, ffn_conv_w, ffn_conv_b, w_down,
           ln_g, ln_b):
    n_prompt, prompt_len, _ = x_prompt.shape
    n_sample, sample_len, _ = x_sample.shape
    n_even = w_in_even.shape[0]
    n_odd = w_in_odd.shape[0]
    assert n_sample <= CTX_ROW

    cond = jnp.zeros((COND_ROWS, D_MODEL), F32).at[:n_sample].set(c).at[CTX_ROW].set(c_ctx)
    ada = _ada_all(cond, w_ada, b_ada)
    ada5 = ada.reshape(DEPTH, COND_ROWS, 6, 1, D_MODEL)

    head_of = lax.iota(jnp.int32, ATTN_WIDTH) // HEAD_DIM
    bd = (head_of[:, None] == head_of[None, :]).astype(BF16)
    wts = (
        w_in_even.astype(BF16),
        jnp.tile(q_norm_g, (1, N_HEADS)).reshape(n_even, 1, ATTN_WIDTH),
        jnp.tile(k_norm_g, (1, N_KV_HEADS)).reshape(n_even, 1, KV_WIDTH),
        bd,
        w_pool.astype(BF16),
        pool_scale.reshape(n_even, 1, POOL_WIDTH),
        w_in_odd.astype(BF16),
        conv_w,
        conv_b.reshape(n_odd, 1, CONV_WIDTH),
        w_out.astype(BF16),
        w_up.astype(BF16),
        ffn_conv_w,
        ffn_conv_b.reshape(DEPTH, 1, D_FF),
        w_down.astype(BF16),
        ln_g.reshape(DEPTH, 2, 1, D_MODEL),
        ln_b.reshape(DEPTH, 2, 1, D_MODEL),
    )

    prompt = _Seq(n_prompt, prompt_len, latent=False)
    y_prompt, ks, vs = _run_trunk(prompt, x_prompt.reshape(-1, D_MODEL), ada5, None, None, wts)
    cache_shape = (n_prompt, prompt_len, N_KV_HEADS, HEAD_DIM)
    new_cache_k = jnp.stack([k.reshape(cache_shape) for k in ks], 1)
    new_cache_v = jnp.stack([v.reshape(cache_shape) for v in vs], 1)

    sample = _Seq(n_sample, sample_len, latent=True)
    past_len = cache_k.shape[2]
    ctx_k = cache_k.reshape(n_sample, n_even, past_len, KV_WIDTH).astype(BF16)
    ctx_v = cache_v.reshape(n_sample, n_even, past_len, KV_WIDTH).astype(BF16)
    y_sample, _, _ = _run_trunk(sample, x_sample.reshape(-1, D_MODEL), ada5, ctx_k, ctx_v, wts)

    return (y_prompt.reshape(x_prompt.shape), y_sample.reshape(x_sample.shape), new_cache_k, new_cache_v)
```

```python
import functools
import math

import jax
import jax.numpy as jnp
from jax import lax
from jax.experimental import pallas as pl
from jax.experimental.pallas import tpu as pltpu

D_MODEL = 1024
DEPTH = 4
GRID_W = 64
N_HEADS = 8
N_KV_HEADS = 2
HEAD_DIM = 64
ATTN_WIDTH = N_HEADS * HEAD_DIM
KV_WIDTH = N_KV_HEADS * HEAD_DIM
POOL_WIDTH = D_MODEL - ATTN_WIDTH
POOL_WINDOWS = (2, 4, 8, 16)
POOL_GROUP = POOL_WIDTH // len(POOL_WINDOWS)
FOURIER_WIDTH = D_MODEL // 4
N_FOURIER_GROUPS = 4
FOURIER_GROUP = FOURIER_WIDTH // N_FOURIER_GROUPS
CONV_WIDTH = D_MODEL - FOURIER_WIDTH
D_FF = 2816
ROPE_THETA = 10000.0
LN_EPS = 1e-6
RMS_EPS = 1e-6
IN_EVEN = ATTN_WIDTH + 2 * KV_WIDTH + POOL_WIDTH
IN_ODD = 3 * CONV_WIDTH + FOURIER_WIDTH
DEEPNORM_ALPHA = (2 * DEPTH) ** 0.25
Q_SCALE = HEAD_DIM ** -0.5 * math.log2(math.e)

SUBLANES = 8
LANES = 128
VMEM_LIMIT_BYTES = 56 * 1024 * 1024

TILE_M = 1024
FFN_TILE_M = 512
SUB_M = 256
ATTN_TILE_M = 512
ATTN_SUB_M = 256
ATTN_ONES_ROWS = 16
ATTN_HEADS_PER_DOT = 2
ATTN_KEY_CHUNK = 256
ATTN_SCORE_BOUND = 80.0
HALO_F32 = SUBLANES
HALO_BF16 = 2 * SUBLANES
COND_ROWS = 16
CTX_ROW = 8

F32 = jnp.float32
BF16 = jnp.bfloat16


def _params(n_axes):
    return pltpu.CompilerParams(dimension_semantics=("arbitrary",) * n_axes,
                                vmem_limit_bytes=VMEM_LIMIT_BYTES)


def _resident(block_shape, index_map):
    return pl.BlockSpec(block_shape, index_map, pipeline_mode=pl.Buffered(1))


def _dot(a, b):
    return jnp.dot(a, b, preferred_element_type=F32)


def _layer_norm(x):
    mu = jnp.mean(x, axis=-1, keepdims=True)
    xc = x - mu
    var = jnp.mean(xc * xc, axis=-1, keepdims=True)
    return xc * lax.rsqrt(var + LN_EPS)


def _split_dot(t, w):
    hi = t.astype(BF16)
    lo = (t - hi.astype(F32)).astype(BF16)
    return _dot(hi, w) + _dot(lo, w)


def _staggered(n, pre, mid, post):
    state = pre(0)
    done = None
    for s in range(n):
        cur = mid(s, state)
        if s + 1 < n:
            state = pre(s + 1)
        if done is not None:
            post(s - 1, done)
        done = cur
    post(n - 1, done)


def _ada_kernel(cond_ref, w_ref, b_ref, o_ref):
    cnd = cond_ref[...]
    act = (cnd * jax.nn.sigmoid(cnd)).astype(BF16)
    o_ref[...] = _dot(act, w_ref[...].astype(BF16)) + b_ref[...]


def _ada_all(cond, w_ada, b_ada):
    tn = 1536
    n_out = 6 * D_MODEL
    return pl.pallas_call(
        _ada_kernel,
        grid=(DEPTH, n_out // tn),
        in_specs=[
            pl.BlockSpec((COND_ROWS, D_MODEL), lambda l, j: (0, 0)),
            pl.BlockSpec((None, D_MODEL, tn), lambda l, j: (l, 0, j)),
            pl.BlockSpec((None, 1, tn), lambda l, j: (l, 0, j)),
        ],
        out_specs=pl.BlockSpec((None, COND_ROWS, tn), lambda l, j: (l, 0, j)),
        out_shape=jax.ShapeDtypeStruct((DEPTH, COND_ROWS, n_out), F32),
        compiler_params=_params(2),
        name="ada",
    )(cond, w_ada, b_ada.reshape(DEPTH, 1, n_out))


class _Seq:
    def __init__(self, n_batch, seq_len, latent, tile_m=TILE_M, sub_m=SUB_M):
        self.n_rows = n_batch * seq_len
        assert seq_len % sub_m == 0 and tile_m % sub_m == 0 and self.n_rows % tile_m == 0
        assert seq_len % tile_m == 0 or not latent
        self.n_batch = n_batch
        self.seq_len = seq_len
        self.latent = latent
        self.tile_m = tile_m
        self.sub_m = sub_m
        self.n_sub = tile_m // sub_m
        self.subs_per_seq = seq_len // sub_m
        self.n_tiles = self.n_rows // tile_m

    def with_tile(self, tile_m, sub_m):
        return _Seq(self.n_batch, self.seq_len, self.latent, tile_m, sub_m)

    def cond_row(self, i):
        return (i * self.tile_m) // self.seq_len if self.latent else CTX_ROW

    def tile_spec(self, width):
        return pl.BlockSpec((self.tile_m, width), lambda i: (i, 0))

    def halo_specs(self, width, halo):
        per_tile = self.tile_m // halo
        last = self.n_rows // halo - 1
        prev = pl.BlockSpec((halo, width), lambda i: (jnp.maximum(i * per_tile - 1, 0), 0))
        nxt = pl.BlockSpec((halo, width), lambda i: (jnp.minimum((i + 1) * per_tile, last), 0))
        return prev, nxt

    def mod_spec(self, layer, which):
        return pl.BlockSpec((None, None, None, 1, D_MODEL),
                            lambda i: (layer, self.cond_row(i), which, 0, 0))

    def sub_index(self, s):
        return (pl.program_id(0) * self.n_sub + s) % self.subs_per_seq

    def sub_edges(self, s):
        if self.subs_per_seq == 1:
            return False, False
        j = self.sub_index(s)
        return j > 0, j < self.subs_per_seq - 1

    def sub_rows(self, s):
        return slice(s * self.sub_m, (s + 1) * self.sub_m)

    def neighbours(self, s, ref, prev_ref, next_ref):
        halo = prev_ref.shape[0]
        lo = ref[s * self.sub_m - halo:s * self.sub_m, :] if s > 0 else prev_ref[...]
        hi = (ref[(s + 1) * self.sub_m:(s + 1) * self.sub_m + halo, :] if s < self.n_sub - 1
              else next_ref[...])
        return lo, hi


def _row_spec(layer, width, n_lead=1):
    if n_lead == 1:
        return pl.BlockSpec((None, 1, width), lambda i: (layer, 0, 0))
    return pl.BlockSpec((None, None, 1, width), lambda i: (layer[0], layer[1], 0, 0))


def _shift_rows(cur, prev_row, next_row):
    n = cur.shape[0]
    row = lax.broadcasted_iota(jnp.int32, (n, 1), 0)
    up = jnp.where(row == 0, prev_row, pltpu.roll(cur, 1, 0))
    down = jnp.where(row == n - 1, next_row, pltpu.roll(cur, n - 1, 0))
    return up, down


def _post_norm(x, gate, y, g, b):
    return _layer_norm(DEEPNORM_ALPHA * x + gate * y) * g + b


def _even_in_kernel(seq, x_ref, sc_ref, sh_ref, w_ref, gq_ref, gk_ref, bd_ref, *rest):
    if seq.latent:
        cos_ref, sin_ref, q_out, k_out, vt_out, p_out = rest
    else:
        q_out, k_out, vt_out, p_out, kraw_out, vraw_out = rest
    o1 = ATTN_WIDTH
    o2 = o1 + KV_WIDTH
    o3 = o2 + KV_WIDTH
    even_lane = (lax.broadcasted_iota(jnp.int32, (seq.sub_m, LANES), 1) & 1) == 0

    def rope(t, cos, sin):
        outs = []
        for j in range(t.shape[1] // LANES):
            slab = t[:, j * LANES:(j + 1) * LANES]
            partner = jnp.where(even_lane, pltpu.roll(slab, LANES - 1, 1), pltpu.roll(slab, 1, 1))
            outs.append(slab * cos + partner * sin)
        return outs[0] if len(outs) == 1 else jnp.concatenate(outs, axis=1)

    def pre(s):
        return (_layer_norm(x_ref[seq.sub_rows(s), :]) * (1.0 + sc_ref[...]) + sh_ref[...]).astype(BF16)

    def mid(s, u):
        return _dot(u, w_ref[...])

    def post(s, proj):
        rows = seq.sub_rows(s)
        q = proj[:, :o1]
        k = proj[:, o1:o2]
        v = proj[:, o2:o3]
        p_out[rows, :] = proj[:, o3:].astype(BF16)
        bd = bd_ref[...]
        q = q * lax.rsqrt(_split_dot(q * q, bd) * (1.0 / HEAD_DIM) + RMS_EPS) * gq_ref[...]
        k = k * lax.rsqrt(_split_dot(k * k, bd[:KV_WIDTH, :KV_WIDTH]) * (1.0 / HEAD_DIM) + RMS_EPS) * gk_ref[...]
        if seq.latent:
            cos = cos_ref[rows, :]
            sin = sin_ref[rows, :]
            q = rope(q, cos, sin)
            k = rope(k, cos, sin)
        else:
            kraw_out[rows, :] = k
            vraw_out[rows, :] = v
        q_out[rows, :] = (q * Q_SCALE).astype(BF16)
        vt = v.T.astype(BF16)
        for g in range(N_KV_HEADS):
            heads = slice(g * HEAD_DIM, (g + 1) * HEAD_DIM)
            k_out[g, rows, :] = k[:, heads].astype(BF16)
            vt_out[g, :HEAD_DIM, rows] = vt[heads, :]
            vt_out[g, HEAD_DIM:, rows] = jnp.ones((ATTN_ONES_ROWS, seq.sub_m), BF16)

    _staggered(seq.n_sub, pre, mid, post)


def _even_in(seq, layer, e, x, ada5, w_in, gq, gk, bd, rope_tabs):
    n = seq.n_rows
    in_specs = [
        seq.tile_spec(D_MODEL),
        seq.mod_spec(layer, 1),
        seq.mod_spec(layer, 0),
        _resident((None, D_MODEL, IN_EVEN), lambda i: (e, 0, 0)),
        _row_spec(e, ATTN_WIDTH),
        _row_spec(e, KV_WIDTH),
        _resident((ATTN_WIDTH, ATTN_WIDTH), lambda i: (0, 0)),
    ]
    args = [x, ada5, ada5, w_in, gq, gk, bd]
    vt_rows = HEAD_DIM + ATTN_ONES_ROWS
    out_specs = [seq.tile_spec(ATTN_WIDTH),
                 pl.BlockSpec((N_KV_HEADS, seq.tile_m, HEAD_DIM), lambda i: (0, i, 0)),
                 pl.BlockSpec((N_KV_HEADS, vt_rows, seq.tile_m), lambda i: (0, 0, i)),
                 seq.tile_spec(POOL_WIDTH)]
    out_shape = [jax.ShapeDtypeStruct((n, ATTN_WIDTH), BF16),
                 jax.ShapeDtypeStruct((N_KV_HEADS, n, HEAD_DIM), BF16),
                 jax.ShapeDtypeStruct((N_KV_HEADS, vt_rows, n), BF16),
                 jax.ShapeDtypeStruct((n, POOL_WIDTH), BF16)]
    if seq.latent:
        tiles_per_seq = seq.seq_len // seq.tile_m
        in_specs += [pl.BlockSpec((seq.tile_m, LANES), lambda i: (i % tiles_per_seq, 0))] * 2
        args += list(rope_tabs)
    else:
        out_specs += [seq.tile_spec(KV_WIDTH)] * 2
        out_shape += [jax.ShapeDtypeStruct((n, KV_WIDTH), F32)] * 2
    return pl.pallas_call(
        functools.partial(_even_in_kernel, seq),
        grid=(seq.n_tiles,),
        in_specs=in_specs,
        out_specs=out_specs,
        out_shape=out_shape,
        compiler_params=_params(1),
        name="even_in",
    )(*args)


def _reduce_rows(x, op, final, chunk=256):
    n = x.shape[0]
    if n > chunk and n % chunk == 0:
        parts = [x[i:i + chunk] for i in range(0, n, chunk)]
        while len(parts) > 1:
            parts = [op(parts[i], parts[i + 1]) if i + 1 < len(parts) else parts[i]
                     for i in range(0, len(parts), 2)]
        x = parts[0]
        n = chunk
    while n > SUBLANES and n % (2 * SUBLANES) == 0:
        n //= 2
        x = op(x[:n], x[n:])
    return final(x, axis=0, keepdims=True)


def _attn_kernel(has_ctx, q_ref, k_ref, vt_ref, *rest):
    if has_ctx:
        kc_ref, vtc_ref, qmax_ref, o_ref, st_a, st_b, p_a, p_b, kmax_ref = rest
    else:
        qmax_ref, o_ref, st_a, st_b, p_a, p_b, kmax_ref = rest
        kc_ref = vtc_ref = None
    group = N_HEADS // N_KV_HEADS
    heads_per_dot = ATTN_HEADS_PER_DOT
    tq = ATTN_SUB_M
    stages = [(r, h0) for r in range(q_ref.shape[0] // tq) for h0 in range(0, N_HEADS, heads_per_dot)]
    st_bufs = (st_a, st_b)
    p_bufs = (p_a, p_b)
    n_new = k_ref.shape[1]
    n_keys = st_a.shape[0]
    chunk = ATTN_KEY_CHUNK
    key_chunks = [(k_ref, c, c) for c in range(0, n_new, chunk)]
    if has_ctx:
        key_chunks += [(kc_ref, c, n_new + c) for c in range(0, n_keys - n_new, chunk)]

    @pl.when(pl.program_id(1) == 0)
    def _():
        kmax = None
        for ref in (k_ref, kc_ref) if has_ctx else (k_ref,):
            for g in range(N_KV_HEADS):
                kk = ref[g].astype(F32)
                part = jnp.max(jnp.sum(kk * kk, axis=-1, keepdims=True))
                kmax = part if kmax is None else jnp.maximum(kmax, part)
        kmax_ref[0] = kmax

    bounded = qmax_ref[0] * kmax_ref[0] <= ATTN_SCORE_BOUND ** 2

    def score_chunks(i):
        r, h0 = stages[i]
        qs = jnp.concatenate([q_ref[r * tq:(r + 1) * tq, h * HEAD_DIM:(h + 1) * HEAD_DIM]
                              for h in range(h0, h0 + heads_per_dot)], axis=0)
        for ref, src, dst in key_chunks:
            yield dst, lax.dot_general(ref[h0 // group, src:src + chunk, :], qs, (((1,), (1,)), ((), ())),
                                       preferred_element_type=F32)

    def scores_bounded(i):
        for dst, st in score_chunks(i):
            p_bufs[i % 2][dst:dst + chunk, :] = jnp.exp2(st).astype(BF16)

    def scores_general(i):
        for dst, st in score_chunks(i):
            st_bufs[i % 2][dst:dst + chunk, :] = st

    def softmax_general(i):
        st_ref, p_ref = st_bufs[i % 2], p_bufs[i % 2]
        m = None
        for c in range(0, n_keys, chunk):
            part = _reduce_rows(st_ref[c:c + chunk, :], jnp.maximum, jnp.max)
            m = part if m is None else jnp.maximum(m, part)
        for c in range(0, n_keys, chunk):
            p_ref[c:c + chunk, :] = jnp.exp2(st_ref[c:c + chunk, :] - m).astype(BF16)

    def run(scores, softmax):
        outs = []
        scores(0)
        for i, (r, h0) in enumerate(stages):
            if i + 1 < len(stages):
                scores(i + 1)
            softmax(i)
            g = h0 // group
            ot = _dot(vt_ref[g], p_bufs[i % 2][:n_new, :])
            if has_ctx:
                ot = ot + _dot(vtc_ref[g], p_bufs[i % 2][n_new:, :])
            ot = ot[:HEAD_DIM] / ot[HEAD_DIM:HEAD_DIM + 1]
            outs += [ot[:, j * tq:(j + 1) * tq] for j in range(heads_per_dot)]
            if h0 + heads_per_dot == N_HEADS:
                o_ref[r * tq:(r + 1) * tq, :] = jnp.concatenate(outs, axis=0).T.astype(BF16)
                outs = []

    @pl.when(bounded)
    def _():
        run(scores_bounded, lambda i: None)

    @pl.when(jnp.logical_not(bounded))
    def _():
        run(scores_general, softmax_general)


def _attention(seq, q, k, vt, ctx, q_gain):
    qmax = (jnp.max(q_gain * q_gain) * (HEAD_DIM * Q_SCALE ** 2 * 1.02)).reshape(1)
    L = seq.seq_len
    vt_rows = HEAD_DIM + ATTN_ONES_ROWS
    tq = ATTN_TILE_M if L % ATTN_TILE_M == 0 else ATTN_SUB_M
    tps = L // tq
    cols = ATTN_HEADS_PER_DOT * ATTN_SUB_M
    in_specs = [
        pl.BlockSpec((tq, ATTN_WIDTH), lambda b, j: (b * tps + j, 0)),
        pl.BlockSpec((N_KV_HEADS, L, HEAD_DIM), lambda b, j: (0, b, 0)),
        pl.BlockSpec((N_KV_HEADS, vt_rows, L), lambda b, j: (0, 0, b)),
    ]
    args = [q, k, vt]
    n_keys = L
    if ctx is not None:
        e, k_ctx, vt_ctx = ctx
        past = k_ctx.shape[3]
        n_keys += past
        in_specs += [
            pl.BlockSpec((None, None, N_KV_HEADS, past, HEAD_DIM), lambda b, j: (b, e, 0, 0, 0)),
            pl.BlockSpec((None, None, N_KV_HEADS, vt_rows, past), lambda b, j: (b, e, 0, 0, 0)),
        ]
        args += [k_ctx, vt_ctx]
    assert L % ATTN_KEY_CHUNK == 0 and n_keys % ATTN_KEY_CHUNK == 0
    return pl.pallas_call(
        functools.partial(_attn_kernel, ctx is not None),
        grid=(seq.n_batch, tps),
        in_specs=in_specs + [pl.BlockSpec(memory_space=pltpu.SMEM)],
        out_specs=pl.BlockSpec((tq, ATTN_WIDTH), lambda b, j: (b * tps + j, 0)),
        out_shape=jax.ShapeDtypeStruct((seq.n_rows, ATTN_WIDTH), BF16),
        scratch_shapes=[pltpu.VMEM((n_keys, cols), F32), pltpu.VMEM((n_keys, cols), F32),
                        pltpu.VMEM((n_keys, cols), BF16), pltpu.VMEM((n_keys, cols), BF16),
                        pltpu.SMEM((1,), F32)],
        compiler_params=_params(2),
        name="attention",
    )(*args, qmax)


def _even_out_kernel(seq, attn_ref, p_ref, pp_ref, pn_ref, x_ref, gate_ref, wp_ref, ps_ref, wo_ref,
                     g_ref, b_ref, o_ref):
    sub = seq.sub_m
    halo = pp_ref.shape[0]
    n_ext = sub + 2 * halo
    row = lax.broadcasted_iota(jnp.int32, (sub, 1), 0)

    def pre(s):
        has_prev, has_next = seq.sub_edges(s)
        lo, hi = seq.neighbours(s, p_ref, pp_ref, pn_ref)
        p = p_ref[seq.sub_rows(s), :].astype(F32)
        ext = jnp.concatenate([jnp.where(has_prev, lo.astype(F32), 0.0), p,
                               jnp.where(has_next, hi.astype(F32), 0.0)], axis=0)
        pos = seq.sub_index(s) * sub + row
        mixed = []
        for gi, w in enumerate(POOL_WINDOWS):
            half = w // 2
            lanes = slice(gi * POOL_GROUP, (gi + 1) * POOL_GROUP)
            run = ext[:, lanes]
            span = 1
            while span < w:
                run = run + pltpu.roll(run, span, 0)
                span *= 2
            if half > 1:
                run = pltpu.roll(run, n_ext - (half - 1), 0)
            total = run[halo:halo + sub]
            cnt = jnp.minimum(pos + half, seq.seq_len) - jnp.maximum(pos - half, 0)
            centred = total / cnt.astype(F32) - p[:, lanes]
            mixed.append(_dot(centred.astype(BF16), wp_ref[gi]))
        pool = jnp.concatenate(mixed, axis=1) * ps_ref[...]
        return jnp.concatenate([attn_ref[seq.sub_rows(s), :], pool.astype(BF16)], axis=1)

    def mid(s, m):
        return _dot(m, wo_ref[...])

    def post(s, y):
        rows = seq.sub_rows(s)
        o_ref[rows, :] = _post_norm(x_ref[rows, :], gate_ref[...], y, g_ref[...], b_ref[...])

    _staggered(seq.n_sub, pre, mid, post)


def _even_out(seq, layer, e, attn, p, x, ada5, w_pool, pool_scale, w_out, ln_g, ln_b):
    prev_spec, next_spec = seq.halo_specs(POOL_WIDTH, HALO_BF16)
    return pl.pallas_call(
        functools.partial(_even_out_kernel, seq),
        grid=(seq.n_tiles,),
        in_specs=[
            seq.tile_spec(ATTN_WIDTH),
            seq.tile_spec(POOL_WIDTH), prev_spec, next_spec,
            seq.tile_spec(D_MODEL),
            seq.mod_spec(layer, 2),
            _resident((None, len(POOL_WINDOWS), POOL_GROUP, POOL_GROUP), lambda i: (e, 0, 0, 0)),
            _row_spec(e, POOL_WIDTH),
            _resident((None, D_MODEL, D_MODEL), lambda i: (layer, 0, 0)),
            _row_spec((layer, 0), D_MODEL, 2),
            _row_spec((layer, 0), D_MODEL, 2),
        ],
        out_specs=seq.tile_spec(D_MODEL),
        out_shape=jax.ShapeDtypeStruct((seq.n_rows, D_MODEL), F32),
        compiler_params=_params(1),
        name="even_out",
    )(attn, p, p, p, x, ada5, w_pool, pool_scale, w_out, ln_g, ln_b)


def _odd_in_kernel(seq, x_ref, sc_ref, sh_ref, w_ref, ch_out, bg_out, f_out):
    c = CONV_WIDTH

    def pre(s):
        return (_layer_norm(x_ref[seq.sub_rows(s), :]) * (1.0 + sc_ref[...]) + sh_ref[...]).astype(BF16)

    def mid(s, u):
        return _dot(u, w_ref[...])

    def post(s, proj):
        rows = seq.sub_rows(s)
        ch_out[rows, :] = (proj[:, 2 * c:3 * c] * proj[:, :c]).astype(BF16)
        bg_out[rows, :] = proj[:, c:2 * c].astype(BF16)
        f_out[rows, :] = proj[:, 3 * c:].astype(BF16)

    _staggered(seq.n_sub, pre, mid, post)


def _odd_in(seq, layer, o, x, ada5, w_in):
    n = seq.n_rows
    return pl.pallas_call(
        functools.partial(_odd_in_kernel, seq),
        grid=(seq.n_tiles,),
        in_specs=[
            seq.tile_spec(D_MODEL),
            seq.mod_spec(layer, 1),
            seq.mod_spec(layer, 0),
            _resident((None, D_MODEL, IN_ODD), lambda i: (o, 0, 0)),
        ],
        out_specs=[seq.tile_spec(CONV_WIDTH), seq.tile_spec(CONV_WIDTH), seq.tile_spec(FOURIER_WIDTH)],
        out_shape=[jax.ShapeDtypeStruct((n, CONV_WIDTH), BF16), jax.ShapeDtypeStruct((n, CONV_WIDTH), BF16),
                   jax.ShapeDtypeStruct((n, FOURIER_WIDTH), BF16)],
        compiler_params=_params(1),
        name="odd_in",
    )(x, ada5, ada5, w_in)


def _fourier_kernel(scale, f_ref, cl_ref, sl_ref, cc_ref, sc_ref, o_ref):
    fb = f_ref[...]
    g_cos = (_dot(fb, cc_ref[...]) * scale).astype(BF16)
    g_sin = (_dot(fb, sc_ref[...]) * scale).astype(BF16)
    o_ref[...] = (_dot(cl_ref[...], g_cos) - _dot(sl_ref[...], g_sin)).astype(BF16)


def _dft_angles(rows, n):
    k = lax.iota(jnp.int32, n)[None, :]
    return ((rows[:, None] * k) % n).astype(F32) * (2.0 * math.pi / n)


def _dft_expand_kernel(hc_ref, hs_ref, lc_ref, ls_ref, c_out, s_out):
    hc, hs = hc_ref[...], hs_ref[...]
    lc, ls = lc_ref[...], ls_ref[...]
    c_out[...] = (hc * lc - hs * ls).astype(BF16)
    s_out[...] = (hs * lc + hc * ls).astype(BF16)


def _dft_tables(n):
    split = FOURIER_GROUP
    assert n % split == 0
    ang_hi = _dft_angles(lax.iota(jnp.int32, n // split) * split, n).reshape(n // split, 1, n)
    ang_lo = _dft_angles(lax.iota(jnp.int32, split), n)
    hi_spec = pl.BlockSpec((None, 1, n), lambda i: (i, 0, 0))
    lo_spec = pl.BlockSpec((split, n), lambda i: (0, 0))
    out_spec = pl.BlockSpec((split, n), lambda i: (i, 0))
    return pl.pallas_call(
        _dft_expand_kernel,
        grid=(n // split,),
        in_specs=[hi_spec, hi_spec, lo_spec, lo_spec],
        out_specs=[out_spec, out_spec],
        out_shape=[jax.ShapeDtypeStruct((n, n), BF16)] * 2,
        compiler_params=_params(1),
        name="dft_tables",
    )(jnp.cos(ang_hi), jnp.sin(ang_hi), jnp.cos(ang_lo), jnp.sin(ang_lo))


def _fourier(seq, f, tabs):
    L = seq.seq_len
    cl, sl, cc, sc = tabs
    scale = 1.0 / math.sqrt(L * FOURIER_GROUP)
    return pl.pallas_call(
        functools.partial(_fourier_kernel, scale),
        grid=(seq.n_batch,),
        in_specs=[
            pl.BlockSpec((L, FOURIER_WIDTH), lambda b: (b, 0)),
            _resident((L, L), lambda b: (0, 0)),
            _resident((L, L), lambda b: (0, 0)),
            _resident((FOURIER_WIDTH, FOURIER_WIDTH), lambda b: (0, 0)),
            _resident((FOURIER_WIDTH, FOURIER_WIDTH), lambda b: (0, 0)),
        ],
        out_specs=pl.BlockSpec((L, FOURIER_WIDTH), lambda b: (b, 0)),
        out_shape=jax.ShapeDtypeStruct((seq.n_rows, FOURIER_WIDTH), BF16),
        compiler_params=_params(1),
        name="fourier",
    )(f, cl, sl, cc, sc)


def _odd_out_kernel(seq, ch_ref, chp_ref, chn_ref, bg_ref, fo_ref, x_ref, gate_ref, cw_ref, cb_ref, wo_ref,
                    g_ref, b_ref, o_ref):
    halo = chp_ref.shape[0]

    def pre(s):
        has_prev, has_next = seq.sub_edges(s)
        rows = seq.sub_rows(s)
        lo, hi = seq.neighbours(s, ch_ref, chp_ref, chn_ref)
        ch = ch_ref[rows, :].astype(F32)
        prev_row = jnp.where(has_prev, lo.astype(F32)[halo - 1:halo, :], 0.0)
        next_row = jnp.where(has_next, hi.astype(F32)[0:1, :], 0.0)
        up, down = _shift_rows(ch, prev_row, next_row)
        conv = cb_ref[...] + up * cw_ref[0:1, :] + ch * cw_ref[1:2, :] + down * cw_ref[2:3, :]
        conv_out = bg_ref[rows, :].astype(F32) * conv
        return jnp.concatenate([conv_out.astype(BF16), fo_ref[rows, :]], axis=1)

    def mid(s, m):
        return _dot(m, wo_ref[...])

    def post(s, y):
        rows = seq.sub_rows(s)
        o_ref[rows, :] = _post_norm(x_ref[rows, :], gate_ref[...], y, g_ref[...], b_ref[...])

    _staggered(seq.n_sub, pre, mid, post)


def _odd_out(seq, layer, o, ch, bg, fo, x, ada5, conv_w, conv_b, w_out, ln_g, ln_b):
    prev_spec, next_spec = seq.halo_specs(CONV_WIDTH, HALO_BF16)
    return pl.pallas_call(
        functools.partial(_odd_out_kernel, seq),
        grid=(seq.n_tiles,),
        in_specs=[
            seq.tile_spec(CONV_WIDTH), prev_spec, next_spec,
            seq.tile_spec(CONV_WIDTH),
            seq.tile_spec(FOURIER_WIDTH),
            seq.tile_spec(D_MODEL),
            seq.mod_spec(layer, 2),
            pl.BlockSpec((None, 3, CONV_WIDTH), lambda i: (o, 0, 0)),
            _row_spec(o, CONV_WIDTH),
            _resident((None, D_MODEL, D_MODEL), lambda i: (layer, 0, 0)),
            _row_spec((layer, 0), D_MODEL, 2),
            _row_spec((layer, 0), D_MODEL, 2),
        ],
        out_specs=seq.tile_spec(D_MODEL),
        out_shape=jax.ShapeDtypeStruct((seq.n_rows, D_MODEL), F32),
        compiler_params=_params(1),
        name="odd_out",
    )(ch, ch, ch, bg, fo, x, ada5, conv_w, conv_b, w_out, ln_g, ln_b)


def _ffn_kernel(seq, x_ref, xp_ref, xn_ref, sc_ref, sh_ref, gate_ref, wa_ref, wg_ref, cw_ref, cb_ref, wd_ref,
                g_ref, b_ref, o_ref):
    halo = xp_ref.shape[0]
    sub = seq.sub_m
    n_sub = seq.n_sub
    n_ext = sub + 2 * halo
    row = lax.broadcasted_iota(jnp.int32, (n_ext, 1), 0)

    def stage_up(s):
        has_prev, has_next = seq.sub_edges(s)
        lo, hi = seq.neighbours(s, x_ref, xp_ref, xn_ref)
        x = x_ref[seq.sub_rows(s), :]
        u_ext = (_layer_norm(jnp.concatenate([lo, x, hi], axis=0)) * (1.0 + sc_ref[...]) + sh_ref[...]).astype(BF16)
        a_ext = _dot(u_ext, wa_ref[...])
        gate_lin = _dot(u_ext[halo:halo + sub], wg_ref[...])
        inside = ((row >= halo) | has_prev) & ((row < halo + sub) | has_next)
        return x, jnp.where(inside, a_ext, 0.0), gate_lin

    def stage_hidden(a_ext, gate_lin):
        up = pltpu.roll(a_ext, 1, 0)[halo:halo + sub]
        mid = a_ext[halo:halo + sub]
        down = pltpu.roll(a_ext, n_ext - 1, 0)[halo:halo + sub]
        conv = cb_ref[...] + up * cw_ref[0:1, :] + mid * cw_ref[1:2, :] + down * cw_ref[2:3, :]
        return (conv * jax.nn.sigmoid(conv) * gate_lin).astype(BF16)

    def stage_out(s, x, y):
        o_ref[seq.sub_rows(s), :] = _post_norm(x, gate_ref[...], y, g_ref[...], b_ref[...])

    ups = {0: stage_up(0)}
    ys = {}
    for s in range(n_sub):
        if s + 1 < n_sub:
            ups[s + 1] = stage_up(s + 1)
        x, a_ext, gate_lin = ups.pop(s)
        h = stage_hidden(a_ext, gate_lin)
        if s > 0:
            stage_out(s - 1, *ys.pop(s - 1))
        ys[s] = (x, _dot(h, wd_ref[...]))
    stage_out(n_sub - 1, *ys.pop(n_sub - 1))


def _ffn(seq, layer, x, ada5, w_up, ffn_conv_w, ffn_conv_b, w_down, ln_g, ln_b):
    prev_spec, next_spec = seq.halo_specs(D_MODEL, HALO_F32)
    return pl.pallas_call(
        functools.partial(_ffn_kernel, seq),
        grid=(seq.n_tiles,),
        in_specs=[
            seq.tile_spec(D_MODEL), prev_spec, next_spec,
            seq.mod_spec(layer, 4),
            seq.mod_spec(layer, 3),
            seq.mod_spec(layer, 5),
            _resident((None, D_MODEL, D_FF), lambda i: (layer, 0, 0)),
            _resident((None, D_MODEL, D_FF), lambda i: (layer, 0, 1)),
            pl.BlockSpec((None, 3, D_FF), lambda i: (layer, 0, 0)),
            _row_spec(layer, D_FF),
            _resident((None, D_FF, D_MODEL), lambda i: (layer, 0, 0)),
            _row_spec((layer, 1), D_MODEL, 2),
            _row_spec((layer, 1), D_MODEL, 2),
        ],
        out_specs=seq.tile_spec(D_MODEL),
        out_shape=jax.ShapeDtypeStruct((seq.n_rows, D_MODEL), F32),
        compiler_params=_params(1),
        name="conv_ffn",
    )(x, x, x, ada5, ada5, ada5, w_up, w_up, ffn_conv_w, ffn_conv_b, w_down, ln_g, ln_b)


def _rope_tables(seq_len):
    t = lax.iota(jnp.int32, seq_len)
    row = (t // GRID_W).astype(F32)
    col = (t % GRID_W).astype(F32)
    n_freq = HEAD_DIM // 4
    inv = 1.0 / (ROPE_THETA ** (jnp.arange(n_freq, dtype=F32) / n_freq))
    ang = jnp.concatenate([row[:, None] * inv, col[:, None] * inv], -1)
    cos = jnp.repeat(jnp.cos(ang), 2, axis=1)
    sin = jnp.repeat(jnp.sin(ang), 2, axis=1) * jnp.tile(jnp.array([-1.0, 1.0], F32), HEAD_DIM // 2)
    reps = LANES // HEAD_DIM
    return jnp.tile(cos, (1, reps)), jnp.tile(sin, (1, reps))


def _fourier_tables(seq_len):
    cl, sl = _dft_tables(seq_len)
    ang = _dft_angles(lax.iota(jnp.int32, FOURIER_GROUP), FOURIER_GROUP)
    eye = jnp.eye(N_FOURIER_GROUPS, dtype=F32)
    return cl, sl, jnp.kron(eye, jnp.cos(ang)).astype(BF16), jnp.kron(eye, jnp.sin(ang)).astype(BF16)


def _run_trunk(seq, x, ada5, ctx_k, ctx_v, wts):
    (w_in_even, gq, gk, bd, w_pool, pool_scale, w_in_odd, conv_w, conv_b, w_out, w_up, ffn_conv_w,
     ffn_conv_b, w_down, ln_g, ln_b) = wts
    rope_tabs = _rope_tables(seq.seq_len) if seq.latent else None
    four_tabs = _fourier_tables(seq.seq_len)
    new_k, new_v = [], []
    for layer in range(DEPTH):
        if layer % 2 == 0:
            e = layer // 2
            outs = _even_in(seq, layer, e, x, ada5, w_in_even, gq, gk, bd, rope_tabs)
            q, k, vt, p = outs[:4]
            if not seq.latent:
                new_k.append(outs[4])
                new_v.append(outs[5])
            attn = _attention(seq, q, k, vt, (e, ctx_k, ctx_v) if seq.latent else None, gq[e])
            x = _even_out(seq, layer, e, attn, p, x, ada5, w_pool, pool_scale, w_out, ln_g, ln_b)
        else:
            o = layer // 2
            ch, bg, f = _odd_in(seq, layer, o, x, ada5, w_in_odd)
            fo = _fourier(seq, f, four_tabs)
            x = _odd_out(seq, layer, o, ch, bg, fo, x, ada5, conv_w, conv_b, w_out, ln_g, ln_b)
        x = _ffn(seq.with_tile(FFN_TILE_M, SUB_M), layer, x, ada5, w_up, ffn_conv_w, ffn_conv_b, w_down,
                 ln_g, ln_b)
    return x, new_k, new_v


def kernel(x_prompt, x_sample, cache_k, cache_v, c, c_ctx, w_ada, b_ada, w_in_even, q_norm_g, k_norm_g,
           w_pool, pool_scale, w_in_odd, conv_w, conv_b, w_out, w_up, ffn_conv_w, ffn_conv_b, w_down,
           ln_g, ln_b):
    n_prompt, prompt_len, _ = x_prompt.shape
    n_sample, sample_len, _ = x_sample.shape
    n_even = w_in_even.shape[0]
    n_odd = w_in_odd.shape[0]
    assert n_sample <= CTX_ROW

    cond = jnp.zeros((COND_ROWS, D_MODEL), F32).at[:n_sample].set(c).at[CTX_ROW].set(c_ctx)
    ada = _ada_all(cond, w_ada, b_ada)
    ada5 = ada.reshape(DEPTH, COND_ROWS, 6, 1, D_MODEL)

    head_of = lax.iota(jnp.int32, ATTN_WIDTH) // HEAD_DIM
    bd = (head_of[:, None] == head_of[None, :]).astype(BF16)
    wts = (
        w_in_even.astype(BF16),
        jnp.tile(q_norm_g, (1, N_HEADS)).reshape(n_even, 1, ATTN_WIDTH),
        jnp.tile(k_norm_g, (1, N_KV_HEADS)).reshape(n_even, 1, KV_WIDTH),
        bd,
        w_pool.astype(BF16),
        pool_scale.reshape(n_even, 1, POOL_WIDTH),
        w_in_odd.astype(BF16),
        conv_w,
        conv_b.reshape(n_odd, 1, CONV_WIDTH),
        w_out.astype(BF16),
        w_up.astype(BF16),
        ffn_conv_w,
        ffn_conv_b.reshape(DEPTH, 1, D_FF),
        w_down.astype(BF16),
        ln_g.reshape(DEPTH, 2, 1, D_MODEL),
        ln_b.reshape(DEPTH, 2, 1, D_MODEL),
    )

    prompt = _Seq(n_prompt, prompt_len, latent=False)
    y_prompt, ks, vs = _run_trunk(prompt, x_prompt.reshape(-1, D_MODEL), ada5, None, None, wts)
    cache_shape = (n_prompt, prompt_len, N_KV_HEADS, HEAD_DIM)
    new_cache_k = jnp.stack([k.reshape(cache_shape) for k in ks], 1)
    new_cache_v = jnp.stack([v.reshape(cache_shape) for v in vs], 1)

    sample = _Seq(n_sample, sample_len, latent=True)
    past_len = cache_k.shape[2]
    ctx_k = cache_k.astype(BF16).transpose(0, 1, 3, 2, 4)
    ones = jnp.ones((n_sample, n_even, N_KV_HEADS, ATTN_ONES_ROWS, past_len), BF16)
    ctx_v = jnp.concatenate([cache_v.astype(BF16).transpose(0, 1, 3, 4, 2), ones], axis=3)
    y_sample, _, _ = _run_trunk(sample, x_sample.reshape(-1, D_MODEL), ada5, ctx_k, ctx_v, wts)

    return (y_prompt.reshape(x_prompt.shape), y_sample.reshape(x_sample.shape), new_cache_k, new_cache_v)
```

```python
import functools
import math

import jax
import jax.numpy as jnp
from jax import lax
from jax.experimental import pallas as pl
from jax.experimental.pallas import tpu as pltpu

D_MODEL = 1024
DEPTH = 4
GRID_W = 64
N_HEADS = 8
N_KV_HEADS = 2
HEAD_DIM = 64
ATTN_WIDTH = N_HEADS * HEAD_DIM
KV_WIDTH = N_KV_HEADS * HEAD_DIM
POOL_WIDTH = D_MODEL - ATTN_WIDTH
POOL_WINDOWS = (2, 4, 8, 16)
POOL_GROUP = POOL_WIDTH // len(POOL_WINDOWS)
FOURIER_WIDTH = D_MODEL // 4
N_FOURIER_GROUPS = 4
FOURIER_GROUP = FOURIER_WIDTH // N_FOURIER_GROUPS
CONV_WIDTH = D_MODEL - FOURIER_WIDTH
D_FF = 2816
ROPE_THETA = 10000.0
LN_EPS = 1e-6
RMS_EPS = 1e-6
IN_EVEN = ATTN_WIDTH + 2 * KV_WIDTH + POOL_WIDTH
IN_ODD = 3 * CONV_WIDTH + FOURIER_WIDTH
DEEPNORM_ALPHA = (2 * DEPTH) ** 0.25
Q_SCALE = HEAD_DIM ** -0.5 * math.log2(math.e)

SUBLANES = 8
LANES = 128
VMEM_LIMIT_BYTES = 56 * 1024 * 1024

TILE_M = 1024
FFN_TILE_M = 1024
SUB_M = 256
ATTN_TILE_M = 512
ATTN_SUB_M = 256
ATTN_ONES_ROWS = 16
ATTN_HEADS_PER_DOT = 2
ATTN_KEY_CHUNK = 256
ATTN_SCORE_BOUND = 80.0
HALO_F32 = SUBLANES
HALO_BF16 = 2 * SUBLANES
DFT_HI_ROWS_PER_STEP = 4
COND_ROWS = 16
CTX_ROW = 8

F32 = jnp.float32
BF16 = jnp.bfloat16


def _params(n_axes):
    return pltpu.CompilerParams(dimension_semantics=("arbitrary",) * n_axes,
                                vmem_limit_bytes=VMEM_LIMIT_BYTES)


def _resident(block_shape, index_map):
    return pl.BlockSpec(block_shape, index_map, pipeline_mode=pl.Buffered(1))


def _dot(a, b):
    return jnp.dot(a, b, preferred_element_type=F32)


def _layer_norm(x):
    mu = jnp.mean(x, axis=-1, keepdims=True)
    xc = x - mu
    var = jnp.mean(xc * xc, axis=-1, keepdims=True)
    return xc * lax.rsqrt(var + LN_EPS)


def _staggered(n, pre, mid, post):
    state = pre(0)
    done = None
    for s in range(n):
        cur = mid(s, state)
        if s + 1 < n:
            state = pre(s + 1)
        if done is not None:
            post(s - 1, done)
        done = cur
    post(n - 1, done)


def _ada_kernel(cond_ref, w_ref, b_ref, o_ref):
    cnd = cond_ref[...]
    act = (cnd * jax.nn.sigmoid(cnd)).astype(BF16)
    o_ref[...] = _dot(act, w_ref[...].astype(BF16)) + b_ref[...]


def _ada_all(cond, w_ada, b_ada):
    tn = 3072
    n_out = 6 * D_MODEL
    return pl.pallas_call(
        _ada_kernel,
        grid=(DEPTH, n_out // tn),
        in_specs=[
            pl.BlockSpec((COND_ROWS, D_MODEL), lambda l, j: (0, 0)),
            pl.BlockSpec((None, D_MODEL, tn), lambda l, j: (l, 0, j)),
            pl.BlockSpec((None, 1, tn), lambda l, j: (l, 0, j)),
        ],
        out_specs=pl.BlockSpec((None, COND_ROWS, tn), lambda l, j: (l, 0, j)),
        out_shape=jax.ShapeDtypeStruct((DEPTH, COND_ROWS, n_out), F32),
        compiler_params=_params(2),
        name="ada",
    )(cond, w_ada, b_ada.reshape(DEPTH, 1, n_out))


def _sub_sizes(tile_m, seq_len):
    unit = min(SUB_M, seq_len)
    assert tile_m % unit == 0 and seq_len % unit == 0
    return (unit,) * (tile_m // unit)


class _Seq:
    def __init__(self, n_batch, seq_len, latent, tile_m=TILE_M):
        self.n_rows = n_batch * seq_len
        assert self.n_rows % tile_m == 0
        assert seq_len % tile_m == 0 or not latent
        self.n_batch = n_batch
        self.seq_len = seq_len
        self.latent = latent
        self.tile_m = tile_m
        self.sub_sizes = _sub_sizes(tile_m, seq_len)
        self.sub_starts = tuple(sum(self.sub_sizes[:s]) for s in range(len(self.sub_sizes)))
        self.n_sub = len(self.sub_sizes)
        self.n_tiles = self.n_rows // tile_m

    def with_tile(self, tile_m):
        return _Seq(self.n_batch, self.seq_len, self.latent, tile_m)

    def cond_row(self, i):
        return (i * self.tile_m) // self.seq_len if self.latent else CTX_ROW

    def tile_spec(self, width):
        return pl.BlockSpec((self.tile_m, width), lambda i: (i, 0))

    def halo_specs(self, width, halo):
        per_tile = self.tile_m // halo
        last = self.n_rows // halo - 1
        prev = pl.BlockSpec((halo, width), lambda i: (jnp.maximum(i * per_tile - 1, 0), 0))
        nxt = pl.BlockSpec((halo, width), lambda i: (jnp.minimum((i + 1) * per_tile, last), 0))
        return prev, nxt

    def mod_spec(self, layer, which):
        return pl.BlockSpec((None, None, None, 1, D_MODEL),
                            lambda i: (layer, self.cond_row(i), which, 0, 0))

    def sub_pos(self, s):
        if self.sub_sizes[s] == self.seq_len:
            return 0
        return (pl.program_id(0) * self.tile_m + self.sub_starts[s]) % self.seq_len

    def sub_edges(self, s):
        if self.sub_sizes[s] == self.seq_len:
            return False, False
        pos = self.sub_pos(s)
        return pos > 0, pos + self.sub_sizes[s] < self.seq_len

    def sub_rows(self, s):
        return slice(self.sub_starts[s], self.sub_starts[s] + self.sub_sizes[s])

    def neighbours(self, s, ref, prev_ref, next_ref):
        halo = prev_ref.shape[0]
        start, stop = self.sub_starts[s], self.sub_starts[s] + self.sub_sizes[s]
        lo = ref[start - halo:start, :] if s > 0 else prev_ref[...]
        hi = ref[stop:stop + halo, :] if s < self.n_sub - 1 else next_ref[...]
        return lo, hi


def _row_spec(layer, width, n_lead=1):
    if n_lead == 1:
        return pl.BlockSpec((None, 1, width), lambda i: (layer, 0, 0))
    return pl.BlockSpec((None, None, 1, width), lambda i: (layer[0], layer[1], 0, 0))


def _shift_rows(cur, prev_row, next_row):
    n = cur.shape[0]
    row = lax.broadcasted_iota(jnp.int32, (n, 1), 0)
    up = jnp.where(row == 0, prev_row, pltpu.roll(cur, 1, 0))
    down = jnp.where(row == n - 1, next_row, pltpu.roll(cur, n - 1, 0))
    return up, down


def _post_norm(x, gate, y, g, b):
    return _layer_norm(DEEPNORM_ALPHA * x + gate * y) * g + b


def _even_in_kernel(seq, x_ref, sc_ref, sh_ref, w_ref, gq_ref, gk_ref, bd_ref, *rest):
    if seq.latent:
        cos_ref, sin_ref, q_out, k_out, vt_out, p_out = rest
    else:
        q_out, k_out, vt_out, p_out, kraw_out, vraw_out = rest
    o1 = ATTN_WIDTH
    o2 = o1 + KV_WIDTH
    o3 = o2 + KV_WIDTH

    def rope(t, cos, sin):
        even_lane = (lax.broadcasted_iota(jnp.int32, cos.shape, 1) & 1) == 0
        outs = []
        for j in range(t.shape[1] // LANES):
            slab = t[:, j * LANES:(j + 1) * LANES]
            partner = jnp.where(even_lane, pltpu.roll(slab, LANES - 1, 1), pltpu.roll(slab, 1, 1))
            outs.append(slab * cos + partner * sin)
        return outs[0] if len(outs) == 1 else jnp.concatenate(outs, axis=1)

    def pre(s):
        return (_layer_norm(x_ref[seq.sub_rows(s), :]) * (1.0 + sc_ref[...]) + sh_ref[...]).astype(BF16)

    def mid(s, u):
        return _dot(u, w_ref[...])

    def post(s, proj):
        rows = seq.sub_rows(s)
        q = proj[:, :o1]
        k = proj[:, o1:o2]
        v = proj[:, o2:o3]
        p_out[rows, :] = proj[:, o3:].astype(BF16)
        bd = bd_ref[...]
        q = q * lax.rsqrt(_dot((q * q).astype(BF16), bd) * (1.0 / HEAD_DIM) + RMS_EPS) * gq_ref[...]
        k = k * lax.rsqrt(_dot((k * k).astype(BF16), bd[:KV_WIDTH, :KV_WIDTH]) * (1.0 / HEAD_DIM)
                          + RMS_EPS) * gk_ref[...]
        if seq.latent:
            cos = cos_ref[rows, :]
            sin = sin_ref[rows, :]
            q = rope(q, cos, sin)
            k = rope(k, cos, sin)
        else:
            kraw_out[rows, :] = k
            vraw_out[rows, :] = v
        q_out[rows, :] = (q * Q_SCALE).astype(BF16)
        vt = v.T.astype(BF16)
        for g in range(N_KV_HEADS):
            heads = slice(g * HEAD_DIM, (g + 1) * HEAD_DIM)
            k_out[g, rows, :] = k[:, heads].astype(BF16)
            vt_out[g, :HEAD_DIM, rows] = vt[heads, :]
            vt_out[g, HEAD_DIM:, rows] = jnp.ones((ATTN_ONES_ROWS, seq.sub_sizes[s]), BF16)

    _staggered(seq.n_sub, pre, mid, post)


def _even_in(seq, layer, e, x, ada5, w_in, gq, gk, bd, rope_tabs):
    n = seq.n_rows
    in_specs = [
        seq.tile_spec(D_MODEL),
        seq.mod_spec(layer, 1),
        seq.mod_spec(layer, 0),
        _resident((None, D_MODEL, IN_EVEN), lambda i: (e, 0, 0)),
        _row_spec(e, ATTN_WIDTH),
        _row_spec(e, KV_WIDTH),
        _resident((ATTN_WIDTH, ATTN_WIDTH), lambda i: (0, 0)),
    ]
    args = [x, ada5, ada5, w_in, gq, gk, bd]
    vt_rows = HEAD_DIM + ATTN_ONES_ROWS
    out_specs = [seq.tile_spec(ATTN_WIDTH),
                 pl.BlockSpec((N_KV_HEADS, seq.tile_m, HEAD_DIM), lambda i: (0, i, 0)),
                 pl.BlockSpec((N_KV_HEADS, vt_rows, seq.tile_m), lambda i: (0, 0, i)),
                 seq.tile_spec(POOL_WIDTH)]
    out_shape = [jax.ShapeDtypeStruct((n, ATTN_WIDTH), BF16),
                 jax.ShapeDtypeStruct((N_KV_HEADS, n, HEAD_DIM), BF16),
                 jax.ShapeDtypeStruct((N_KV_HEADS, vt_rows, n), BF16),
                 jax.ShapeDtypeStruct((n, POOL_WIDTH), BF16)]
    if seq.latent:
        tiles_per_seq = seq.seq_len // seq.tile_m
        in_specs += [pl.BlockSpec((seq.tile_m, LANES), lambda i: (i % tiles_per_seq, 0))] * 2
        args += list(rope_tabs)
    else:
        out_specs += [seq.tile_spec(KV_WIDTH)] * 2
        out_shape += [jax.ShapeDtypeStruct((n, KV_WIDTH), F32)] * 2
    return pl.pallas_call(
        functools.partial(_even_in_kernel, seq),
        grid=(seq.n_tiles,),
        in_specs=in_specs,
        out_specs=out_specs,
        out_shape=out_shape,
        compiler_params=_params(1),
        name="even_in",
    )(*args)


def _reduce_rows(x, op, final, chunk=256):
    n = x.shape[0]
    if n > chunk and n % chunk == 0:
        parts = [x[i:i + chunk] for i in range(0, n, chunk)]
        while len(parts) > 1:
            parts = [op(parts[i], parts[i + 1]) if i + 1 < len(parts) else parts[i]
                     for i in range(0, len(parts), 2)]
        x = parts[0]
        n = chunk
    while n > SUBLANES and n % (2 * SUBLANES) == 0:
        n //= 2
        x = op(x[:n], x[n:])
    return final(x, axis=0, keepdims=True)


def _attn_kernel(has_ctx, q_ref, k_ref, vt_ref, *rest):
    if has_ctx:
        kc_ref, vtc_ref, qmax_ref, o_ref, st_a, st_b, p_a, p_b, kmax_ref = rest
    else:
        qmax_ref, o_ref, st_a, st_b, p_a, p_b, kmax_ref = rest
        kc_ref = vtc_ref = None
    group = N_HEADS // N_KV_HEADS
    heads_per_dot = ATTN_HEADS_PER_DOT
    tq = ATTN_SUB_M
    stages = [(r, h0) for r in range(q_ref.shape[0] // tq) for h0 in range(0, N_HEADS, heads_per_dot)]
    st_bufs = (st_a, st_b)
    p_bufs = (p_a, p_b)
    n_new = k_ref.shape[1]
    n_keys = st_a.shape[0]
    chunk = ATTN_KEY_CHUNK
    key_chunks = [(k_ref, c, c) for c in range(0, n_new, chunk)]
    if has_ctx:
        key_chunks += [(kc_ref, c, n_new + c) for c in range(0, n_keys - n_new, chunk)]

    @pl.when(pl.program_id(1) == 0)
    def _():
        kmax = None
        for ref in (k_ref, kc_ref) if has_ctx else (k_ref,):
            for g in range(N_KV_HEADS):
                kk = ref[g].astype(F32)
                part = jnp.max(jnp.sum(kk * kk, axis=-1, keepdims=True))
                kmax = part if kmax is None else jnp.maximum(kmax, part)
        kmax_ref[0] = kmax

    bounded = qmax_ref[0] * kmax_ref[0] <= ATTN_SCORE_BOUND ** 2

    def score_chunks(i):
        r, h0 = stages[i]
        qs = jnp.concatenate([q_ref[r * tq:(r + 1) * tq, h * HEAD_DIM:(h + 1) * HEAD_DIM]
                              for h in range(h0, h0 + heads_per_dot)], axis=0)
        for ref, src, dst in key_chunks:
            yield dst, lax.dot_general(ref[h0 // group, src:src + chunk, :], qs, (((1,), (1,)), ((), ())),
                                       preferred_element_type=F32)

    def scores_bounded(i):
        for dst, st in score_chunks(i):
            p_bufs[i % 2][dst:dst + chunk, :] = jnp.exp2(st).astype(BF16)

    def scores_general(i):
        for dst, st in score_chunks(i):
            st_bufs[i % 2][dst:dst + chunk, :] = st

    def softmax_general(i):
        st_ref, p_ref = st_bufs[i % 2], p_bufs[i % 2]
        m = None
        for c in range(0, n_keys, chunk):
            part = _reduce_rows(st_ref[c:c + chunk, :], jnp.maximum, jnp.max)
            m = part if m is None else jnp.maximum(m, part)
        for c in range(0, n_keys, chunk):
            p_ref[c:c + chunk, :] = jnp.exp2(st_ref[c:c + chunk, :] - m).astype(BF16)

    def run(scores, softmax):
        outs = []
        scores(0)
        for i, (r, h0) in enumerate(stages):
            if i + 1 < len(stages):
                scores(i + 1)
            softmax(i)
            g = h0 // group
            ot = _dot(vt_ref[g], p_bufs[i % 2][:n_new, :])
            if has_ctx:
                ot = ot + _dot(vtc_ref[g], p_bufs[i % 2][n_new:, :])
            ot = ot[:HEAD_DIM] / ot[HEAD_DIM:HEAD_DIM + 1]
            outs += [ot[:, j * tq:(j + 1) * tq] for j in range(heads_per_dot)]
            if h0 + heads_per_dot == N_HEADS:
                o_ref[r * tq:(r + 1) * tq, :] = jnp.concatenate(outs, axis=0).T.astype(BF16)
                outs = []

    @pl.when(bounded)
    def _():
        run(scores_bounded, lambda i: None)

    @pl.when(jnp.logical_not(bounded))
    def _():
        run(scores_general, softmax_general)


def _attention(seq, q, k, vt, ctx, q_gain):
    qmax = (jnp.max(q_gain * q_gain) * (HEAD_DIM * Q_SCALE ** 2 * 1.02)).reshape(1)
    L = seq.seq_len
    vt_rows = HEAD_DIM + ATTN_ONES_ROWS
    tq = ATTN_TILE_M if L % ATTN_TILE_M == 0 else ATTN_SUB_M
    tps = L // tq
    cols = ATTN_HEADS_PER_DOT * ATTN_SUB_M
    in_specs = [
        pl.BlockSpec((tq, ATTN_WIDTH), lambda b, j: (b * tps + j, 0)),
        pl.BlockSpec((N_KV_HEADS, L, HEAD_DIM), lambda b, j: (0, b, 0)),
        pl.BlockSpec((N_KV_HEADS, vt_rows, L), lambda b, j: (0, 0, b)),
    ]
    args = [q, k, vt]
    n_keys = L
    if ctx is not None:
        e, k_ctx, vt_ctx = ctx
        past = k_ctx.shape[3]
        n_keys += past
        in_specs += [
            pl.BlockSpec((None, None, N_KV_HEADS, past, HEAD_DIM), lambda b, j: (b, e, 0, 0, 0)),
            pl.BlockSpec((None, None, N_KV_HEADS, vt_rows, past), lambda b, j: (b, e, 0, 0, 0)),
        ]
        args += [k_ctx, vt_ctx]
    assert L % ATTN_KEY_CHUNK == 0 and n_keys % ATTN_KEY_CHUNK == 0
    return pl.pallas_call(
        functools.partial(_attn_kernel, ctx is not None),
        grid=(seq.n_batch, tps),
        in_specs=in_specs + [pl.BlockSpec(memory_space=pltpu.SMEM)],
        out_specs=pl.BlockSpec((tq, ATTN_WIDTH), lambda b, j: (b * tps + j, 0)),
        out_shape=jax.ShapeDtypeStruct((seq.n_rows, ATTN_WIDTH), BF16),
        scratch_shapes=[pltpu.VMEM((n_keys, cols), F32), pltpu.VMEM((n_keys, cols), F32),
                        pltpu.VMEM((n_keys, cols), BF16), pltpu.VMEM((n_keys, cols), BF16),
                        pltpu.SMEM((1,), F32)],
        compiler_params=_params(2),
        name="attention",
    )(*args, qmax)


def _even_out_kernel(seq, attn_ref, p_ref, pp_ref, pn_ref, x_ref, gate_ref, wp_ref, ps_ref, wo_ref,
                     g_ref, b_ref, o_ref):
    halo = pp_ref.shape[0]

    def pre(s):
        sub = seq.sub_sizes[s]
        n_ext = sub + 2 * halo
        has_prev, has_next = seq.sub_edges(s)
        lo, hi = seq.neighbours(s, p_ref, pp_ref, pn_ref)
        p = p_ref[seq.sub_rows(s), :].astype(F32)
        ext = jnp.concatenate([jnp.where(has_prev, lo.astype(F32), 0.0), p,
                               jnp.where(has_next, hi.astype(F32), 0.0)], axis=0)
        pos = seq.sub_pos(s) + lax.broadcasted_iota(jnp.int32, (sub, 1), 0)
        mixed = []
        for gi, w in enumerate(POOL_WINDOWS):
            half = w // 2
            lanes = slice(gi * POOL_GROUP, (gi + 1) * POOL_GROUP)
            run = ext[:, lanes]
            span = 1
            while span < w:
                run = run + pltpu.roll(run, span, 0)
                span *= 2
            if half > 1:
                run = pltpu.roll(run, n_ext - (half - 1), 0)
            total = run[halo:halo + sub]
            cnt = jnp.minimum(pos + half, seq.seq_len) - jnp.maximum(pos - half, 0)
            centred = total / cnt.astype(F32) - p[:, lanes]
            mixed.append(_dot(centred.astype(BF16), wp_ref[gi]))
        pool = jnp.concatenate(mixed, axis=1) * ps_ref[...]
        return jnp.concatenate([attn_ref[seq.sub_rows(s), :], pool.astype(BF16)], axis=1)

    def mid(s, m):
        return _dot(m, wo_ref[...])

    def post(s, y):
        rows = seq.sub_rows(s)
        o_ref[rows, :] = _post_norm(x_ref[rows, :], gate_ref[...], y, g_ref[...], b_ref[...])

    _staggered(seq.n_sub, pre, mid, post)


def _even_out(seq, layer, e, attn, p, x, ada5, w_pool, pool_scale, w_out, ln_g, ln_b):
    prev_spec, next_spec = seq.halo_specs(POOL_WIDTH, HALO_BF16)
    return pl.pallas_call(
        functools.partial(_even_out_kernel, seq),
        grid=(seq.n_tiles,),
        in_specs=[
            seq.tile_spec(ATTN_WIDTH),
            seq.tile_spec(POOL_WIDTH), prev_spec, next_spec,
            seq.tile_spec(D_MODEL),
            seq.mod_spec(layer, 2),
            _resident((None, len(POOL_WINDOWS), POOL_GROUP, POOL_GROUP), lambda i: (e, 0, 0, 0)),
            _row_spec(e, POOL_WIDTH),
            _resident((None, D_MODEL, D_MODEL), lambda i: (layer, 0, 0)),
            _row_spec((layer, 0), D_MODEL, 2),
            _row_spec((layer, 0), D_MODEL, 2),
        ],
        out_specs=seq.tile_spec(D_MODEL),
        out_shape=jax.ShapeDtypeStruct((seq.n_rows, D_MODEL), F32),
        compiler_params=_params(1),
        name="even_out",
    )(attn, p, p, p, x, ada5, w_pool, pool_scale, w_out, ln_g, ln_b)


def _odd_in_kernel(seq, x_ref, sc_ref, sh_ref, w_ref, ch_out, bg_out, f_out):
    c = CONV_WIDTH

    def pre(s):
        return (_layer_norm(x_ref[seq.sub_rows(s), :]) * (1.0 + sc_ref[...]) + sh_ref[...]).astype(BF16)

    def mid(s, u):
        return _dot(u, w_ref[...])

    def post(s, proj):
        rows = seq.sub_rows(s)
        ch_out[rows, :] = (proj[:, 2 * c:3 * c] * proj[:, :c]).astype(BF16)
        bg_out[rows, :] = proj[:, c:2 * c].astype(BF16)
        f_out[rows, :] = proj[:, 3 * c:].astype(BF16)

    _staggered(seq.n_sub, pre, mid, post)


def _odd_in(seq, layer, o, x, ada5, w_in):
    n = seq.n_rows
    return pl.pallas_call(
        functools.partial(_odd_in_kernel, seq),
        grid=(seq.n_tiles,),
        in_specs=[
            seq.tile_spec(D_MODEL),
            seq.mod_spec(layer, 1),
            seq.mod_spec(layer, 0),
            _resident((None, D_MODEL, IN_ODD), lambda i: (o, 0, 0)),
        ],
        out_specs=[seq.tile_spec(CONV_WIDTH), seq.tile_spec(CONV_WIDTH), seq.tile_spec(FOURIER_WIDTH)],
        out_shape=[jax.ShapeDtypeStruct((n, CONV_WIDTH), BF16), jax.ShapeDtypeStruct((n, CONV_WIDTH), BF16),
                   jax.ShapeDtypeStruct((n, FOURIER_WIDTH), BF16)],
        compiler_params=_params(1),
        name="odd_in",
    )(x, ada5, ada5, w_in)


def _fourier_kernel(scale, f_ref, cl_ref, sl_ref, cc_ref, sc_ref, o_ref):
    fb = f_ref[...]
    g_cos = (_dot(fb, cc_ref[...]) * scale).astype(BF16)
    g_sin = (_dot(fb, sc_ref[...]) * scale).astype(BF16)
    o_ref[...] = (_dot(cl_ref[...], g_cos) - _dot(sl_ref[...], g_sin)).astype(BF16)


def _dft_angles(rows, n):
    k = lax.iota(jnp.int32, n)[None, :]
    return ((rows[:, None] * k) % n).astype(F32) * (2.0 * math.pi / n)


def _dft_expand_kernel(hc_ref, hs_ref, lc_ref, ls_ref, c_out, s_out):
    lc, ls = lc_ref[...], ls_ref[...]
    split = lc.shape[0]
    for r in range(hc_ref.shape[0]):
        hc, hs = hc_ref[r], hs_ref[r]
        c_out[r * split:(r + 1) * split, :] = (hc * lc - hs * ls).astype(BF16)
        s_out[r * split:(r + 1) * split, :] = (hs * lc + hc * ls).astype(BF16)


def _dft_tables(n):
    split = FOURIER_GROUP
    assert n % split == 0
    ang_hi = _dft_angles(lax.iota(jnp.int32, n // split) * split, n).reshape(n // split, 1, n)
    ang_lo = _dft_angles(lax.iota(jnp.int32, split), n)
    per_step = min(DFT_HI_ROWS_PER_STEP, n // split)
    assert (n // split) % per_step == 0
    hi_spec = pl.BlockSpec((per_step, 1, n), lambda i: (i, 0, 0))
    lo_spec = pl.BlockSpec((split, n), lambda i: (0, 0))
    out_spec = pl.BlockSpec((per_step * split, n), lambda i: (i, 0))
    return pl.pallas_call(
        _dft_expand_kernel,
        grid=(n // split // per_step,),
        in_specs=[hi_spec, hi_spec, lo_spec, lo_spec],
        out_specs=[out_spec, out_spec],
        out_shape=[jax.ShapeDtypeStruct((n, n), BF16)] * 2,
        compiler_params=_params(1),
        name="dft_tables",
    )(jnp.cos(ang_hi), jnp.sin(ang_hi), jnp.cos(ang_lo), jnp.sin(ang_lo))


def _fourier(seq, f, tabs):
    L = seq.seq_len
    cl, sl, cc, sc = tabs
    scale = 1.0 / math.sqrt(L * FOURIER_GROUP)
    return pl.pallas_call(
        functools.partial(_fourier_kernel, scale),
        grid=(seq.n_batch,),
        in_specs=[
            pl.BlockSpec((L, FOURIER_WIDTH), lambda b: (b, 0)),
            _resident((L, L), lambda b: (0, 0)),
            _resident((L, L), lambda b: (0, 0)),
            _resident((FOURIER_WIDTH, FOURIER_WIDTH), lambda b: (0, 0)),
            _resident((FOURIER_WIDTH, FOURIER_WIDTH), lambda b: (0, 0)),
        ],
        out_specs=pl.BlockSpec((L, FOURIER_WIDTH), lambda b: (b, 0)),
        out_shape=jax.ShapeDtypeStruct((seq.n_rows, FOURIER_WIDTH), BF16),
        compiler_params=_params(1),
        name="fourier",
    )(f, cl, sl, cc, sc)


def _odd_out_kernel(seq, ch_ref, chp_ref, chn_ref, bg_ref, fo_ref, x_ref, gate_ref, cw_ref, cb_ref, wo_ref,
                    g_ref, b_ref, o_ref):
    halo = chp_ref.shape[0]

    def pre(s):
        has_prev, has_next = seq.sub_edges(s)
        rows = seq.sub_rows(s)
        lo, hi = seq.neighbours(s, ch_ref, chp_ref, chn_ref)
        ch = ch_ref[rows, :].astype(F32)
        prev_row = jnp.where(has_prev, lo.astype(F32)[halo - 1:halo, :], 0.0)
        next_row = jnp.where(has_next, hi.astype(F32)[0:1, :], 0.0)
        up, down = _shift_rows(ch, prev_row, next_row)
        conv = cb_ref[...] + up * cw_ref[0:1, :] + ch * cw_ref[1:2, :] + down * cw_ref[2:3, :]
        conv_out = bg_ref[rows, :].astype(F32) * conv
        return jnp.concatenate([conv_out.astype(BF16), fo_ref[rows, :]], axis=1)

    def mid(s, m):
        return _dot(m, wo_ref[...])

    def post(s, y):
        rows = seq.sub_rows(s)
        o_ref[rows, :] = _post_norm(x_ref[rows, :], gate_ref[...], y, g_ref[...], b_ref[...])

    _staggered(seq.n_sub, pre, mid, post)


def _odd_out(seq, layer, o, ch, bg, fo, x, ada5, conv_w, conv_b, w_out, ln_g, ln_b):
    prev_spec, next_spec = seq.halo_specs(CONV_WIDTH, HALO_BF16)
    return pl.pallas_call(
        functools.partial(_odd_out_kernel, seq),
        grid=(seq.n_tiles,),
        in_specs=[
            seq.tile_spec(CONV_WIDTH), prev_spec, next_spec,
            seq.tile_spec(CONV_WIDTH),
            seq.tile_spec(FOURIER_WIDTH),
            seq.tile_spec(D_MODEL),
            seq.mod_spec(layer, 2),
            pl.BlockSpec((None, 3, CONV_WIDTH), lambda i: (o, 0, 0)),
            _row_spec(o, CONV_WIDTH),
            _resident((None, D_MODEL, D_MODEL), lambda i: (layer, 0, 0)),
            _row_spec((layer, 0), D_MODEL, 2),
            _row_spec((layer, 0), D_MODEL, 2),
        ],
        out_specs=seq.tile_spec(D_MODEL),
        out_shape=jax.ShapeDtypeStruct((seq.n_rows, D_MODEL), F32),
        compiler_params=_params(1),
        name="odd_out",
    )(ch, ch, ch, bg, fo, x, ada5, conv_w, conv_b, w_out, ln_g, ln_b)


def _ffn_kernel(seq, x_ref, xp_ref, xn_ref, sc_ref, sh_ref, gate_ref, wa_ref, wg_ref, cw_ref, cb_ref, wd_ref,
                g_ref, b_ref, o_ref):
    halo = xp_ref.shape[0]
    n_sub = seq.n_sub

    def stage_up(s):
        sub = seq.sub_sizes[s]
        has_prev, has_next = seq.sub_edges(s)
        lo, hi = seq.neighbours(s, x_ref, xp_ref, xn_ref)
        x = x_ref[seq.sub_rows(s), :]
        u_ext = (_layer_norm(jnp.concatenate([lo, x, hi], axis=0)) * (1.0 + sc_ref[...]) + sh_ref[...]).astype(BF16)
        a_ext = _dot(u_ext, wa_ref[...])
        gate_lin = _dot(u_ext[halo:halo + sub], wg_ref[...])
        row = lax.broadcasted_iota(jnp.int32, (sub + 2 * halo, 1), 0)
        inside = ((row >= halo) | has_prev) & ((row < halo + sub) | has_next)
        return x, jnp.where(inside, a_ext, 0.0), gate_lin

    def stage_hidden(a_ext, gate_lin):
        n_ext = a_ext.shape[0]
        sub = n_ext - 2 * halo
        up = pltpu.roll(a_ext, 1, 0)[halo:halo + sub]
        mid = a_ext[halo:halo + sub]
        down = pltpu.roll(a_ext, n_ext - 1, 0)[halo:halo + sub]
        conv = cb_ref[...] + up * cw_ref[0:1, :] + mid * cw_ref[1:2, :] + down * cw_ref[2:3, :]
        return (conv * jax.nn.sigmoid(conv) * gate_lin).astype(BF16)

    def stage_out(s, x, y):
        o_ref[seq.sub_rows(s), :] = _post_norm(x, gate_ref[...], y, g_ref[...], b_ref[...])

    ups = {0: stage_up(0)}
    ys = {}
    for s in range(n_sub):
        if s + 1 < n_sub:
            ups[s + 1] = stage_up(s + 1)
        x, a_ext, gate_lin = ups.pop(s)
        h = stage_hidden(a_ext, gate_lin)
        if s > 0:
            stage_out(s - 1, *ys.pop(s - 1))
        ys[s] = (x, _dot(h, wd_ref[...]))
    stage_out(n_sub - 1, *ys.pop(n_sub - 1))


def _ffn(seq, layer, x, ada5, w_up, ffn_conv_w, ffn_conv_b, w_down, ln_g, ln_b):
    prev_spec, next_spec = seq.halo_specs(D_MODEL, HALO_F32)
    return pl.pallas_call(
        functools.partial(_ffn_kernel, seq),
        grid=(seq.n_tiles,),
        in_specs=[
            seq.tile_spec(D_MODEL), prev_spec, next_spec,
            seq.mod_spec(layer, 4),
            seq.mod_spec(layer, 3),
            seq.mod_spec(layer, 5),
            _resident((None, D_MODEL, D_FF), lambda i: (layer, 0, 0)),
            _resident((None, D_MODEL, D_FF), lambda i: (layer, 0, 1)),
            pl.BlockSpec((None, 3, D_FF), lambda i: (layer, 0, 0)),
            _row_spec(layer, D_FF),
            _resident((None, D_FF, D_MODEL), lambda i: (layer, 0, 0)),
            _row_spec((layer, 1), D_MODEL, 2),
            _row_spec((layer, 1), D_MODEL, 2),
        ],
        out_specs=seq.tile_spec(D_MODEL),
        out_shape=jax.ShapeDtypeStruct((seq.n_rows, D_MODEL), F32),
        compiler_params=_params(1),
        name="conv_ffn",
    )(x, x, x, ada5, ada5, ada5, w_up, w_up, ffn_conv_w, ffn_conv_b, w_down, ln_g, ln_b)


def _rope_tables(seq_len):
    t = lax.iota(jnp.int32, seq_len)
    row = (t // GRID_W).astype(F32)
    col = (t % GRID_W).astype(F32)
    n_freq = HEAD_DIM // 4
    inv = 1.0 / (ROPE_THETA ** (jnp.arange(n_freq, dtype=F32) / n_freq))
    ang = jnp.concatenate([row[:, None] * inv, col[:, None] * inv], -1)
    cos = jnp.repeat(jnp.cos(ang), 2, axis=1)
    sin = jnp.repeat(jnp.sin(ang), 2, axis=1) * jnp.tile(jnp.array([-1.0, 1.0], F32), HEAD_DIM // 2)
    reps = LANES // HEAD_DIM
    return jnp.tile(cos, (1, reps)), jnp.tile(sin, (1, reps))


def _fourier_tables(seq_len):
    cl, sl = _dft_tables(seq_len)
    ang = _dft_angles(lax.iota(jnp.int32, FOURIER_GROUP), FOURIER_GROUP)
    eye = jnp.eye(N_FOURIER_GROUPS, dtype=F32)
    return cl, sl, jnp.kron(eye, jnp.cos(ang)).astype(BF16), jnp.kron(eye, jnp.sin(ang)).astype(BF16)


def _run_trunk(seq, x, ada5, ctx_k, ctx_v, wts):
    (w_in_even, gq, gk, bd, w_pool, pool_scale, w_in_odd, conv_w, conv_b, w_out, w_up, ffn_conv_w,
     ffn_conv_b, w_down, ln_g, ln_b) = wts
    rope_tabs = _rope_tables(seq.seq_len) if seq.latent else None
    four_tabs = _fourier_tables(seq.seq_len)
    new_k, new_v = [], []
    for layer in range(DEPTH):
        if layer % 2 == 0:
            e = layer // 2
            outs = _even_in(seq, layer, e, x, ada5, w_in_even, gq, gk, bd, rope_tabs)
            q, k, vt, p = outs[:4]
            if not seq.latent:
                new_k.append(outs[4])
                new_v.append(outs[5])
            attn = _attention(seq, q, k, vt, (e, ctx_k, ctx_v) if seq.latent else None, gq[e])
            x = _even_out(seq, layer, e, attn, p, x, ada5, w_pool, pool_scale, w_out, ln_g, ln_b)
        else:
            o = layer // 2
            ch, bg, f = _odd_in(seq, layer, o, x, ada5, w_in_odd)
            fo = _fourier(seq, f, four_tabs)
            x = _odd_out(seq, layer, o, ch, bg, fo, x, ada5, conv_w, conv_b, w_out, ln_g, ln_b)
        x = _ffn(seq.with_tile(FFN_TILE_M), layer, x, ada5, w_up, ffn_conv_w, ffn_conv_b, w_down,
                 ln_g, ln_b)
    return x, new_k, new_v


def kernel(x_prompt, x_sample, cache_k, cache_v, c, c_ctx, w_ada, b_ada, w_in_even, q_norm_g, k_norm_g,
           w_pool, pool_scale, w_in_odd, conv_w, conv_b, w_out, w_up, ffn_conv_w, ffn_conv_b, w_down,
           ln_g, ln_b):
    n_prompt, prompt_len, _ = x_prompt.shape
    n_sample, sample_len, _ = x_sample.shape
    n_even = w_in_even.shape[0]
    n_odd = w_in_odd.shape[0]
    assert n_sample <= CTX_ROW

    cond = jnp.zeros((COND_ROWS, D_MODEL), F32).at[:n_sample].set(c).at[CTX_ROW].set(c_ctx)
    ada = _ada_all(cond, w_ada, b_ada)
    ada5 = ada.reshape(DEPTH, COND_ROWS, 6, 1, D_MODEL)

    head_of = lax.iota(jnp.int32, ATTN_WIDTH) // HEAD_DIM
    bd = (head_of[:, None] == head_of[None, :]).astype(BF16)
    wts = (
        w_in_even.astype(BF16),
        jnp.tile(q_norm_g, (1, N_HEADS)).reshape(n_even, 1, ATTN_WIDTH),
        jnp.tile(k_norm_g, (1, N_KV_HEADS)).reshape(n_even, 1, KV_WIDTH),
        bd,
        w_pool.astype(BF16),
        pool_scale.reshape(n_even, 1, POOL_WIDTH),
        w_in_odd.astype(BF16),
        conv_w,
        conv_b.reshape(n_odd, 1, CONV_WIDTH),
        w_out.astype(BF16),
        w_up.astype(BF16),
        ffn_conv_w,
        ffn_conv_b.reshape(DEPTH, 1, D_FF),
        w_down.astype(BF16),
        ln_g.reshape(DEPTH, 2, 1, D_MODEL),
        ln_b.reshape(DEPTH, 2, 1, D_MODEL),
    )

    prompt = _Seq(n_prompt, prompt_len, latent=False)
    y_prompt, ks, vs = _run_trunk(prompt, x_prompt.reshape(-1, D_MODEL), ada5, None, None, wts)
    cache_shape = (n_prompt, prompt_len, N_KV_HEADS, HEAD_DIM)
    new_cache_k = jnp.stack([k.reshape(cache_shape) for k in ks], 1)
    new_cache_v = jnp.stack([v.reshape(cache_shape) for v in vs], 1)

    sample = _Seq(n_sample, sample_len, latent=True)
    past_len = cache_k.shape[2]
    ctx_k = cache_k.astype(BF16).transpose(0, 1, 3, 2, 4)
    ones = jnp.ones((n_sample, n_even, N_KV_HEADS, ATTN_ONES_ROWS, past_len), BF16)
    ctx_v = jnp.concatenate([cache_v.astype(BF16).transpose(0, 1, 3, 4, 2), ones], axis=3)
    y_sample, _, _ = _run_trunk(sample, x_sample.reshape(-1, D_MODEL), ada5, ctx_k, ctx_v, wts)

    return (y_prompt.reshape(x_prompt.shape), y_sample.reshape(x_sample.shape), new_cache_k, new_cache_v)
```

```python
import functools
import math

import jax
import jax.numpy as jnp
from jax import lax
from jax.experimental import pallas as pl
from jax.experimental.pallas import tpu as pltpu

D_MODEL = 1024
DEPTH = 4
GRID_W = 64
N_HEADS = 8
N_KV_HEADS = 2
HEAD_DIM = 64
ATTN_WIDTH = N_HEADS * HEAD_DIM
KV_WIDTH = N_KV_HEADS * HEAD_DIM
POOL_WIDTH = D_MODEL - ATTN_WIDTH
POOL_WINDOWS = (2, 4, 8, 16)
POOL_GROUP = POOL_WIDTH // len(POOL_WINDOWS)
FOURIER_WIDTH = D_MODEL // 4
N_FOURIER_GROUPS = 4
FOURIER_GROUP = FOURIER_WIDTH // N_FOURIER_GROUPS
CONV_WIDTH = D_MODEL - FOURIER_WIDTH
D_FF = 2816
ROPE_THETA = 10000.0
LN_EPS = 1e-6
RMS_EPS = 1e-6
IN_EVEN = ATTN_WIDTH + 2 * KV_WIDTH + POOL_WIDTH
IN_ODD = 3 * CONV_WIDTH + FOURIER_WIDTH
DEEPNORM_ALPHA = (2 * DEPTH) ** 0.25
Q_SCALE = HEAD_DIM ** -0.5 * math.log2(math.e)

SUBLANES = 8
LANES = 128
VMEM_LIMIT_BYTES = 56 * 1024 * 1024

TILE_M = 1024
FFN_TILE_M = 1024
SUB_M = 256
ATTN_TILE_M = 512
ATTN_SUB_M = 256
ATTN_ONES_ROWS = 16
ATTN_HEADS_PER_DOT = 2
ATTN_KEY_CHUNK = 256
ATTN_SCORE_BOUND = 80.0
HALO_F32 = SUBLANES
HALO_BF16 = 2 * SUBLANES
DFT_HI_ROWS_PER_STEP = 4
COND_ROWS = 16
CTX_ROW = 8

F32 = jnp.float32
BF16 = jnp.bfloat16


def _params(n_axes):
    return pltpu.CompilerParams(dimension_semantics=("arbitrary",) * n_axes,
                                vmem_limit_bytes=VMEM_LIMIT_BYTES)


def _resident(block_shape, index_map):
    return pl.BlockSpec(block_shape, index_map, pipeline_mode=pl.Buffered(1))


def _dot(a, b):
    return jnp.dot(a, b, preferred_element_type=F32)


def _layer_norm(x):
    mu = jnp.mean(x, axis=-1, keepdims=True)
    xc = x - mu
    var = jnp.mean(xc * xc, axis=-1, keepdims=True)
    return xc * lax.rsqrt(var + LN_EPS)


def _staggered(n, pre, mid, post):
    state = pre(0)
    done = None
    for s in range(n):
        cur = mid(s, state)
        if s + 1 < n:
            state = pre(s + 1)
        if done is not None:
            post(s - 1, done)
        done = cur
    post(n - 1, done)


def _ada_kernel(cond_ref, w_ref, b_ref, o_ref):
    cnd = cond_ref[...]
    act = (cnd * jax.nn.sigmoid(cnd)).astype(BF16)
    o_ref[...] = _dot(act, w_ref[...].astype(BF16)) + b_ref[...]


def _ada_all(cond, w_ada, b_ada):
    tn = 3072
    n_out = 6 * D_MODEL
    return pl.pallas_call(
        _ada_kernel,
        grid=(DEPTH, n_out // tn),
        in_specs=[
            pl.BlockSpec((COND_ROWS, D_MODEL), lambda l, j: (0, 0)),
            pl.BlockSpec((None, D_MODEL, tn), lambda l, j: (l, 0, j)),
            pl.BlockSpec((None, 1, tn), lambda l, j: (l, 0, j)),
        ],
        out_specs=pl.BlockSpec((None, COND_ROWS, tn), lambda l, j: (l, 0, j)),
        out_shape=jax.ShapeDtypeStruct((DEPTH, COND_ROWS, n_out), F32),
        compiler_params=_params(2),
        name="ada",
    )(cond, w_ada, b_ada.reshape(DEPTH, 1, n_out))


def _sub_sizes(tile_m, seq_len):
    unit = min(SUB_M, seq_len)
    assert tile_m % unit == 0 and seq_len % unit == 0
    return (unit,) * (tile_m // unit)


class _Seq:
    def __init__(self, n_batch, seq_len, latent, tile_m=TILE_M):
        self.n_rows = n_batch * seq_len
        assert self.n_rows % tile_m == 0
        assert seq_len % tile_m == 0 or not latent
        self.n_batch = n_batch
        self.seq_len = seq_len
        self.latent = latent
        self.tile_m = tile_m
        self.sub_sizes = _sub_sizes(tile_m, seq_len)
        self.sub_starts = tuple(sum(self.sub_sizes[:s]) for s in range(len(self.sub_sizes)))
        self.n_sub = len(self.sub_sizes)
        self.n_tiles = self.n_rows // tile_m

    def with_tile(self, tile_m):
        return _Seq(self.n_batch, self.seq_len, self.latent, tile_m)

    def cond_row(self, i):
        return (i * self.tile_m) // self.seq_len if self.latent else CTX_ROW

    def tile_spec(self, width):
        return pl.BlockSpec((self.tile_m, width), lambda i: (i, 0))

    def halo_specs(self, width, halo):
        per_tile = self.tile_m // halo
        last = self.n_rows // halo - 1
        prev = pl.BlockSpec((halo, width), lambda i: (jnp.maximum(i * per_tile - 1, 0), 0))
        nxt = pl.BlockSpec((halo, width), lambda i: (jnp.minimum((i + 1) * per_tile, last), 0))
        return prev, nxt

    def mod_spec(self, layer, which):
        return pl.BlockSpec((None, None, None, 1, D_MODEL),
                            lambda i: (layer, self.cond_row(i), which, 0, 0))

    def sub_pos(self, s):
        if self.sub_sizes[s] == self.seq_len:
            return 0
        return (pl.program_id(0) * self.tile_m + self.sub_starts[s]) % self.seq_len

    def sub_edges(self, s):
        if self.sub_sizes[s] == self.seq_len:
            return False, False
        pos = self.sub_pos(s)
        return pos > 0, pos + self.sub_sizes[s] < self.seq_len

    def sub_rows(self, s):
        return slice(self.sub_starts[s], self.sub_starts[s] + self.sub_sizes[s])

    def neighbours(self, s, ref, prev_ref, next_ref):
        halo = prev_ref.shape[0]
        start, stop = self.sub_starts[s], self.sub_starts[s] + self.sub_sizes[s]
        lo = ref[start - halo:start, :] if s > 0 else prev_ref[...]
        hi = ref[stop:stop + halo, :] if s < self.n_sub - 1 else next_ref[...]
        return lo, hi


def _row_spec(layer, width, n_lead=1):
    if n_lead == 1:
        return pl.BlockSpec((None, 1, width), lambda i: (layer, 0, 0))
    return pl.BlockSpec((None, None, 1, width), lambda i: (layer[0], layer[1], 0, 0))


def _post_norm(x, gate, y, g, b):
    return _layer_norm(DEEPNORM_ALPHA * x + gate * y) * g + b


def _even_in_kernel(seq, x_ref, sc_ref, sh_ref, w_ref, gq_ref, gk_ref, bd_ref, *rest):
    if seq.latent:
        cos_ref, sin_ref, q_out, k_out, vt_out, p_out = rest
    else:
        q_out, k_out, vt_out, p_out, kraw_out, vraw_out = rest
    o1 = ATTN_WIDTH
    o2 = o1 + KV_WIDTH
    o3 = o2 + KV_WIDTH

    def rope(t, cos, sin):
        even_lane = (lax.broadcasted_iota(jnp.int32, cos.shape, 1) & 1) == 0
        outs = []
        for j in range(t.shape[1] // LANES):
            slab = t[:, j * LANES:(j + 1) * LANES]
            partner = jnp.where(even_lane, pltpu.roll(slab, LANES - 1, 1), pltpu.roll(slab, 1, 1))
            outs.append(slab * cos + partner * sin)
        return outs[0] if len(outs) == 1 else jnp.concatenate(outs, axis=1)

    def pre(s):
        return (_layer_norm(x_ref[seq.sub_rows(s), :]) * (1.0 + sc_ref[...]) + sh_ref[...]).astype(BF16)

    def mid(s, u):
        return _dot(u, w_ref[...])

    def post(s, proj):
        rows = seq.sub_rows(s)
        q = proj[:, :o1]
        k = proj[:, o1:o2]
        v = proj[:, o2:o3]
        p_out[rows, :] = proj[:, o3:].astype(BF16)
        bd = bd_ref[...]
        q = q * lax.rsqrt(_dot((q * q).astype(BF16), bd) * (1.0 / HEAD_DIM) + RMS_EPS) * gq_ref[...]
        k = k * lax.rsqrt(_dot((k * k).astype(BF16), bd[:KV_WIDTH, :KV_WIDTH]) * (1.0 / HEAD_DIM)
                          + RMS_EPS) * gk_ref[...]
        if seq.latent:
            cos = cos_ref[rows, :]
            sin = sin_ref[rows, :]
            q = rope(q, cos, sin)
            k = rope(k, cos, sin)
        else:
            kraw_out[rows, :] = k
            vraw_out[rows, :] = v
        q_out[rows, :] = (q * Q_SCALE).astype(BF16)
        vt = v.T.astype(BF16)
        for g in range(N_KV_HEADS):
            heads = slice(g * HEAD_DIM, (g + 1) * HEAD_DIM)
            k_out[g, rows, :] = k[:, heads].astype(BF16)
            vt_out[g, :HEAD_DIM, rows] = vt[heads, :]
            vt_out[g, HEAD_DIM:, rows] = jnp.ones((ATTN_ONES_ROWS, seq.sub_sizes[s]), BF16)

    _staggered(seq.n_sub, pre, mid, post)


def _even_in(seq, layer, e, x, ada5, w_in, gq, gk, bd, rope_tabs):
    n = seq.n_rows
    in_specs = [
        seq.tile_spec(D_MODEL),
        seq.mod_spec(layer, 1),
        seq.mod_spec(layer, 0),
        _resident((None, D_MODEL, IN_EVEN), lambda i: (e, 0, 0)),
        _row_spec(e, ATTN_WIDTH),
        _row_spec(e, KV_WIDTH),
        _resident((ATTN_WIDTH, ATTN_WIDTH), lambda i: (0, 0)),
    ]
    args = [x, ada5, ada5, w_in, gq, gk, bd]
    vt_rows = HEAD_DIM + ATTN_ONES_ROWS
    out_specs = [seq.tile_spec(ATTN_WIDTH),
                 pl.BlockSpec((N_KV_HEADS, seq.tile_m, HEAD_DIM), lambda i: (0, i, 0)),
                 pl.BlockSpec((N_KV_HEADS, vt_rows, seq.tile_m), lambda i: (0, 0, i)),
                 seq.tile_spec(POOL_WIDTH)]
    out_shape = [jax.ShapeDtypeStruct((n, ATTN_WIDTH), BF16),
                 jax.ShapeDtypeStruct((N_KV_HEADS, n, HEAD_DIM), BF16),
                 jax.ShapeDtypeStruct((N_KV_HEADS, vt_rows, n), BF16),
                 jax.ShapeDtypeStruct((n, POOL_WIDTH), BF16)]
    if seq.latent:
        tiles_per_seq = seq.seq_len // seq.tile_m
        in_specs += [pl.BlockSpec((seq.tile_m, LANES), lambda i: (i % tiles_per_seq, 0))] * 2
        args += list(rope_tabs)
    else:
        out_specs += [seq.tile_spec(KV_WIDTH)] * 2
        out_shape += [jax.ShapeDtypeStruct((n, KV_WIDTH), F32)] * 2
    return pl.pallas_call(
        functools.partial(_even_in_kernel, seq),
        grid=(seq.n_tiles,),
        in_specs=in_specs,
        out_specs=out_specs,
        out_shape=out_shape,
        compiler_params=_params(1),
        name="even_in",
    )(*args)


def _reduce_rows(x, op, final, chunk=256):
    n = x.shape[0]
    if n > chunk and n % chunk == 0:
        parts = [x[i:i + chunk] for i in range(0, n, chunk)]
        while len(parts) > 1:
            parts = [op(parts[i], parts[i + 1]) if i + 1 < len(parts) else parts[i]
                     for i in range(0, len(parts), 2)]
        x = parts[0]
        n = chunk
    while n > SUBLANES and n % (2 * SUBLANES) == 0:
        n //= 2
        x = op(x[:n], x[n:])
    return final(x, axis=0, keepdims=True)


def _attn_kernel(has_ctx, q_ref, k_ref, vt_ref, *rest):
    if has_ctx:
        kc_ref, vtc_ref, qmax_ref, o_ref, st_a, st_b, p_a, p_b, kmax_ref = rest
    else:
        qmax_ref, o_ref, st_a, st_b, p_a, p_b, kmax_ref = rest
        kc_ref = vtc_ref = None
    group = N_HEADS // N_KV_HEADS
    heads_per_dot = ATTN_HEADS_PER_DOT
    tq = ATTN_SUB_M
    stages = [(r, h0) for r in range(q_ref.shape[0] // tq) for h0 in range(0, N_HEADS, heads_per_dot)]
    st_bufs = (st_a, st_b)
    p_bufs = (p_a, p_b)
    n_new = k_ref.shape[1]
    n_keys = st_a.shape[0]
    chunk = ATTN_KEY_CHUNK
    key_chunks = [(k_ref, c, c) for c in range(0, n_new, chunk)]
    if has_ctx:
        key_chunks += [(kc_ref, c, n_new + c) for c in range(0, n_keys - n_new, chunk)]

    @pl.when(pl.program_id(1) == 0)
    def _():
        kmax = None
        for ref in (k_ref, kc_ref) if has_ctx else (k_ref,):
            for g in range(N_KV_HEADS):
                kk = ref[g].astype(F32)
                part = jnp.max(jnp.sum(kk * kk, axis=-1, keepdims=True))
                kmax = part if kmax is None else jnp.maximum(kmax, part)
        kmax_ref[0] = kmax

    bounded = qmax_ref[0] * kmax_ref[0] <= ATTN_SCORE_BOUND ** 2

    def score_chunks(i):
        r, h0 = stages[i]
        qs = jnp.concatenate([q_ref[r * tq:(r + 1) * tq, h * HEAD_DIM:(h + 1) * HEAD_DIM]
                              for h in range(h0, h0 + heads_per_dot)], axis=0)
        for ref, src, dst in key_chunks:
            yield dst, lax.dot_general(ref[h0 // group, src:src + chunk, :], qs, (((1,), (1,)), ((), ())),
                                       preferred_element_type=F32)

    def scores_bounded(i):
        for dst, st in score_chunks(i):
            p_bufs[i % 2][dst:dst + chunk, :] = jnp.exp2(st).astype(BF16)

    def scores_general(i):
        for dst, st in score_chunks(i):
            st_bufs[i % 2][dst:dst + chunk, :] = st

    def softmax_general(i):
        st_ref, p_ref = st_bufs[i % 2], p_bufs[i % 2]
        m = None
        for c in range(0, n_keys, chunk):
            part = _reduce_rows(st_ref[c:c + chunk, :], jnp.maximum, jnp.max)
            m = part if m is None else jnp.maximum(m, part)
        for c in range(0, n_keys, chunk):
            p_ref[c:c + chunk, :] = jnp.exp2(st_ref[c:c + chunk, :] - m).astype(BF16)

    def run(scores, softmax):
        outs = []
        scores(0)
        for i, (r, h0) in enumerate(stages):
            if i + 1 < len(stages):
                scores(i + 1)
            softmax(i)
            g = h0 // group
            ot = _dot(vt_ref[g], p_bufs[i % 2][:n_new, :])
            if has_ctx:
                ot = ot + _dot(vtc_ref[g], p_bufs[i % 2][n_new:, :])
            ot = ot[:HEAD_DIM] / ot[HEAD_DIM:HEAD_DIM + 1]
            outs += [ot[:, j * tq:(j + 1) * tq] for j in range(heads_per_dot)]
            if h0 + heads_per_dot == N_HEADS:
                o_ref[r * tq:(r + 1) * tq, :] = jnp.concatenate(outs, axis=0).T.astype(BF16)
                outs = []

    @pl.when(bounded)
    def _():
        run(scores_bounded, lambda i: None)

    @pl.when(jnp.logical_not(bounded))
    def _():
        run(scores_general, softmax_general)


def _attention(seq, q, k, vt, ctx, q_gain):
    qmax = (jnp.max(q_gain * q_gain) * (HEAD_DIM * Q_SCALE ** 2 * 1.02)).reshape(1)
    L = seq.seq_len
    vt_rows = HEAD_DIM + ATTN_ONES_ROWS
    tq = ATTN_TILE_M if L % ATTN_TILE_M == 0 else ATTN_SUB_M
    tps = L // tq
    cols = ATTN_HEADS_PER_DOT * ATTN_SUB_M
    in_specs = [
        pl.BlockSpec((tq, ATTN_WIDTH), lambda b, j: (b * tps + j, 0)),
        pl.BlockSpec((N_KV_HEADS, L, HEAD_DIM), lambda b, j: (0, b, 0)),
        pl.BlockSpec((N_KV_HEADS, vt_rows, L), lambda b, j: (0, 0, b)),
    ]
    args = [q, k, vt]
    n_keys = L
    if ctx is not None:
        e, k_ctx, vt_ctx = ctx
        past = k_ctx.shape[3]
        n_keys += past
        in_specs += [
            pl.BlockSpec((None, None, N_KV_HEADS, past, HEAD_DIM), lambda b, j: (b, e, 0, 0, 0)),
            pl.BlockSpec((None, None, N_KV_HEADS, vt_rows, past), lambda b, j: (b, e, 0, 0, 0)),
        ]
        args += [k_ctx, vt_ctx]
    assert L % ATTN_KEY_CHUNK == 0 and n_keys % ATTN_KEY_CHUNK == 0
    return pl.pallas_call(
        functools.partial(_attn_kernel, ctx is not None),
        grid=(seq.n_batch, tps),
        in_specs=in_specs + [pl.BlockSpec(memory_space=pltpu.SMEM)],
        out_specs=pl.BlockSpec((tq, ATTN_WIDTH), lambda b, j: (b * tps + j, 0)),
        out_shape=jax.ShapeDtypeStruct((seq.n_rows, ATTN_WIDTH), BF16),
        scratch_shapes=[pltpu.VMEM((n_keys, cols), F32), pltpu.VMEM((n_keys, cols), F32),
                        pltpu.VMEM((n_keys, cols), BF16), pltpu.VMEM((n_keys, cols), BF16),
                        pltpu.SMEM((1,), F32)],
        compiler_params=_params(2),
        name="attention",
    )(*args, qmax)


def _even_out_kernel(seq, attn_ref, p_ref, pp_ref, pn_ref, x_ref, gate_ref, wp_ref, ps_ref, wo_ref,
                     g_ref, b_ref, o_ref):
    halo = pp_ref.shape[0]

    def pre(s):
        sub = seq.sub_sizes[s]
        n_ext = sub + 2 * halo
        has_prev, has_next = seq.sub_edges(s)
        lo, hi = seq.neighbours(s, p_ref, pp_ref, pn_ref)
        p = p_ref[seq.sub_rows(s), :].astype(F32)
        ext = jnp.concatenate([jnp.where(has_prev, lo.astype(F32), 0.0), p,
                               jnp.where(has_next, hi.astype(F32), 0.0)], axis=0)
        pos = seq.sub_pos(s) + lax.broadcasted_iota(jnp.int32, (sub, 1), 0)
        mixed = []
        for gi, w in enumerate(POOL_WINDOWS):
            half = w // 2
            lanes = slice(gi * POOL_GROUP, (gi + 1) * POOL_GROUP)
            run = ext[:, lanes]
            span = 1
            while span < w:
                run = run + pltpu.roll(run, span, 0)
                span *= 2
            if half > 1:
                run = pltpu.roll(run, n_ext - (half - 1), 0)
            total = run[halo:halo + sub]
            cnt = jnp.minimum(pos + half, seq.seq_len) - jnp.maximum(pos - half, 0)
            centred = total / cnt.astype(F32) - p[:, lanes]
            mixed.append(_dot(centred.astype(BF16), wp_ref[gi]))
        pool = jnp.concatenate(mixed, axis=1) * ps_ref[...]
        return jnp.concatenate([attn_ref[seq.sub_rows(s), :], pool.astype(BF16)], axis=1)

    def mid(s, m):
        return _dot(m, wo_ref[...])

    def post(s, y):
        rows = seq.sub_rows(s)
        o_ref[rows, :] = _post_norm(x_ref[rows, :], gate_ref[...], y, g_ref[...], b_ref[...])

    _staggered(seq.n_sub, pre, mid, post)


def _even_out(seq, layer, e, attn, p, x, ada5, w_pool, pool_scale, w_out, ln_g, ln_b):
    prev_spec, next_spec = seq.halo_specs(POOL_WIDTH, HALO_BF16)
    return pl.pallas_call(
        functools.partial(_even_out_kernel, seq),
        grid=(seq.n_tiles,),
        in_specs=[
            seq.tile_spec(ATTN_WIDTH),
            seq.tile_spec(POOL_WIDTH), prev_spec, next_spec,
            seq.tile_spec(D_MODEL),
            seq.mod_spec(layer, 2),
            _resident((None, len(POOL_WINDOWS), POOL_GROUP, POOL_GROUP), lambda i: (e, 0, 0, 0)),
            _row_spec(e, POOL_WIDTH),
            _resident((None, D_MODEL, D_MODEL), lambda i: (layer, 0, 0)),
            _row_spec((layer, 0), D_MODEL, 2),
            _row_spec((layer, 0), D_MODEL, 2),
        ],
        out_specs=seq.tile_spec(D_MODEL),
        out_shape=jax.ShapeDtypeStruct((seq.n_rows, D_MODEL), F32),
        compiler_params=_params(1),
        name="even_out",
    )(attn, p, p, p, x, ada5, w_pool, pool_scale, w_out, ln_g, ln_b)


def _odd_in_kernel(seq, x_ref, xp_ref, xn_ref, sc_ref, sh_ref, w_ref, cw_ref, cb_ref, conv_out, f_out):
    c = CONV_WIDTH
    halo = xp_ref.shape[0]

    def pre(s):
        lo, hi = seq.neighbours(s, x_ref, xp_ref, xn_ref)
        x_ext = jnp.concatenate([lo, x_ref[seq.sub_rows(s), :], hi], axis=0)
        return (_layer_norm(x_ext) * (1.0 + sc_ref[...]) + sh_ref[...]).astype(BF16)

    def mid(s, u_ext):
        return _dot(u_ext, w_ref[...])

    def post(s, proj):
        sub = seq.sub_sizes[s]
        n_ext = sub + 2 * halo
        rows = seq.sub_rows(s)
        inner = slice(halo, halo + sub)
        has_prev, has_next = seq.sub_edges(s)
        row = lax.broadcasted_iota(jnp.int32, (n_ext, 1), 0)
        inside = ((row >= halo) | has_prev) & ((row < halo + sub) | has_next)
        ch = jnp.where(inside, proj[:, 2 * c:3 * c] * proj[:, :c], 0.0)
        conv = (cb_ref[...] + pltpu.roll(ch, 1, 0)[inner] * cw_ref[0:1, :] + ch[inner] * cw_ref[1:2, :]
                + pltpu.roll(ch, n_ext - 1, 0)[inner] * cw_ref[2:3, :])
        conv_out[rows, :] = (proj[inner, c:2 * c] * conv).astype(BF16)
        f_out[rows, :] = proj[inner, 3 * c:].astype(BF16)

    _staggered(seq.n_sub, pre, mid, post)


def _odd_in(seq, layer, o, x, ada5, w_in, conv_w, conv_b):
    n = seq.n_rows
    prev_spec, next_spec = seq.halo_specs(D_MODEL, HALO_F32)
    return pl.pallas_call(
        functools.partial(_odd_in_kernel, seq),
        grid=(seq.n_tiles,),
        in_specs=[
            seq.tile_spec(D_MODEL), prev_spec, next_spec,
            seq.mod_spec(layer, 1),
            seq.mod_spec(layer, 0),
            _resident((None, D_MODEL, IN_ODD), lambda i: (o, 0, 0)),
            pl.BlockSpec((None, 3, CONV_WIDTH), lambda i: (o, 0, 0)),
            _row_spec(o, CONV_WIDTH),
        ],
        out_specs=[seq.tile_spec(CONV_WIDTH), seq.tile_spec(FOURIER_WIDTH)],
        out_shape=[jax.ShapeDtypeStruct((n, CONV_WIDTH), BF16), jax.ShapeDtypeStruct((n, FOURIER_WIDTH), BF16)],
        compiler_params=_params(1),
        name="odd_in",
    )(x, x, x, ada5, ada5, w_in, conv_w, conv_b)


def _fourier_kernel(scale, f_ref, cl_ref, sl_ref, cc_ref, sc_ref, o_ref):
    fb = f_ref[...]
    g_cos = (_dot(fb, cc_ref[...]) * scale).astype(BF16)
    g_sin = (_dot(fb, sc_ref[...]) * scale).astype(BF16)
    o_ref[...] = (_dot(cl_ref[...], g_cos) - _dot(sl_ref[...], g_sin)).astype(BF16)


def _dft_angles(rows, n):
    k = lax.iota(jnp.int32, n)[None, :]
    return ((rows[:, None] * k) % n).astype(F32) * (2.0 * math.pi / n)


def _dft_expand_kernel(hc_ref, hs_ref, lc_ref, ls_ref, c_out, s_out):
    lc, ls = lc_ref[...], ls_ref[...]
    split = lc.shape[0]
    for r in range(hc_ref.shape[0]):
        hc, hs = hc_ref[r], hs_ref[r]
        c_out[r * split:(r + 1) * split, :] = (hc * lc - hs * ls).astype(BF16)
        s_out[r * split:(r + 1) * split, :] = (hs * lc + hc * ls).astype(BF16)


def _dft_tables(n):
    split = FOURIER_GROUP
    assert n % split == 0
    ang_hi = _dft_angles(lax.iota(jnp.int32, n // split) * split, n).reshape(n // split, 1, n)
    ang_lo = _dft_angles(lax.iota(jnp.int32, split), n)
    per_step = min(DFT_HI_ROWS_PER_STEP, n // split)
    assert (n // split) % per_step == 0
    hi_spec = pl.BlockSpec((per_step, 1, n), lambda i: (i, 0, 0))
    lo_spec = pl.BlockSpec((split, n), lambda i: (0, 0))
    out_spec = pl.BlockSpec((per_step * split, n), lambda i: (i, 0))
    return pl.pallas_call(
        _dft_expand_kernel,
        grid=(n // split // per_step,),
        in_specs=[hi_spec, hi_spec, lo_spec, lo_spec],
        out_specs=[out_spec, out_spec],
        out_shape=[jax.ShapeDtypeStruct((n, n), BF16)] * 2,
        compiler_params=_params(1),
        name="dft_tables",
    )(jnp.cos(ang_hi), jnp.sin(ang_hi), jnp.cos(ang_lo), jnp.sin(ang_lo))


def _fourier(seq, f, tabs):
    L = seq.seq_len
    cl, sl, cc, sc = tabs
    scale = 1.0 / math.sqrt(L * FOURIER_GROUP)
    return pl.pallas_call(
        functools.partial(_fourier_kernel, scale),
        grid=(seq.n_batch,),
        in_specs=[
            pl.BlockSpec((L, FOURIER_WIDTH), lambda b: (b, 0)),
            _resident((L, L), lambda b: (0, 0)),
            _resident((L, L), lambda b: (0, 0)),
            _resident((FOURIER_WIDTH, FOURIER_WIDTH), lambda b: (0, 0)),
            _resident((FOURIER_WIDTH, FOURIER_WIDTH), lambda b: (0, 0)),
        ],
        out_specs=pl.BlockSpec((L, FOURIER_WIDTH), lambda b: (b, 0)),
        out_shape=jax.ShapeDtypeStruct((seq.n_rows, FOURIER_WIDTH), BF16),
        compiler_params=_params(1),
        name="fourier",
    )(f, cl, sl, cc, sc)


def _odd_out_kernel(seq, conv_ref, fo_ref, x_ref, gate_ref, wo_ref, g_ref, b_ref, o_ref):
    def pre(s):
        rows = seq.sub_rows(s)
        return jnp.concatenate([conv_ref[rows, :], fo_ref[rows, :]], axis=1)

    def mid(s, m):
        return _dot(m, wo_ref[...])

    def post(s, y):
        rows = seq.sub_rows(s)
        o_ref[rows, :] = _post_norm(x_ref[rows, :], gate_ref[...], y, g_ref[...], b_ref[...])

    _staggered(seq.n_sub, pre, mid, post)


def _odd_out(seq, layer, conv, fo, x, ada5, w_out, ln_g, ln_b):
    return pl.pallas_call(
        functools.partial(_odd_out_kernel, seq),
        grid=(seq.n_tiles,),
        in_specs=[
            seq.tile_spec(CONV_WIDTH),
            seq.tile_spec(FOURIER_WIDTH),
            seq.tile_spec(D_MODEL),
            seq.mod_spec(layer, 2),
            _resident((None, D_MODEL, D_MODEL), lambda i: (layer, 0, 0)),
            _row_spec((layer, 0), D_MODEL, 2),
            _row_spec((layer, 0), D_MODEL, 2),
        ],
        out_specs=seq.tile_spec(D_MODEL),
        out_shape=jax.ShapeDtypeStruct((seq.n_rows, D_MODEL), F32),
        compiler_params=_params(1),
        name="odd_out",
    )(conv, fo, x, ada5, w_out, ln_g, ln_b)


def _ffn_kernel(seq, x_ref, xp_ref, xn_ref, sc_ref, sh_ref, gate_ref, wa_ref, wg_ref, cw_ref, cb_ref, wd_ref,
                g_ref, b_ref, o_ref):
    halo = xp_ref.shape[0]
    n_sub = seq.n_sub

    def stage_up(s):
        sub = seq.sub_sizes[s]
        has_prev, has_next = seq.sub_edges(s)
        lo, hi = seq.neighbours(s, x_ref, xp_ref, xn_ref)
        x = x_ref[seq.sub_rows(s), :]
        u_ext = (_layer_norm(jnp.concatenate([lo, x, hi], axis=0)) * (1.0 + sc_ref[...]) + sh_ref[...]).astype(BF16)
        a_ext = _dot(u_ext, wa_ref[...])
        gate_lin = _dot(u_ext[halo:halo + sub], wg_ref[...])
        row = lax.broadcasted_iota(jnp.int32, (sub + 2 * halo, 1), 0)
        inside = ((row >= halo) | has_prev) & ((row < halo + sub) | has_next)
        return x, jnp.where(inside, a_ext, 0.0), gate_lin

    def stage_hidden(a_ext, gate_lin):
        n_ext = a_ext.shape[0]
        sub = n_ext - 2 * halo
        up = pltpu.roll(a_ext, 1, 0)[halo:halo + sub]
        mid = a_ext[halo:halo + sub]
        down = pltpu.roll(a_ext, n_ext - 1, 0)[halo:halo + sub]
        conv = cb_ref[...] + up * cw_ref[0:1, :] + mid * cw_ref[1:2, :] + down * cw_ref[2:3, :]
        return (conv * jax.nn.sigmoid(conv) * gate_lin).astype(BF16)

    def stage_out(s, x, y):
        o_ref[seq.sub_rows(s), :] = _post_norm(x, gate_ref[...], y, g_ref[...], b_ref[...])

    ups = {0: stage_up(0)}
    ys = {}
    for s in range(n_sub):
        if s + 1 < n_sub:
            ups[s + 1] = stage_up(s + 1)
        x, a_ext, gate_lin = ups.pop(s)
        h = stage_hidden(a_ext, gate_lin)
        if s > 0:
            stage_out(s - 1, *ys.pop(s - 1))
        ys[s] = (x, _dot(h, wd_ref[...]))
    stage_out(n_sub - 1, *ys.pop(n_sub - 1))


def _ffn(seq, layer, x, ada5, w_up, ffn_conv_w, ffn_conv_b, w_down, ln_g, ln_b):
    prev_spec, next_spec = seq.halo_specs(D_MODEL, HALO_F32)
    return pl.pallas_call(
        functools.partial(_ffn_kernel, seq),
        grid=(seq.n_tiles,),
        in_specs=[
            seq.tile_spec(D_MODEL), prev_spec, next_spec,
            seq.mod_spec(layer, 4),
            seq.mod_spec(layer, 3),
            seq.mod_spec(layer, 5),
            _resident((None, D_MODEL, D_FF), lambda i: (layer, 0, 0)),
            _resident((None, D_MODEL, D_FF), lambda i: (layer, 0, 1)),
            pl.BlockSpec((None, 3, D_FF), lambda i: (layer, 0, 0)),
            _row_spec(layer, D_FF),
            _resident((None, D_FF, D_MODEL), lambda i: (layer, 0, 0)),
            _row_spec((layer, 1), D_MODEL, 2),
            _row_spec((layer, 1), D_MODEL, 2),
        ],
        out_specs=seq.tile_spec(D_MODEL),
        out_shape=jax.ShapeDtypeStruct((seq.n_rows, D_MODEL), F32),
        compiler_params=_params(1),
        name="conv_ffn",
    )(x, x, x, ada5, ada5, ada5, w_up, w_up, ffn_conv_w, ffn_conv_b, w_down, ln_g, ln_b)


def _rope_tables(seq_len):
    t = lax.iota(jnp.int32, seq_len)
    row = (t // GRID_W).astype(F32)
    col = (t % GRID_W).astype(F32)
    n_freq = HEAD_DIM // 4
    inv = 1.0 / (ROPE_THETA ** (jnp.arange(n_freq, dtype=F32) / n_freq))
    ang = jnp.concatenate([row[:, None] * inv, col[:, None] * inv], -1)
    cos = jnp.repeat(jnp.cos(ang), 2, axis=1)
    sin = jnp.repeat(jnp.sin(ang), 2, axis=1) * jnp.tile(jnp.array([-1.0, 1.0], F32), HEAD_DIM // 2)
    reps = LANES // HEAD_DIM
    return jnp.tile(cos, (1, reps)), jnp.tile(sin, (1, reps))


def _fourier_tables(seq_len):
    cl, sl = _dft_tables(seq_len)
    ang = _dft_angles(lax.iota(jnp.int32, FOURIER_GROUP), FOURIER_GROUP)
    eye = jnp.eye(N_FOURIER_GROUPS, dtype=F32)
    return cl, sl, jnp.kron(eye, jnp.cos(ang)).astype(BF16), jnp.kron(eye, jnp.sin(ang)).astype(BF16)


def _run_trunk(seq, x, ada5, ctx_k, ctx_v, wts):
    (w_in_even, gq, gk, bd, w_pool, pool_scale, w_in_odd, conv_w, conv_b, w_out, w_up, ffn_conv_w,
     ffn_conv_b, w_down, ln_g, ln_b) = wts
    rope_tabs = _rope_tables(seq.seq_len) if seq.latent else None
    four_tabs = _fourier_tables(seq.seq_len)
    new_k, new_v = [], []
    for layer in range(DEPTH):
        if layer % 2 == 0:
            e = layer // 2
            outs = _even_in(seq, layer, e, x, ada5, w_in_even, gq, gk, bd, rope_tabs)
            q, k, vt, p = outs[:4]
            if not seq.latent:
                new_k.append(outs[4])
                new_v.append(outs[5])
            attn = _attention(seq, q, k, vt, (e, ctx_k, ctx_v) if seq.latent else None, gq[e])
            x = _even_out(seq, layer, e, attn, p, x, ada5, w_pool, pool_scale, w_out, ln_g, ln_b)
        else:
            o = layer // 2
            conv, f = _odd_in(seq, layer, o, x, ada5, w_in_odd, conv_w, conv_b)
            fo = _fourier(seq, f, four_tabs)
            x = _odd_out(seq, layer, conv, fo, x, ada5, w_out, ln_g, ln_b)
        x = _ffn(seq.with_tile(FFN_TILE_M), layer, x, ada5, w_up, ffn_conv_w, ffn_conv_b, w_down,
                 ln_g, ln_b)
    return x, new_k, new_v


def kernel(x_prompt, x_sample, cache_k, cache_v, c, c_ctx, w_ada, b_ada, w_in_even, q_norm_g, k_norm_g,
           w_pool, pool_scale, w_in_odd, conv_w, conv_b, w_out, w_up, ffn_conv_w, ffn_conv_b, w_down,
           ln_g, ln_b):
    n_prompt, prompt_len, _ = x_prompt.shape
    n_sample, sample_len, _ = x_sample.shape
    n_even = w_in_even.shape[0]
    n_odd = w_in_odd.shape[0]
    assert n_sample <= CTX_ROW

    cond = jnp.zeros((COND_ROWS, D_MODEL), F32).at[:n_sample].set(c).at[CTX_ROW].set(c_ctx)
    ada = _ada_all(cond, w_ada, b_ada)
    ada5 = ada.reshape(DEPTH, COND_ROWS, 6, 1, D_MODEL)

    head_of = lax.iota(jnp.int32, ATTN_WIDTH) // HEAD_DIM
    bd = (head_of[:, None] == head_of[None, :]).astype(BF16)
    wts = (
        w_in_even.astype(BF16),
        jnp.tile(q_norm_g, (1, N_HEADS)).reshape(n_even, 1, ATTN_WIDTH),
        jnp.tile(k_norm_g, (1, N_KV_HEADS)).reshape(n_even, 1, KV_WIDTH),
        bd,
        w_pool.astype(BF16),
        pool_scale.reshape(n_even, 1, POOL_WIDTH),
        w_in_odd.astype(BF16),
        conv_w,
        conv_b.reshape(n_odd, 1, CONV_WIDTH),
        w_out.astype(BF16),
        w_up.astype(BF16),
        ffn_conv_w,
        ffn_conv_b.reshape(DEPTH, 1, D_FF),
        w_down.astype(BF16),
        ln_g.reshape(DEPTH, 2, 1, D_MODEL),
        ln_b.reshape(DEPTH, 2, 1, D_MODEL),
    )

    prompt = _Seq(n_prompt, prompt_len, latent=False)
    y_prompt, ks, vs = _run_trunk(prompt, x_prompt.reshape(-1, D_MODEL), ada5, None, None, wts)
    cache_shape = (n_prompt, prompt_len, N_KV_HEADS, HEAD_DIM)
    new_cache_k = jnp.stack([k.reshape(cache_shape) for k in ks], 1)
    new_cache_v = jnp.stack([v.reshape(cache_shape) for v in vs], 1)

    sample = _Seq(n_sample, sample_len, latent=True)
    past_len = cache_k.shape[2]
    ctx_k = cache_k.astype(BF16).transpose(0, 1, 3, 2, 4)
    ones = jnp.ones((n_sample, n_even, N_KV_HEADS, ATTN_ONES_ROWS, past_len), BF16)
    ctx_v = jnp.concatenate([cache_v.astype(BF16).transpose(0, 1, 3, 4, 2), ones], axis=3)
    y_sample, _, _ = _run_trunk(sample, x_sample.reshape(-1, D_MODEL), ada5, ctx_k, ctx_v, wts)

    return (y_prompt.reshape(x_prompt.shape), y_sample.reshape(x_sample.shape), new_cache_k, new_cache_v)
```

```python
import functools
import math

import jax
import jax.numpy as jnp
from jax import lax
from jax.experimental import pallas as pl
from jax.experimental.pallas import tpu as pltpu

D_MODEL = 1024
DEPTH = 4
GRID_W = 64
N_HEADS = 8
N_KV_HEADS = 2
HEAD_DIM = 64
ATTN_WIDTH = N_HEADS * HEAD_DIM
KV_WIDTH = N_KV_HEADS * HEAD_DIM
POOL_WIDTH = D_MODEL - ATTN_WIDTH
POOL_WINDOWS = (2, 4, 8, 16)
POOL_GROUP = POOL_WIDTH // len(POOL_WINDOWS)
FOURIER_WIDTH = D_MODEL // 4
N_FOURIER_GROUPS = 4
FOURIER_GROUP = FOURIER_WIDTH // N_FOURIER_GROUPS
CONV_WIDTH = D_MODEL - FOURIER_WIDTH
D_FF = 2816
ROPE_THETA = 10000.0
LN_EPS = 1e-6
RMS_EPS = 1e-6
IN_EVEN = ATTN_WIDTH + 2 * KV_WIDTH + POOL_WIDTH
IN_ODD = 3 * CONV_WIDTH + FOURIER_WIDTH
DEEPNORM_ALPHA = (2 * DEPTH) ** 0.25
Q_SCALE = HEAD_DIM ** -0.5 * math.log2(math.e)

SUBLANES = 8
LANES = 128
VMEM_LIMIT_BYTES = 56 * 1024 * 1024

TILE_M = 1024
FFN_TILE_M = 1024
FUSED_TILE_M = 512
SUB_M = 256
ATTN_TILE_M = 512
ATTN_SUB_M = 256
ATTN_ONES_ROWS = 16
ATTN_HEADS_PER_DOT = 2
ATTN_KEY_CHUNK = 256
ATTN_SCORE_BOUND = 80.0
HALO_F32 = SUBLANES
HALO_BF16 = 2 * SUBLANES
DFT_HI_ROWS_PER_STEP = 4
COND_ROWS = 16
CTX_ROW = 8

F32 = jnp.float32
BF16 = jnp.bfloat16


def _params(n_axes):
    return pltpu.CompilerParams(dimension_semantics=("arbitrary",) * n_axes,
                                vmem_limit_bytes=VMEM_LIMIT_BYTES)


def _resident(block_shape, index_map):
    return pl.BlockSpec(block_shape, index_map, pipeline_mode=pl.Buffered(1))


def _dot(a, b):
    return jnp.dot(a, b, preferred_element_type=F32)


def _layer_norm(x):
    mu = jnp.mean(x, axis=-1, keepdims=True)
    xc = x - mu
    var = jnp.mean(xc * xc, axis=-1, keepdims=True)
    return xc * lax.rsqrt(var + LN_EPS)


def _staggered(n, pre, mid, post):
    state = pre(0)
    done = None
    for s in range(n):
        cur = mid(s, state)
        if s + 1 < n:
            state = pre(s + 1)
        if done is not None:
            post(s - 1, done)
        done = cur
    post(n - 1, done)


def _ada_kernel(cond_ref, w_ref, b_ref, o_ref):
    cnd = cond_ref[...]
    act = (cnd * jax.nn.sigmoid(cnd)).astype(BF16)
    o_ref[...] = _dot(act, w_ref[...].astype(BF16)) + b_ref[...]


def _ada_all(cond, w_ada, b_ada):
    tn = 3072
    n_out = 6 * D_MODEL
    return pl.pallas_call(
        _ada_kernel,
        grid=(DEPTH, n_out // tn),
        in_specs=[
            pl.BlockSpec((COND_ROWS, D_MODEL), lambda l, j: (0, 0)),
            pl.BlockSpec((None, D_MODEL, tn), lambda l, j: (l, 0, j)),
            pl.BlockSpec((None, 1, tn), lambda l, j: (l, 0, j)),
        ],
        out_specs=pl.BlockSpec((None, COND_ROWS, tn), lambda l, j: (l, 0, j)),
        out_shape=jax.ShapeDtypeStruct((DEPTH, COND_ROWS, n_out), F32),
        compiler_params=_params(2),
        name="ada",
    )(cond, w_ada, b_ada.reshape(DEPTH, 1, n_out))


def _sub_sizes(tile_m, seq_len):
    unit = min(SUB_M, seq_len)
    assert tile_m % unit == 0 and seq_len % unit == 0
    return (unit,) * (tile_m // unit)


class _Seq:
    def __init__(self, n_batch, seq_len, latent, tile_m=TILE_M):
        self.n_rows = n_batch * seq_len
        assert self.n_rows % tile_m == 0
        assert seq_len % tile_m == 0 or not latent
        self.n_batch = n_batch
        self.seq_len = seq_len
        self.latent = latent
        self.tile_m = tile_m
        self.sub_sizes = _sub_sizes(tile_m, seq_len)
        self.sub_starts = tuple(sum(self.sub_sizes[:s]) for s in range(len(self.sub_sizes)))
        self.n_sub = len(self.sub_sizes)
        self.n_tiles = self.n_rows // tile_m

    def with_tile(self, tile_m):
        return _Seq(self.n_batch, self.seq_len, self.latent, tile_m)

    def cond_row(self, i):
        return (i * self.tile_m) // self.seq_len if self.latent else CTX_ROW

    def tile_spec(self, width):
        return pl.BlockSpec((self.tile_m, width), lambda i: (i, 0))

    def halo_specs(self, width, halo):
        per_tile = self.tile_m // halo
        last = self.n_rows // halo - 1
        prev = pl.BlockSpec((halo, width), lambda i: (jnp.maximum(i * per_tile - 1, 0), 0))
        nxt = pl.BlockSpec((halo, width), lambda i: (jnp.minimum((i + 1) * per_tile, last), 0))
        return prev, nxt

    def mod_spec(self, layer, which):
        return pl.BlockSpec((None, None, None, 1, D_MODEL),
                            lambda i: (layer, self.cond_row(i), which, 0, 0))

    def sub_pos(self, s):
        if self.sub_sizes[s] == self.seq_len:
            return 0
        return (pl.program_id(0) * self.tile_m + self.sub_starts[s]) % self.seq_len

    def sub_edges(self, s):
        if self.sub_sizes[s] == self.seq_len:
            return False, False
        pos = self.sub_pos(s)
        return pos > 0, pos + self.sub_sizes[s] < self.seq_len

    def sub_rows(self, s):
        return slice(self.sub_starts[s], self.sub_starts[s] + self.sub_sizes[s])

    def neighbours(self, s, ref, prev_ref, next_ref):
        halo = prev_ref.shape[0]
        start, stop = self.sub_starts[s], self.sub_starts[s] + self.sub_sizes[s]
        lo = ref[start - halo:start, :] if s > 0 else prev_ref[...]
        hi = ref[stop:stop + halo, :] if s < self.n_sub - 1 else next_ref[...]
        return lo, hi


def _row_spec(layer, width, n_lead=1):
    if n_lead == 1:
        return pl.BlockSpec((None, 1, width), lambda i: (layer, 0, 0))
    return pl.BlockSpec((None, None, 1, width), lambda i: (layer[0], layer[1], 0, 0))


def _shift_rows(cur, prev_row, next_row):
    n = cur.shape[0]
    row = lax.broadcasted_iota(jnp.int32, (n, 1), 0)
    up = jnp.where(row == 0, prev_row, pltpu.roll(cur, 1, 0))
    down = jnp.where(row == n - 1, next_row, pltpu.roll(cur, n - 1, 0))
    return up, down


def _post_norm(x, gate, y, g, b):
    return _layer_norm(DEEPNORM_ALPHA * x + gate * y) * g + b


def _even_in_kernel(seq, x_ref, sc_ref, sh_ref, w_ref, gq_ref, gk_ref, bd_ref, *rest):
    if seq.latent:
        cos_ref, sin_ref, q_out, k_out, vt_out, p_out = rest
    else:
        q_out, k_out, vt_out, p_out, kraw_out, vraw_out = rest
    o1 = ATTN_WIDTH
    o2 = o1 + KV_WIDTH
    o3 = o2 + KV_WIDTH

    def rope(t, cos, sin):
        even_lane = (lax.broadcasted_iota(jnp.int32, cos.shape, 1) & 1) == 0
        outs = []
        for j in range(t.shape[1] // LANES):
            slab = t[:, j * LANES:(j + 1) * LANES]
            partner = jnp.where(even_lane, pltpu.roll(slab, LANES - 1, 1), pltpu.roll(slab, 1, 1))
            outs.append(slab * cos + partner * sin)
        return outs[0] if len(outs) == 1 else jnp.concatenate(outs, axis=1)

    def pre(s):
        return (_layer_norm(x_ref[seq.sub_rows(s), :]) * (1.0 + sc_ref[...]) + sh_ref[...]).astype(BF16)

    def mid(s, u):
        return _dot(u, w_ref[...])

    def post(s, proj):
        rows = seq.sub_rows(s)
        q = proj[:, :o1]
        k = proj[:, o1:o2]
        v = proj[:, o2:o3]
        p_out[rows, :] = proj[:, o3:].astype(BF16)
        bd = bd_ref[...]
        q = q * lax.rsqrt(_dot((q * q).astype(BF16), bd) * (1.0 / HEAD_DIM) + RMS_EPS) * gq_ref[...]
        k = k * lax.rsqrt(_dot((k * k).astype(BF16), bd[:KV_WIDTH, :KV_WIDTH]) * (1.0 / HEAD_DIM)
                          + RMS_EPS) * gk_ref[...]
        if seq.latent:
            cos = cos_ref[rows, :]
            sin = sin_ref[rows, :]
            q = rope(q, cos, sin)
            k = rope(k, cos, sin)
        else:
            kraw_out[rows, :] = k
            vraw_out[rows, :] = v
        q_out[rows, :] = (q * Q_SCALE).astype(BF16)
        vt = v.T.astype(BF16)
        for g in range(N_KV_HEADS):
            heads = slice(g * HEAD_DIM, (g + 1) * HEAD_DIM)
            k_out[g, rows, :] = k[:, heads].astype(BF16)
            vt_out[g, :HEAD_DIM, rows] = vt[heads, :]
            vt_out[g, HEAD_DIM:, rows] = jnp.ones((ATTN_ONES_ROWS, seq.sub_sizes[s]), BF16)

    _staggered(seq.n_sub, pre, mid, post)


def _even_in(seq, layer, e, x, ada5, w_in, gq, gk, bd, rope_tabs):
    n = seq.n_rows
    in_specs = [
        seq.tile_spec(D_MODEL),
        seq.mod_spec(layer, 1),
        seq.mod_spec(layer, 0),
        _resident((None, D_MODEL, IN_EVEN), lambda i: (e, 0, 0)),
        _row_spec(e, ATTN_WIDTH),
        _row_spec(e, KV_WIDTH),
        _resident((ATTN_WIDTH, ATTN_WIDTH), lambda i: (0, 0)),
    ]
    args = [x, ada5, ada5, w_in, gq, gk, bd]
    vt_rows = HEAD_DIM + ATTN_ONES_ROWS
    out_specs = [seq.tile_spec(ATTN_WIDTH),
                 pl.BlockSpec((N_KV_HEADS, seq.tile_m, HEAD_DIM), lambda i: (0, i, 0)),
                 pl.BlockSpec((N_KV_HEADS, vt_rows, seq.tile_m), lambda i: (0, 0, i)),
                 seq.tile_spec(POOL_WIDTH)]
    out_shape = [jax.ShapeDtypeStruct((n, ATTN_WIDTH), BF16),
                 jax.ShapeDtypeStruct((N_KV_HEADS, n, HEAD_DIM), BF16),
                 jax.ShapeDtypeStruct((N_KV_HEADS, vt_rows, n), BF16),
                 jax.ShapeDtypeStruct((n, POOL_WIDTH), BF16)]
    if seq.latent:
        tiles_per_seq = seq.seq_len // seq.tile_m
        in_specs += [pl.BlockSpec((seq.tile_m, LANES), lambda i: (i % tiles_per_seq, 0))] * 2
        args += list(rope_tabs)
    else:
        out_specs += [seq.tile_spec(KV_WIDTH)] * 2
        out_shape += [jax.ShapeDtypeStruct((n, KV_WIDTH), F32)] * 2
    return pl.pallas_call(
        functools.partial(_even_in_kernel, seq),
        grid=(seq.n_tiles,),
        in_specs=in_specs,
        out_specs=out_specs,
        out_shape=out_shape,
        compiler_params=_params(1),
        name="even_in",
    )(*args)


def _reduce_rows(x, op, final, chunk=256):
    n = x.shape[0]
    if n > chunk and n % chunk == 0:
        parts = [x[i:i + chunk] for i in range(0, n, chunk)]
        while len(parts) > 1:
            parts = [op(parts[i], parts[i + 1]) if i + 1 < len(parts) else parts[i]
                     for i in range(0, len(parts), 2)]
        x = parts[0]
        n = chunk
    while n > SUBLANES and n % (2 * SUBLANES) == 0:
        n //= 2
        x = op(x[:n], x[n:])
    return final(x, axis=0, keepdims=True)


def _attn_kernel(has_ctx, q_ref, k_ref, vt_ref, *rest):
    if has_ctx:
        kc_ref, vtc_ref, qmax_ref, o_ref, st_a, st_b, p_a, p_b, kmax_ref = rest
    else:
        qmax_ref, o_ref, st_a, st_b, p_a, p_b, kmax_ref = rest
        kc_ref = vtc_ref = None
    group = N_HEADS // N_KV_HEADS
    heads_per_dot = ATTN_HEADS_PER_DOT
    tq = ATTN_SUB_M
    stages = [(r, h0) for r in range(q_ref.shape[0] // tq) for h0 in range(0, N_HEADS, heads_per_dot)]
    st_bufs = (st_a, st_b)
    p_bufs = (p_a, p_b)
    n_new = k_ref.shape[1]
    n_keys = st_a.shape[0]
    chunk = ATTN_KEY_CHUNK
    key_chunks = [(k_ref, c, c) for c in range(0, n_new, chunk)]
    if has_ctx:
        key_chunks += [(kc_ref, c, n_new + c) for c in range(0, n_keys - n_new, chunk)]

    @pl.when(pl.program_id(1) == 0)
    def _():
        kmax = None
        for ref in (k_ref, kc_ref) if has_ctx else (k_ref,):
            for g in range(N_KV_HEADS):
                kk = ref[g].astype(F32)
                part = jnp.max(jnp.sum(kk * kk, axis=-1, keepdims=True))
                kmax = part if kmax is None else jnp.maximum(kmax, part)
        kmax_ref[0] = kmax

    bounded = qmax_ref[0] * kmax_ref[0] <= ATTN_SCORE_BOUND ** 2

    def score_chunks(i):
        r, h0 = stages[i]
        qs = jnp.concatenate([q_ref[r * tq:(r + 1) * tq, h * HEAD_DIM:(h + 1) * HEAD_DIM]
                              for h in range(h0, h0 + heads_per_dot)], axis=0)
        for ref, src, dst in key_chunks:
            yield dst, lax.dot_general(ref[h0 // group, src:src + chunk, :], qs, (((1,), (1,)), ((), ())),
                                       preferred_element_type=F32)

    def scores_bounded(i):
        for dst, st in score_chunks(i):
            p_bufs[i % 2][dst:dst + chunk, :] = jnp.exp2(st).astype(BF16)

    def scores_general(i):
        for dst, st in score_chunks(i):
            st_bufs[i % 2][dst:dst + chunk, :] = st

    def softmax_general(i):
        st_ref, p_ref = st_bufs[i % 2], p_bufs[i % 2]
        m = None
        for c in range(0, n_keys, chunk):
            part = _reduce_rows(st_ref[c:c + chunk, :], jnp.maximum, jnp.max)
            m = part if m is None else jnp.maximum(m, part)
        for c in range(0, n_keys, chunk):
            p_ref[c:c + chunk, :] = jnp.exp2(st_ref[c:c + chunk, :] - m).astype(BF16)

    def run(scores, softmax):
        outs = []
        scores(0)
        for i, (r, h0) in enumerate(stages):
            if i + 1 < len(stages):
                scores(i + 1)
            softmax(i)
            g = h0 // group
            ot = _dot(vt_ref[g], p_bufs[i % 2][:n_new, :])
            if has_ctx:
                ot = ot + _dot(vtc_ref[g], p_bufs[i % 2][n_new:, :])
            ot = ot[:HEAD_DIM] / ot[HEAD_DIM:HEAD_DIM + 1]
            outs += [ot[:, j * tq:(j + 1) * tq] for j in range(heads_per_dot)]
            if h0 + heads_per_dot == N_HEADS:
                o_ref[r * tq:(r + 1) * tq, :] = jnp.concatenate(outs, axis=0).T.astype(BF16)
                outs = []

    @pl.when(bounded)
    def _():
        run(scores_bounded, lambda i: None)

    @pl.when(jnp.logical_not(bounded))
    def _():
        run(scores_general, softmax_general)


def _attention(seq, q, k, vt, ctx, q_gain):
    qmax = (jnp.max(q_gain * q_gain) * (HEAD_DIM * Q_SCALE ** 2 * 1.02)).reshape(1)
    L = seq.seq_len
    vt_rows = HEAD_DIM + ATTN_ONES_ROWS
    tq = ATTN_TILE_M if L % ATTN_TILE_M == 0 else ATTN_SUB_M
    tps = L // tq
    cols = ATTN_HEADS_PER_DOT * ATTN_SUB_M
    in_specs = [
        pl.BlockSpec((tq, ATTN_WIDTH), lambda b, j: (b * tps + j, 0)),
        pl.BlockSpec((N_KV_HEADS, L, HEAD_DIM), lambda b, j: (0, b, 0)),
        pl.BlockSpec((N_KV_HEADS, vt_rows, L), lambda b, j: (0, 0, b)),
    ]
    args = [q, k, vt]
    n_keys = L
    if ctx is not None:
        e, k_ctx, vt_ctx = ctx
        past = k_ctx.shape[3]
        n_keys += past
        in_specs += [
            pl.BlockSpec((None, None, N_KV_HEADS, past, HEAD_DIM), lambda b, j: (b, e, 0, 0, 0)),
            pl.BlockSpec((None, None, N_KV_HEADS, vt_rows, past), lambda b, j: (b, e, 0, 0, 0)),
        ]
        args += [k_ctx, vt_ctx]
    assert L % ATTN_KEY_CHUNK == 0 and n_keys % ATTN_KEY_CHUNK == 0
    return pl.pallas_call(
        functools.partial(_attn_kernel, ctx is not None),
        grid=(seq.n_batch, tps),
        in_specs=in_specs + [pl.BlockSpec(memory_space=pltpu.SMEM)],
        out_specs=pl.BlockSpec((tq, ATTN_WIDTH), lambda b, j: (b * tps + j, 0)),
        out_shape=jax.ShapeDtypeStruct((seq.n_rows, ATTN_WIDTH), BF16),
        scratch_shapes=[pltpu.VMEM((n_keys, cols), F32), pltpu.VMEM((n_keys, cols), F32),
                        pltpu.VMEM((n_keys, cols), BF16), pltpu.VMEM((n_keys, cols), BF16),
                        pltpu.SMEM((1,), F32)],
        compiler_params=_params(2),
        name="attention",
    )(*args, qmax)


def _even_out_kernel(seq, attn_ref, p_ref, pp_ref, pn_ref, x_ref, gate_ref, wp_ref, ps_ref, wo_ref,
                     g_ref, b_ref, o_ref):
    halo = pp_ref.shape[0]

    def pre(s):
        sub = seq.sub_sizes[s]
        n_ext = sub + 2 * halo
        has_prev, has_next = seq.sub_edges(s)
        lo, hi = seq.neighbours(s, p_ref, pp_ref, pn_ref)
        p = p_ref[seq.sub_rows(s), :].astype(F32)
        ext = jnp.concatenate([jnp.where(has_prev, lo.astype(F32), 0.0), p,
                               jnp.where(has_next, hi.astype(F32), 0.0)], axis=0)
        pos = seq.sub_pos(s) + lax.broadcasted_iota(jnp.int32, (sub, 1), 0)
        mixed = []
        for gi, w in enumerate(POOL_WINDOWS):
            half = w // 2
            lanes = slice(gi * POOL_GROUP, (gi + 1) * POOL_GROUP)
            run = ext[:, lanes]
            span = 1
            while span < w:
                run = run + pltpu.roll(run, span, 0)
                span *= 2
            if half > 1:
                run = pltpu.roll(run, n_ext - (half - 1), 0)
            total = run[halo:halo + sub]
            cnt = jnp.minimum(pos + half, seq.seq_len) - jnp.maximum(pos - half, 0)
            centred = total / cnt.astype(F32) - p[:, lanes]
            mixed.append(_dot(centred.astype(BF16), wp_ref[gi]))
        pool = jnp.concatenate(mixed, axis=1) * ps_ref[...]
        return jnp.concatenate([attn_ref[seq.sub_rows(s), :], pool.astype(BF16)], axis=1)

    def mid(s, m):
        return _dot(m, wo_ref[...])

    def post(s, y):
        rows = seq.sub_rows(s)
        o_ref[rows, :] = _post_norm(x_ref[rows, :], gate_ref[...], y, g_ref[...], b_ref[...])

    _staggered(seq.n_sub, pre, mid, post)


def _even_out(seq, layer, e, attn, p, x, ada5, w_pool, pool_scale, w_out, ln_g, ln_b):
    prev_spec, next_spec = seq.halo_specs(POOL_WIDTH, HALO_BF16)
    return pl.pallas_call(
        functools.partial(_even_out_kernel, seq),
        grid=(seq.n_tiles,),
        in_specs=[
            seq.tile_spec(ATTN_WIDTH),
            seq.tile_spec(POOL_WIDTH), prev_spec, next_spec,
            seq.tile_spec(D_MODEL),
            seq.mod_spec(layer, 2),
            _resident((None, len(POOL_WINDOWS), POOL_GROUP, POOL_GROUP), lambda i: (e, 0, 0, 0)),
            _row_spec(e, POOL_WIDTH),
            _resident((None, D_MODEL, D_MODEL), lambda i: (layer, 0, 0)),
            _row_spec((layer, 0), D_MODEL, 2),
            _row_spec((layer, 0), D_MODEL, 2),
        ],
        out_specs=seq.tile_spec(D_MODEL),
        out_shape=jax.ShapeDtypeStruct((seq.n_rows, D_MODEL), F32),
        compiler_params=_params(1),
        name="even_out",
    )(attn, p, p, p, x, ada5, w_pool, pool_scale, w_out, ln_g, ln_b)


def _odd_in_kernel(seq, x_ref, sc_ref, sh_ref, w_ref, ch_out, bg_out, f_out):
    c = CONV_WIDTH

    def pre(s):
        return (_layer_norm(x_ref[seq.sub_rows(s), :]) * (1.0 + sc_ref[...]) + sh_ref[...]).astype(BF16)

    def mid(s, u):
        return _dot(u, w_ref[...])

    def post(s, proj):
        rows = seq.sub_rows(s)
        ch_out[rows, :] = (proj[:, 2 * c:3 * c] * proj[:, :c]).astype(BF16)
        bg_out[rows, :] = proj[:, c:2 * c].astype(BF16)
        f_out[rows, :] = proj[:, 3 * c:].astype(BF16)

    _staggered(seq.n_sub, pre, mid, post)


def _odd_in(seq, layer, o, x, ada5, w_in):
    n = seq.n_rows
    return pl.pallas_call(
        functools.partial(_odd_in_kernel, seq),
        grid=(seq.n_tiles,),
        in_specs=[
            seq.tile_spec(D_MODEL),
            seq.mod_spec(layer, 1),
            seq.mod_spec(layer, 0),
            _resident((None, D_MODEL, IN_ODD), lambda i: (o, 0, 0)),
        ],
        out_specs=[seq.tile_spec(CONV_WIDTH), seq.tile_spec(CONV_WIDTH), seq.tile_spec(FOURIER_WIDTH)],
        out_shape=[jax.ShapeDtypeStruct((n, CONV_WIDTH), BF16), jax.ShapeDtypeStruct((n, CONV_WIDTH), BF16),
                   jax.ShapeDtypeStruct((n, FOURIER_WIDTH), BF16)],
        compiler_params=_params(1),
        name="odd_in",
    )(x, ada5, ada5, w_in)


def _fourier_kernel(scale, f_ref, cl_ref, sl_ref, cc_ref, sc_ref, o_ref):
    fb = f_ref[...]
    g_cos = (_dot(fb, cc_ref[...]) * scale).astype(BF16)
    g_sin = (_dot(fb, sc_ref[...]) * scale).astype(BF16)
    o_ref[...] = (_dot(cl_ref[...], g_cos) - _dot(sl_ref[...], g_sin)).astype(BF16)


def _dft_angles(rows, n):
    k = lax.iota(jnp.int32, n)[None, :]
    return ((rows[:, None] * k) % n).astype(F32) * (2.0 * math.pi / n)


def _dft_expand_kernel(hc_ref, hs_ref, lc_ref, ls_ref, c_out, s_out):
    lc, ls = lc_ref[...], ls_ref[...]
    split = lc.shape[0]
    for r in range(hc_ref.shape[0]):
        hc, hs = hc_ref[r], hs_ref[r]
        c_out[r * split:(r + 1) * split, :] = (hc * lc - hs * ls).astype(BF16)
        s_out[r * split:(r + 1) * split, :] = (hs * lc + hc * ls).astype(BF16)


def _dft_tables(n):
    split = FOURIER_GROUP
    assert n % split == 0
    ang_hi = _dft_angles(lax.iota(jnp.int32, n // split) * split, n).reshape(n // split, 1, n)
    ang_lo = _dft_angles(lax.iota(jnp.int32, split), n)
    per_step = min(DFT_HI_ROWS_PER_STEP, n // split)
    assert (n // split) % per_step == 0
    hi_spec = pl.BlockSpec((per_step, 1, n), lambda i: (i, 0, 0))
    lo_spec = pl.BlockSpec((split, n), lambda i: (0, 0))
    out_spec = pl.BlockSpec((per_step * split, n), lambda i: (i, 0))
    return pl.pallas_call(
        _dft_expand_kernel,
        grid=(n // split // per_step,),
        in_specs=[hi_spec, hi_spec, lo_spec, lo_spec],
        out_specs=[out_spec, out_spec],
        out_shape=[jax.ShapeDtypeStruct((n, n), BF16)] * 2,
        compiler_params=_params(1),
        name="dft_tables",
    )(jnp.cos(ang_hi), jnp.sin(ang_hi), jnp.cos(ang_lo), jnp.sin(ang_lo))


def _fourier(seq, f, tabs):
    L = seq.seq_len
    cl, sl, cc, sc = tabs
    scale = 1.0 / math.sqrt(L * FOURIER_GROUP)
    return pl.pallas_call(
        functools.partial(_fourier_kernel, scale),
        grid=(seq.n_batch,),
        in_specs=[
            pl.BlockSpec((L, FOURIER_WIDTH), lambda b: (b, 0)),
            _resident((L, L), lambda b: (0, 0)),
            _resident((L, L), lambda b: (0, 0)),
            _resident((FOURIER_WIDTH, FOURIER_WIDTH), lambda b: (0, 0)),
            _resident((FOURIER_WIDTH, FOURIER_WIDTH), lambda b: (0, 0)),
        ],
        out_specs=pl.BlockSpec((L, FOURIER_WIDTH), lambda b: (b, 0)),
        out_shape=jax.ShapeDtypeStruct((seq.n_rows, FOURIER_WIDTH), BF16),
        compiler_params=_params(1),
        name="fourier",
    )(f, cl, sl, cc, sc)


def _odd_out_kernel(seq, ch_ref, chp_ref, chn_ref, bg_ref, fo_ref, x_ref, gate_ref, cw_ref, cb_ref, wo_ref,
                    g_ref, b_ref, o_ref):
    halo = chp_ref.shape[0]

    def pre(s):
        has_prev, has_next = seq.sub_edges(s)
        rows = seq.sub_rows(s)
        lo, hi = seq.neighbours(s, ch_ref, chp_ref, chn_ref)
        ch = ch_ref[rows, :].astype(F32)
        prev_row = jnp.where(has_prev, lo.astype(F32)[halo - 1:halo, :], 0.0)
        next_row = jnp.where(has_next, hi.astype(F32)[0:1, :], 0.0)
        up, down = _shift_rows(ch, prev_row, next_row)
        conv = cb_ref[...] + up * cw_ref[0:1, :] + ch * cw_ref[1:2, :] + down * cw_ref[2:3, :]
        conv_out = bg_ref[rows, :].astype(F32) * conv
        return jnp.concatenate([conv_out.astype(BF16), fo_ref[rows, :]], axis=1)

    def mid(s, m):
        return _dot(m, wo_ref[...])

    def post(s, y):
        rows = seq.sub_rows(s)
        o_ref[rows, :] = _post_norm(x_ref[rows, :], gate_ref[...], y, g_ref[...], b_ref[...])

    _staggered(seq.n_sub, pre, mid, post)


def _odd_out(seq, layer, o, ch, bg, fo, x, ada5, conv_w, conv_b, w_out, ln_g, ln_b):
    prev_spec, next_spec = seq.halo_specs(CONV_WIDTH, HALO_BF16)
    return pl.pallas_call(
        functools.partial(_odd_out_kernel, seq),
        grid=(seq.n_tiles,),
        in_specs=[
            seq.tile_spec(CONV_WIDTH), prev_spec, next_spec,
            seq.tile_spec(CONV_WIDTH),
            seq.tile_spec(FOURIER_WIDTH),
            seq.tile_spec(D_MODEL),
            seq.mod_spec(layer, 2),
            pl.BlockSpec((None, 3, CONV_WIDTH), lambda i: (o, 0, 0)),
            _row_spec(o, CONV_WIDTH),
            _resident((None, D_MODEL, D_MODEL), lambda i: (layer, 0, 0)),
            _row_spec((layer, 0), D_MODEL, 2),
            _row_spec((layer, 0), D_MODEL, 2),
        ],
        out_specs=seq.tile_spec(D_MODEL),
        out_shape=jax.ShapeDtypeStruct((seq.n_rows, D_MODEL), F32),
        compiler_params=_params(1),
        name="odd_out",
    )(ch, ch, ch, bg, fo, x, ada5, conv_w, conv_b, w_out, ln_g, ln_b)


def _ffn_kernel(seq, x_ref, xp_ref, xn_ref, sc_ref, sh_ref, gate_ref, wa_ref, wg_ref, cw_ref, cb_ref, wd_ref,
                g_ref, b_ref, o_ref):
    halo = xp_ref.shape[0]
    n_sub = seq.n_sub

    def stage_up(s):
        sub = seq.sub_sizes[s]
        has_prev, has_next = seq.sub_edges(s)
        lo, hi = seq.neighbours(s, x_ref, xp_ref, xn_ref)
        x = x_ref[seq.sub_rows(s), :]
        u_ext = (_layer_norm(jnp.concatenate([lo, x, hi], axis=0)) * (1.0 + sc_ref[...]) + sh_ref[...]).astype(BF16)
        a_ext = _dot(u_ext, wa_ref[...])
        gate_lin = _dot(u_ext[halo:halo + sub], wg_ref[...])
        row = lax.broadcasted_iota(jnp.int32, (sub + 2 * halo, 1), 0)
        inside = ((row >= halo) | has_prev) & ((row < halo + sub) | has_next)
        return x, jnp.where(inside, a_ext, 0.0), gate_lin

    def stage_hidden(a_ext, gate_lin):
        n_ext = a_ext.shape[0]
        sub = n_ext - 2 * halo
        up = pltpu.roll(a_ext, 1, 0)[halo:halo + sub]
        mid = a_ext[halo:halo + sub]
        down = pltpu.roll(a_ext, n_ext - 1, 0)[halo:halo + sub]
        conv = cb_ref[...] + up * cw_ref[0:1, :] + mid * cw_ref[1:2, :] + down * cw_ref[2:3, :]
        return (conv * jax.nn.sigmoid(conv) * gate_lin).astype(BF16)

    def stage_out(s, x, y):
        o_ref[seq.sub_rows(s), :] = _post_norm(x, gate_ref[...], y, g_ref[...], b_ref[...])

    ups = {0: stage_up(0)}
    ys = {}
    for s in range(n_sub):
        if s + 1 < n_sub:
            ups[s + 1] = stage_up(s + 1)
        x, a_ext, gate_lin = ups.pop(s)
        h = stage_hidden(a_ext, gate_lin)
        if s > 0:
            stage_out(s - 1, *ys.pop(s - 1))
        ys[s] = (x, _dot(h, wd_ref[...]))
    stage_out(n_sub - 1, *ys.pop(n_sub - 1))


def _ffn(seq, layer, x, ada5, w_up, ffn_conv_w, ffn_conv_b, w_down, ln_g, ln_b):
    prev_spec, next_spec = seq.halo_specs(D_MODEL, HALO_F32)
    return pl.pallas_call(
        functools.partial(_ffn_kernel, seq),
        grid=(seq.n_tiles,),
        in_specs=[
            seq.tile_spec(D_MODEL), prev_spec, next_spec,
            seq.mod_spec(layer, 4),
            seq.mod_spec(layer, 3),
            seq.mod_spec(layer, 5),
            _resident((None, D_MODEL, D_FF), lambda i: (layer, 0, 0)),
            _resident((None, D_MODEL, D_FF), lambda i: (layer, 0, 1)),
            pl.BlockSpec((None, 3, D_FF), lambda i: (layer, 0, 0)),
            _row_spec(layer, D_FF),
            _resident((None, D_FF, D_MODEL), lambda i: (layer, 0, 0)),
            _row_spec((layer, 1), D_MODEL, 2),
            _row_spec((layer, 1), D_MODEL, 2),
        ],
        out_specs=seq.tile_spec(D_MODEL),
        out_shape=jax.ShapeDtypeStruct((seq.n_rows, D_MODEL), F32),
        compiler_params=_params(1),
        name="conv_ffn",
    )(x, x, x, ada5, ada5, ada5, w_up, w_up, ffn_conv_w, ffn_conv_b, w_down, ln_g, ln_b)


def _pipelined(n_sub, stages):
    state = {}
    for t in range(len(stages) + n_sub - 1):
        for s in range(n_sub):
            k = t - s
            if 0 <= k < len(stages):
                state[s] = stages[k](s, state.get(s))


def _odd_ffn_kernel(seq, ch_ref, chp_ref, chn_ref, bg_ref, bgp_ref, bgn_ref, fo_ref, fop_ref, fon_ref,
                    x_ref, xp_ref, xn_ref, g1_ref, sc2_ref, sh2_ref, g2_ref, cw_ref, cb_ref, wo_ref,
                    lg1_ref, lb1_ref, wa_ref, wg_ref, fcw_ref, fcb_ref, wd_ref, lg2_ref, lb2_ref, o_ref):
    halo = xp_ref.shape[0]
    wide = chp_ref.shape[0]

    def widened(s, ref, prev_ref, next_ref):
        lo, hi = seq.neighbours(s, ref, prev_ref, next_ref)
        return lo.astype(F32), ref[seq.sub_rows(s), :].astype(F32), hi.astype(F32)

    def mix(s, _):
        sub = seq.sub_sizes[s]
        has_prev, has_next = seq.sub_edges(s)
        lo, ch, hi = widened(s, ch_ref, chp_ref, chn_ref)
        ch_w = jnp.concatenate([jnp.where(has_prev, lo, 0.0), ch, jnp.where(has_next, hi, 0.0)], axis=0)
        n_w = sub + 2 * wide
        conv = (cb_ref[...] + pltpu.roll(ch_w, 1, 0) * cw_ref[0:1, :] + ch_w * cw_ref[1:2, :]
                + pltpu.roll(ch_w, n_w - 1, 0) * cw_ref[2:3, :])
        keep = slice(wide - halo, wide + sub + halo)
        conv_out = (jnp.concatenate(widened(s, bg_ref, bgp_ref, bgn_ref), axis=0) * conv)[keep]
        four = jnp.concatenate(widened(s, fo_ref, fop_ref, fon_ref), axis=0)[keep]
        return jnp.concatenate([conv_out.astype(BF16), four.astype(BF16)], axis=1)

    def project(s, m_ext):
        return _dot(m_ext, wo_ref[...])

    def norms(s, y_ext):
        sub = seq.sub_sizes[s]
        lo, hi = seq.neighbours(s, x_ref, xp_ref, xn_ref)
        x_ext = jnp.concatenate([lo, x_ref[seq.sub_rows(s), :], hi], axis=0)
        x1_ext = _post_norm(x_ext, g1_ref[...], y_ext, lg1_ref[...], lb1_ref[...])
        u_ext = (_layer_norm(x1_ext) * (1.0 + sc2_ref[...]) + sh2_ref[...]).astype(BF16)
        return x1_ext[halo:halo + sub], u_ext

    def up(s, st):
        x1, u_ext = st
        sub = seq.sub_sizes[s]
        has_prev, has_next = seq.sub_edges(s)
        a_ext = _dot(u_ext, wa_ref[...])
        gate_lin = _dot(u_ext[halo:halo + sub], wg_ref[...])
        row = lax.broadcasted_iota(jnp.int32, (sub + 2 * halo, 1), 0)
        inside = ((row >= halo) | has_prev) & ((row < halo + sub) | has_next)
        return x1, jnp.where(inside, a_ext, 0.0), gate_lin

    def hidden(s, st):
        x1, a_ext, gate_lin = st
        n_ext = a_ext.shape[0]
        inner = slice(halo, n_ext - halo)
        conv = (fcb_ref[...] + pltpu.roll(a_ext, 1, 0)[inner] * fcw_ref[0:1, :] + a_ext[inner] * fcw_ref[1:2, :]
                + pltpu.roll(a_ext, n_ext - 1, 0)[inner] * fcw_ref[2:3, :])
        return x1, (conv * jax.nn.sigmoid(conv) * gate_lin).astype(BF16)

    def down(s, st):
        x1, h = st
        return x1, _dot(h, wd_ref[...])

    def out(s, st):
        x1, y = st
        o_ref[seq.sub_rows(s), :] = _post_norm(x1, g2_ref[...], y, lg2_ref[...], lb2_ref[...])

    _pipelined(seq.n_sub, [mix, project, norms, up, hidden, down, out])


def _odd_ffn(seq, layer, o, ch, bg, fo, x, ada5, conv_w, conv_b, w_out, w_up, ffn_conv_w, ffn_conv_b, w_down,
             ln_g, ln_b):
    def with_halos(width, halo):
        return [seq.tile_spec(width), *seq.halo_specs(width, halo)]

    return pl.pallas_call(
        functools.partial(_odd_ffn_kernel, seq),
        grid=(seq.n_tiles,),
        in_specs=[
            *with_halos(CONV_WIDTH, HALO_BF16),
            *with_halos(CONV_WIDTH, HALO_BF16),
            *with_halos(FOURIER_WIDTH, HALO_BF16),
            *with_halos(D_MODEL, HALO_F32),
            seq.mod_spec(layer, 2), seq.mod_spec(layer, 4), seq.mod_spec(layer, 3), seq.mod_spec(layer, 5),
            pl.BlockSpec((None, 3, CONV_WIDTH), lambda i: (o, 0, 0)),
            _row_spec(o, CONV_WIDTH),
            _resident((None, D_MODEL, D_MODEL), lambda i: (layer, 0, 0)),
            _row_spec((layer, 0), D_MODEL, 2),
            _row_spec((layer, 0), D_MODEL, 2),
            _resident((None, D_MODEL, D_FF), lambda i: (layer, 0, 0)),
            _resident((None, D_MODEL, D_FF), lambda i: (layer, 0, 1)),
            pl.BlockSpec((None, 3, D_FF), lambda i: (layer, 0, 0)),
            _row_spec(layer, D_FF),
            _resident((None, D_FF, D_MODEL), lambda i: (layer, 0, 0)),
            _row_spec((layer, 1), D_MODEL, 2),
            _row_spec((layer, 1), D_MODEL, 2),
        ],
        out_specs=seq.tile_spec(D_MODEL),
        out_shape=jax.ShapeDtypeStruct((seq.n_rows, D_MODEL), F32),
        compiler_params=_params(1),
        name="odd_ffn",
    )(ch, ch, ch, bg, bg, bg, fo, fo, fo, x, x, x, ada5, ada5, ada5, ada5, conv_w, conv_b, w_out, ln_g, ln_b,
      w_up, w_up, ffn_conv_w, ffn_conv_b, w_down, ln_g, ln_b)


def _rope_tables(seq_len):
    t = lax.iota(jnp.int32, seq_len)
    row = (t // GRID_W).astype(F32)
    col = (t % GRID_W).astype(F32)
    n_freq = HEAD_DIM // 4
    inv = 1.0 / (ROPE_THETA ** (jnp.arange(n_freq, dtype=F32) / n_freq))
    ang = jnp.concatenate([row[:, None] * inv, col[:, None] * inv], -1)
    cos = jnp.repeat(jnp.cos(ang), 2, axis=1)
    sin = jnp.repeat(jnp.sin(ang), 2, axis=1) * jnp.tile(jnp.array([-1.0, 1.0], F32), HEAD_DIM // 2)
    reps = LANES // HEAD_DIM
    return jnp.tile(cos, (1, reps)), jnp.tile(sin, (1, reps))


def _fourier_tables(seq_len):
    cl, sl = _dft_tables(seq_len)
    ang = _dft_angles(lax.iota(jnp.int32, FOURIER_GROUP), FOURIER_GROUP)
    eye = jnp.eye(N_FOURIER_GROUPS, dtype=F32)
    return cl, sl, jnp.kron(eye, jnp.cos(ang)).astype(BF16), jnp.kron(eye, jnp.sin(ang)).astype(BF16)


def _run_trunk(seq, x, ada5, ctx_k, ctx_v, wts):
    (w_in_even, gq, gk, bd, w_pool, pool_scale, w_in_odd, conv_w, conv_b, w_out, w_up, ffn_conv_w,
     ffn_conv_b, w_down, ln_g, ln_b) = wts
    rope_tabs = _rope_tables(seq.seq_len) if seq.latent else None
    four_tabs = _fourier_tables(seq.seq_len)
    new_k, new_v = [], []
    for layer in range(DEPTH):
        if layer % 2 == 0:
            e = layer // 2
            outs = _even_in(seq, layer, e, x, ada5, w_in_even, gq, gk, bd, rope_tabs)
            q, k, vt, p = outs[:4]
            if not seq.latent:
                new_k.append(outs[4])
                new_v.append(outs[5])
            attn = _attention(seq, q, k, vt, (e, ctx_k, ctx_v) if seq.latent else None, gq[e])
            x = _even_out(seq, layer, e, attn, p, x, ada5, w_pool, pool_scale, w_out, ln_g, ln_b)
        else:
            o = layer // 2
            ch, bg, f = _odd_in(seq, layer, o, x, ada5, w_in_odd)
            fo = _fourier(seq, f, four_tabs)
            x = _odd_ffn(seq.with_tile(FUSED_TILE_M), layer, o, ch, bg, fo, x, ada5, conv_w, conv_b, w_out, w_up,
                         ffn_conv_w, ffn_conv_b, w_down, ln_g, ln_b)
            continue
        x = _ffn(seq.with_tile(FFN_TILE_M), layer, x, ada5, w_up, ffn_conv_w, ffn_conv_b, w_down,
                 ln_g, ln_b)
    return x, new_k, new_v


def kernel(x_prompt, x_sample, cache_k, cache_v, c, c_ctx, w_ada, b_ada, w_in_even, q_norm_g, k_norm_g,
           w_pool, pool_scale, w_in_odd, conv_w, conv_b, w_out, w_up, ffn_conv_w, ffn_conv_b, w_down,
           ln_g, ln_b):
    n_prompt, prompt_len, _ = x_prompt.shape
    n_sample, sample_len, _ = x_sample.shape
    n_even = w_in_even.shape[0]
    n_odd = w_in_odd.shape[0]
    assert n_sample <= CTX_ROW

    cond = jnp.zeros((COND_ROWS, D_MODEL), F32).at[:n_sample].set(c).at[CTX_ROW].set(c_ctx)
    ada = _ada_all(cond, w_ada, b_ada)
    ada5 = ada.reshape(DEPTH, COND_ROWS, 6, 1, D_MODEL)

    head_of = lax.iota(jnp.int32, ATTN_WIDTH) // HEAD_DIM
    bd = (head_of[:, None] == head_of[None, :]).astype(BF16)
    wts = (
        w_in_even.astype(BF16),
        jnp.tile(q_norm_g, (1, N_HEADS)).reshape(n_even, 1, ATTN_WIDTH),
        jnp.tile(k_norm_g, (1, N_KV_HEADS)).reshape(n_even, 1, KV_WIDTH),
        bd,
        w_pool.astype(BF16),
        pool_scale.reshape(n_even, 1, POOL_WIDTH),
        w_in_odd.astype(BF16),
        conv_w,
        conv_b.reshape(n_odd, 1, CONV_WIDTH),
        w_out.astype(BF16),
        w_up.astype(BF16),
        ffn_conv_w,
        ffn_conv_b.reshape(DEPTH, 1, D_FF),
        w_down.astype(BF16),
        ln_g.reshape(DEPTH, 2, 1, D_MODEL),
        ln_b.reshape(DEPTH, 2, 1, D_MODEL),
    )

    prompt = _Seq(n_prompt, prompt_len, latent=False)
    y_prompt, ks, vs = _run_trunk(prompt, x_prompt.reshape(-1, D_MODEL), ada5, None, None, wts)
    cache_shape = (n_prompt, prompt_len, N_KV_HEADS, HEAD_DIM)
    new_cache_k = jnp.stack([k.reshape(cache_shape) for k in ks], 1)
    new_cache_v = jnp.stack([v.reshape(cache_shape) for v in vs], 1)

    sample = _Seq(n_sample, sample_len, latent=True)
    past_len = cache_k.shape[2]
    ctx_k = cache_k.astype(BF16).transpose(0, 1, 3, 2, 4)
    ones = jnp.ones((n_sample, n_even, N_KV_HEADS, ATTN_ONES_ROWS, past_len), BF16)
    ctx_v = jnp.concatenate([cache_v.astype(BF16).transpose(0, 1, 3, 4, 2), ones], axis=3)
    y_sample, _, _ = _run_trunk(sample, x_sample.reshape(-1, D_MODEL), ada5, ctx_k, ctx_v, wts)

    return (y_prompt.reshape(x_prompt.shape), y_sample.reshape(x_sample.shape), new_cache_k, new_cache_v)
```

```python
import functools
import math

import jax
import jax.numpy as jnp
from jax import lax
from jax.experimental import pallas as pl
from jax.experimental.pallas import tpu as pltpu

D_MODEL = 1024
DEPTH = 4
GRID_W = 64
N_HEADS = 8
N_KV_HEADS = 2
HEAD_DIM = 64
ATTN_WIDTH = N_HEADS * HEAD_DIM
KV_WIDTH = N_KV_HEADS * HEAD_DIM
POOL_WIDTH = D_MODEL - ATTN_WIDTH
POOL_WINDOWS = (2, 4, 8, 16)
POOL_GROUP = POOL_WIDTH // len(POOL_WINDOWS)
FOURIER_WIDTH = D_MODEL // 4
N_FOURIER_GROUPS = 4
FOURIER_GROUP = FOURIER_WIDTH // N_FOURIER_GROUPS
CONV_WIDTH = D_MODEL - FOURIER_WIDTH
D_FF = 2816
ROPE_THETA = 10000.0
LN_EPS = 1e-6
RMS_EPS = 1e-6
IN_EVEN = ATTN_WIDTH + 2 * KV_WIDTH + POOL_WIDTH
IN_ODD = 3 * CONV_WIDTH + FOURIER_WIDTH
DEEPNORM_ALPHA = (2 * DEPTH) ** 0.25
Q_SCALE = HEAD_DIM ** -0.5 * math.log2(math.e)

SUBLANES = 8
LANES = 128
VMEM_LIMIT_BYTES = 56 * 1024 * 1024

TILE_M = 1024
FUSED_TILE_M = 512
SUB_M = 256
ATTN_TILE_M = 512
ATTN_SUB_M = 256
ATTN_ONES_ROWS = 16
ATTN_HEADS_PER_DOT = 2
ATTN_KEY_CHUNK = 256
ATTN_SCORE_BOUND = 80.0
HALO_F32 = SUBLANES
HALO_BF16 = 2 * SUBLANES
DFT_HI_ROWS_PER_STEP = 4
COND_ROWS = 16
CTX_ROW = 8

F32 = jnp.float32
BF16 = jnp.bfloat16


def _params(n_axes):
    return pltpu.CompilerParams(dimension_semantics=("arbitrary",) * n_axes,
                                vmem_limit_bytes=VMEM_LIMIT_BYTES)


def _resident(block_shape, index_map):
    return pl.BlockSpec(block_shape, index_map, pipeline_mode=pl.Buffered(1))


def _dot(a, b):
    return jnp.dot(a, b, preferred_element_type=F32)


def _layer_norm(x):
    mu = jnp.mean(x, axis=-1, keepdims=True)
    xc = x - mu
    var = jnp.mean(xc * xc, axis=-1, keepdims=True)
    return xc * lax.rsqrt(var + LN_EPS)


def _post_norm(x, gate, y, g, b):
    return _layer_norm(DEEPNORM_ALPHA * x + gate * y) * g + b


def _staggered(n, pre, mid, post):
    state = pre(0)
    done = None
    for s in range(n):
        cur = mid(s, state)
        if s + 1 < n:
            state = pre(s + 1)
        if done is not None:
            post(s - 1, done)
        done = cur
    post(n - 1, done)


def _pipelined(n_sub, stages):
    state = {}
    for t in range(len(stages) + n_sub - 1):
        for s in range(n_sub):
            k = t - s
            if 0 <= k < len(stages):
                state[s] = stages[k](s, state.get(s))


def _ada_kernel(cond_ref, w_ref, b_ref, o_ref):
    cnd = cond_ref[...]
    act = (cnd * jax.nn.sigmoid(cnd)).astype(BF16)
    o_ref[...] = _dot(act, w_ref[...].astype(BF16)) + b_ref[...]


def _ada_all(cond, w_ada, b_ada):
    tn = 3072
    n_out = 6 * D_MODEL
    return pl.pallas_call(
        _ada_kernel,
        grid=(DEPTH, n_out // tn),
        in_specs=[
            pl.BlockSpec((COND_ROWS, D_MODEL), lambda l, j: (0, 0)),
            pl.BlockSpec((None, D_MODEL, tn), lambda l, j: (l, 0, j)),
            pl.BlockSpec((None, 1, tn), lambda l, j: (l, 0, j)),
        ],
        out_specs=pl.BlockSpec((None, COND_ROWS, tn), lambda l, j: (l, 0, j)),
        out_shape=jax.ShapeDtypeStruct((DEPTH, COND_ROWS, n_out), F32),
        compiler_params=_params(2),
        name="ada",
    )(cond, w_ada, b_ada.reshape(DEPTH, 1, n_out))


def _sub_sizes(tile_m, seq_len):
    unit = min(SUB_M, seq_len)
    assert tile_m % unit == 0 and seq_len % unit == 0
    return (unit,) * (tile_m // unit)


class _Seq:
    def __init__(self, n_batch, seq_len, latent, tile_m=TILE_M):
        self.n_rows = n_batch * seq_len
        assert self.n_rows % tile_m == 0
        assert seq_len % tile_m == 0 or not latent
        self.n_batch = n_batch
        self.seq_len = seq_len
        self.latent = latent
        self.tile_m = tile_m
        self.sub_sizes = _sub_sizes(tile_m, seq_len)
        self.sub_starts = tuple(sum(self.sub_sizes[:s]) for s in range(len(self.sub_sizes)))
        self.n_sub = len(self.sub_sizes)
        self.n_tiles = self.n_rows // tile_m

    def with_tile(self, tile_m):
        return _Seq(self.n_batch, self.seq_len, self.latent, tile_m)

    def cond_row(self, i):
        return (i * self.tile_m) // self.seq_len if self.latent else CTX_ROW

    def tile_spec(self, width):
        return pl.BlockSpec((self.tile_m, width), lambda i: (i, 0))

    def halo_specs(self, width, halo):
        per_tile = self.tile_m // halo
        last = self.n_rows // halo - 1
        prev = pl.BlockSpec((halo, width), lambda i: (jnp.maximum(i * per_tile - 1, 0), 0))
        nxt = pl.BlockSpec((halo, width), lambda i: (jnp.minimum((i + 1) * per_tile, last), 0))
        return prev, nxt

    def with_halos(self, width, halo):
        return [self.tile_spec(width), *self.halo_specs(width, halo)]

    def mod_spec(self, layer, which):
        return pl.BlockSpec((None, None, None, 1, D_MODEL),
                            lambda i: (layer, self.cond_row(i), which, 0, 0))

    def sub_pos(self, s):
        if self.sub_sizes[s] == self.seq_len:
            return 0
        return (pl.program_id(0) * self.tile_m + self.sub_starts[s]) % self.seq_len

    def sub_edges(self, s):
        if self.sub_sizes[s] == self.seq_len:
            return False, False
        pos = self.sub_pos(s)
        return pos > 0, pos + self.sub_sizes[s] < self.seq_len

    def sub_rows(self, s):
        return slice(self.sub_starts[s], self.sub_starts[s] + self.sub_sizes[s])

    def neighbours(self, s, ref, prev_ref, next_ref):
        halo = prev_ref.shape[0]
        start, stop = self.sub_starts[s], self.sub_starts[s] + self.sub_sizes[s]
        lo = ref[start - halo:start, :] if s > 0 else prev_ref[...]
        hi = ref[stop:stop + halo, :] if s < self.n_sub - 1 else next_ref[...]
        return lo, hi

    def widened(self, s, ref, prev_ref, next_ref, mask_edges):
        lo, hi = self.neighbours(s, ref, prev_ref, next_ref)
        lo, hi = lo.astype(F32), hi.astype(F32)
        if mask_edges:
            has_prev, has_next = self.sub_edges(s)
            lo, hi = jnp.where(has_prev, lo, 0.0), jnp.where(has_next, hi, 0.0)
        return jnp.concatenate([lo, ref[self.sub_rows(s), :].astype(F32), hi], axis=0)


def _row_spec(layer, width, n_lead=1):
    if n_lead == 1:
        return pl.BlockSpec((None, 1, width), lambda i: (layer, 0, 0))
    return pl.BlockSpec((None, None, 1, width), lambda i: (layer[0], layer[1], 0, 0))


def _even_in_kernel(seq, x_ref, sc_ref, sh_ref, w_ref, gq_ref, gk_ref, bd_ref, *rest):
    if seq.latent:
        cos_ref, sin_ref, q_out, k_out, vt_out, p_out = rest
    else:
        q_out, k_out, vt_out, p_out, kraw_out, vraw_out = rest
    o1 = ATTN_WIDTH
    o2 = o1 + KV_WIDTH
    o3 = o2 + KV_WIDTH

    def rope(t, cos, sin):
        even_lane = (lax.broadcasted_iota(jnp.int32, cos.shape, 1) & 1) == 0
        outs = []
        for j in range(t.shape[1] // LANES):
            slab = t[:, j * LANES:(j + 1) * LANES]
            partner = jnp.where(even_lane, pltpu.roll(slab, LANES - 1, 1), pltpu.roll(slab, 1, 1))
            outs.append(slab * cos + partner * sin)
        return outs[0] if len(outs) == 1 else jnp.concatenate(outs, axis=1)

    def pre(s):
        return (_layer_norm(x_ref[seq.sub_rows(s), :]) * (1.0 + sc_ref[...]) + sh_ref[...]).astype(BF16)

    def mid(s, u):
        return _dot(u, w_ref[...])

    def post(s, proj):
        rows = seq.sub_rows(s)
        q = proj[:, :o1]
        k = proj[:, o1:o2]
        v = proj[:, o2:o3]
        p_out[rows, :] = proj[:, o3:].astype(BF16)
        bd = bd_ref[...]
        q = q * lax.rsqrt(_dot((q * q).astype(BF16), bd) * (1.0 / HEAD_DIM) + RMS_EPS) * gq_ref[...]
        k = k * lax.rsqrt(_dot((k * k).astype(BF16), bd[:KV_WIDTH, :KV_WIDTH]) * (1.0 / HEAD_DIM)
                          + RMS_EPS) * gk_ref[...]
        if seq.latent:
            cos = cos_ref[rows, :]
            sin = sin_ref[rows, :]
            q = rope(q, cos, sin)
            k = rope(k, cos, sin)
        else:
            kraw_out[rows, :] = k
            vraw_out[rows, :] = v
        q_out[rows, :] = (q * Q_SCALE).astype(BF16)
        vt = v.T.astype(BF16)
        for g in range(N_KV_HEADS):
            heads = slice(g * HEAD_DIM, (g + 1) * HEAD_DIM)
            k_out[g, rows, :] = k[:, heads].astype(BF16)
            vt_out[g, :HEAD_DIM, rows] = vt[heads, :]
            vt_out[g, HEAD_DIM:, rows] = jnp.ones((ATTN_ONES_ROWS, seq.sub_sizes[s]), BF16)

    _staggered(seq.n_sub, pre, mid, post)


def _even_in(seq, layer, e, x, ada5, w_in, gq, gk, bd, rope_tabs):
    n = seq.n_rows
    in_specs = [
        seq.tile_spec(D_MODEL),
        seq.mod_spec(layer, 1),
        seq.mod_spec(layer, 0),
        _resident((None, D_MODEL, IN_EVEN), lambda i: (e, 0, 0)),
        _row_spec(e, ATTN_WIDTH),
        _row_spec(e, KV_WIDTH),
        _resident((ATTN_WIDTH, ATTN_WIDTH), lambda i: (0, 0)),
    ]
    args = [x, ada5, ada5, w_in, gq, gk, bd]
    vt_rows = HEAD_DIM + ATTN_ONES_ROWS
    out_specs = [seq.tile_spec(ATTN_WIDTH),
                 pl.BlockSpec((N_KV_HEADS, seq.tile_m, HEAD_DIM), lambda i: (0, i, 0)),
                 pl.BlockSpec((N_KV_HEADS, vt_rows, seq.tile_m), lambda i: (0, 0, i)),
                 seq.tile_spec(POOL_WIDTH)]
    out_shape = [jax.ShapeDtypeStruct((n, ATTN_WIDTH), BF16),
                 jax.ShapeDtypeStruct((N_KV_HEADS, n, HEAD_DIM), BF16),
                 jax.ShapeDtypeStruct((N_KV_HEADS, vt_rows, n), BF16),
                 jax.ShapeDtypeStruct((n, POOL_WIDTH), BF16)]
    if seq.latent:
        tiles_per_seq = seq.seq_len // seq.tile_m
        in_specs += [pl.BlockSpec((seq.tile_m, LANES), lambda i: (i % tiles_per_seq, 0))] * 2
        args += list(rope_tabs)
    else:
        out_specs += [seq.tile_spec(KV_WIDTH)] * 2
        out_shape += [jax.ShapeDtypeStruct((n, KV_WIDTH), F32)] * 2
    return pl.pallas_call(
        functools.partial(_even_in_kernel, seq),
        grid=(seq.n_tiles,),
        in_specs=in_specs,
        out_specs=out_specs,
        out_shape=out_shape,
        compiler_params=_params(1),
        name="even_in",
    )(*args)


def _reduce_rows(x, op, final, chunk=256):
    n = x.shape[0]
    if n > chunk and n % chunk == 0:
        parts = [x[i:i + chunk] for i in range(0, n, chunk)]
        while len(parts) > 1:
            parts = [op(parts[i], parts[i + 1]) if i + 1 < len(parts) else parts[i]
                     for i in range(0, len(parts), 2)]
        x = parts[0]
        n = chunk
    while n > SUBLANES and n % (2 * SUBLANES) == 0:
        n //= 2
        x = op(x[:n], x[n:])
    return final(x, axis=0, keepdims=True)


def _attn_kernel(has_ctx, q_ref, k_ref, vt_ref, *rest):
    if has_ctx:
        kc_ref, vtc_ref, qmax_ref, o_ref, st_a, st_b, p_a, p_b, kmax_ref = rest
    else:
        qmax_ref, o_ref, st_a, st_b, p_a, p_b, kmax_ref = rest
        kc_ref = vtc_ref = None
    group = N_HEADS // N_KV_HEADS
    heads_per_dot = ATTN_HEADS_PER_DOT
    tq = ATTN_SUB_M
    stages = [(r, h0) for r in range(q_ref.shape[0] // tq) for h0 in range(0, N_HEADS, heads_per_dot)]
    st_bufs = (st_a, st_b)
    p_bufs = (p_a, p_b)
    n_new = k_ref.shape[1]
    n_keys = st_a.shape[0]
    chunk = ATTN_KEY_CHUNK
    key_chunks = [(k_ref, c, c) for c in range(0, n_new, chunk)]
    if has_ctx:
        key_chunks += [(kc_ref, c, n_new + c) for c in range(0, n_keys - n_new, chunk)]

    @pl.when(pl.program_id(1) == 0)
    def _():
        kmax = None
        for ref in (k_ref, kc_ref) if has_ctx else (k_ref,):
            for g in range(N_KV_HEADS):
                kk = ref[g].astype(F32)
                part = jnp.max(jnp.sum(kk * kk, axis=-1, keepdims=True))
                kmax = part if kmax is None else jnp.maximum(kmax, part)
        kmax_ref[0] = kmax

    bounded = qmax_ref[0] * kmax_ref[0] <= ATTN_SCORE_BOUND ** 2

    def score_chunks(i):
        r, h0 = stages[i]
        qs = jnp.concatenate([q_ref[r * tq:(r + 1) * tq, h * HEAD_DIM:(h + 1) * HEAD_DIM]
                              for h in range(h0, h0 + heads_per_dot)], axis=0)
        for ref, src, dst in key_chunks:
            yield dst, lax.dot_general(ref[h0 // group, src:src + chunk, :], qs, (((1,), (1,)), ((), ())),
                                       preferred_element_type=F32)

    def scores_bounded(i):
        for dst, st in score_chunks(i):
            p_bufs[i % 2][dst:dst + chunk, :] = jnp.exp2(st).astype(BF16)

    def scores_general(i):
        for dst, st in score_chunks(i):
            st_bufs[i % 2][dst:dst + chunk, :] = st

    def softmax_general(i):
        st_ref, p_ref = st_bufs[i % 2], p_bufs[i % 2]
        m = None
        for c in range(0, n_keys, chunk):
            part = _reduce_rows(st_ref[c:c + chunk, :], jnp.maximum, jnp.max)
            m = part if m is None else jnp.maximum(m, part)
        for c in range(0, n_keys, chunk):
            p_ref[c:c + chunk, :] = jnp.exp2(st_ref[c:c + chunk, :] - m).astype(BF16)

    def run(scores, softmax):
        outs = []
        scores(0)
        for i, (r, h0) in enumerate(stages):
            if i + 1 < len(stages):
                scores(i + 1)
            softmax(i)
            g = h0 // group
            ot = _dot(vt_ref[g], p_bufs[i % 2][:n_new, :])
            if has_ctx:
                ot = ot + _dot(vtc_ref[g], p_bufs[i % 2][n_new:, :])
            ot = ot[:HEAD_DIM] / ot[HEAD_DIM:HEAD_DIM + 1]
            outs += [ot[:, j * tq:(j + 1) * tq] for j in range(heads_per_dot)]
            if h0 + heads_per_dot == N_HEADS:
                o_ref[r * tq:(r + 1) * tq, :] = jnp.concatenate(outs, axis=0).T.astype(BF16)
                outs = []

    @pl.when(bounded)
    def _():
        run(scores_bounded, lambda i: None)

    @pl.when(jnp.logical_not(bounded))
    def _():
        run(scores_general, softmax_general)


def _attention(seq, q, k, vt, ctx, q_gain):
    qmax = (jnp.max(q_gain * q_gain) * (HEAD_DIM * Q_SCALE ** 2 * 1.02)).reshape(1)
    L = seq.seq_len
    vt_rows = HEAD_DIM + ATTN_ONES_ROWS
    tq = ATTN_TILE_M if L % ATTN_TILE_M == 0 else ATTN_SUB_M
    tps = L // tq
    cols = ATTN_HEADS_PER_DOT * ATTN_SUB_M
    in_specs = [
        pl.BlockSpec((tq, ATTN_WIDTH), lambda b, j: (b * tps + j, 0)),
        pl.BlockSpec((N_KV_HEADS, L, HEAD_DIM), lambda b, j: (0, b, 0)),
        pl.BlockSpec((N_KV_HEADS, vt_rows, L), lambda b, j: (0, 0, b)),
    ]
    args = [q, k, vt]
    n_keys = L
    if ctx is not None:
        e, k_ctx, vt_ctx = ctx
        past = k_ctx.shape[3]
        n_keys += past
        in_specs += [
            pl.BlockSpec((None, None, N_KV_HEADS, past, HEAD_DIM), lambda b, j: (b, e, 0, 0, 0)),
            pl.BlockSpec((None, None, N_KV_HEADS, vt_rows, past), lambda b, j: (b, e, 0, 0, 0)),
        ]
        args += [k_ctx, vt_ctx]
    assert L % ATTN_KEY_CHUNK == 0 and n_keys % ATTN_KEY_CHUNK == 0
    return pl.pallas_call(
        functools.partial(_attn_kernel, ctx is not None),
        grid=(seq.n_batch, tps),
        in_specs=in_specs + [pl.BlockSpec(memory_space=pltpu.SMEM)],
        out_specs=pl.BlockSpec((tq, ATTN_WIDTH), lambda b, j: (b * tps + j, 0)),
        out_shape=jax.ShapeDtypeStruct((seq.n_rows, ATTN_WIDTH), BF16),
        scratch_shapes=[pltpu.VMEM((n_keys, cols), F32), pltpu.VMEM((n_keys, cols), F32),
                        pltpu.VMEM((n_keys, cols), BF16), pltpu.VMEM((n_keys, cols), BF16),
                        pltpu.SMEM((1,), F32)],
        compiler_params=_params(2),
        name="attention",
    )(*args, qmax)


def _odd_in_kernel(seq, x_ref, sc_ref, sh_ref, w_ref, ch_out, bg_out, f_out):
    c = CONV_WIDTH

    def pre(s):
        return (_layer_norm(x_ref[seq.sub_rows(s), :]) * (1.0 + sc_ref[...]) + sh_ref[...]).astype(BF16)

    def mid(s, u):
        return _dot(u, w_ref[...])

    def post(s, proj):
        rows = seq.sub_rows(s)
        ch_out[rows, :] = (proj[:, 2 * c:3 * c] * proj[:, :c]).astype(BF16)
        bg_out[rows, :] = proj[:, c:2 * c].astype(BF16)
        f_out[rows, :] = proj[:, 3 * c:].astype(BF16)

    _staggered(seq.n_sub, pre, mid, post)


def _odd_in(seq, layer, o, x, ada5, w_in):
    n = seq.n_rows
    return pl.pallas_call(
        functools.partial(_odd_in_kernel, seq),
        grid=(seq.n_tiles,),
        in_specs=[
            seq.tile_spec(D_MODEL),
            seq.mod_spec(layer, 1),
            seq.mod_spec(layer, 0),
            _resident((None, D_MODEL, IN_ODD), lambda i: (o, 0, 0)),
        ],
        out_specs=[seq.tile_spec(CONV_WIDTH), seq.tile_spec(CONV_WIDTH), seq.tile_spec(FOURIER_WIDTH)],
        out_shape=[jax.ShapeDtypeStruct((n, CONV_WIDTH), BF16), jax.ShapeDtypeStruct((n, CONV_WIDTH), BF16),
                   jax.ShapeDtypeStruct((n, FOURIER_WIDTH), BF16)],
        compiler_params=_params(1),
        name="odd_in",
    )(x, ada5, ada5, w_in)


def _fourier_kernel(scale, f_ref, cl_ref, sl_ref, cc_ref, sc_ref, o_ref):
    fb = f_ref[...]
    g_cos = (_dot(fb, cc_ref[...]) * scale).astype(BF16)
    g_sin = (_dot(fb, sc_ref[...]) * scale).astype(BF16)
    o_ref[...] = (_dot(cl_ref[...], g_cos) - _dot(sl_ref[...], g_sin)).astype(BF16)


def _dft_angles(rows, n):
    k = lax.iota(jnp.int32, n)[None, :]
    return ((rows[:, None] * k) % n).astype(F32) * (2.0 * math.pi / n)


def _dft_expand_kernel(hc_ref, hs_ref, lc_ref, ls_ref, c_out, s_out):
    lc, ls = lc_ref[...], ls_ref[...]
    split = lc.shape[0]
    for r in range(hc_ref.shape[0]):
        hc, hs = hc_ref[r], hs_ref[r]
        c_out[r * split:(r + 1) * split, :] = (hc * lc - hs * ls).astype(BF16)
        s_out[r * split:(r + 1) * split, :] = (hs * lc + hc * ls).astype(BF16)


def _dft_tables(n):
    split = FOURIER_GROUP
    assert n % split == 0
    ang_hi = _dft_angles(lax.iota(jnp.int32, n // split) * split, n).reshape(n // split, 1, n)
    ang_lo = _dft_angles(lax.iota(jnp.int32, split), n)
    per_step = min(DFT_HI_ROWS_PER_STEP, n // split)
    assert (n // split) % per_step == 0
    hi_spec = pl.BlockSpec((per_step, 1, n), lambda i: (i, 0, 0))
    lo_spec = pl.BlockSpec((split, n), lambda i: (0, 0))
    out_spec = pl.BlockSpec((per_step * split, n), lambda i: (i, 0))
    return pl.pallas_call(
        _dft_expand_kernel,
        grid=(n // split // per_step,),
        in_specs=[hi_spec, hi_spec, lo_spec, lo_spec],
        out_specs=[out_spec, out_spec],
        out_shape=[jax.ShapeDtypeStruct((n, n), BF16)] * 2,
        compiler_params=_params(1),
        name="dft_tables",
    )(jnp.cos(ang_hi), jnp.sin(ang_hi), jnp.cos(ang_lo), jnp.sin(ang_lo))


def _fourier(seq, f, tabs):
    L = seq.seq_len
    cl, sl, cc, sc = tabs
    scale = 1.0 / math.sqrt(L * FOURIER_GROUP)
    return pl.pallas_call(
        functools.partial(_fourier_kernel, scale),
        grid=(seq.n_batch,),
        in_specs=[
            pl.BlockSpec((L, FOURIER_WIDTH), lambda b: (b, 0)),
            _resident((L, L), lambda b: (0, 0)),
            _resident((L, L), lambda b: (0, 0)),
            _resident((FOURIER_WIDTH, FOURIER_WIDTH), lambda b: (0, 0)),
            _resident((FOURIER_WIDTH, FOURIER_WIDTH), lambda b: (0, 0)),
        ],
        out_specs=pl.BlockSpec((L, FOURIER_WIDTH), lambda b: (b, 0)),
        out_shape=jax.ShapeDtypeStruct((seq.n_rows, FOURIER_WIDTH), BF16),
        compiler_params=_params(1),
        name="fourier",
    )(f, cl, sl, cc, sc)


def _odd_mix(seq, halo, refs):
    ch_ref, chp_ref, chn_ref, bg_ref, bgp_ref, bgn_ref, fo_ref, fop_ref, fon_ref, cw_ref, cb_ref = refs
    wide = chp_ref.shape[0]
    assert wide >= halo + 1

    def mix(s, _):
        sub = seq.sub_sizes[s]
        n_w = sub + 2 * wide
        keep = slice(wide - halo, wide + sub + halo)
        ch_w = seq.widened(s, ch_ref, chp_ref, chn_ref, True)
        conv = (cb_ref[...] + pltpu.roll(ch_w, 1, 0) * cw_ref[0:1, :] + ch_w * cw_ref[1:2, :]
                + pltpu.roll(ch_w, n_w - 1, 0) * cw_ref[2:3, :])
        conv_out = (seq.widened(s, bg_ref, bgp_ref, bgn_ref, False) * conv)[keep]
        four = seq.widened(s, fo_ref, fop_ref, fon_ref, False)[keep]
        return jnp.concatenate([conv_out.astype(BF16), four.astype(BF16)], axis=1)

    return mix


def _even_mix(seq, halo, refs):
    attn_ref, attnp_ref, attnn_ref, p_ref, pp_ref, pn_ref, wp_ref, ps_ref = refs
    wide = pp_ref.shape[0]
    assert wide >= halo + max(POOL_WINDOWS) // 2

    def mix(s, _):
        sub = seq.sub_sizes[s]
        n_w = sub + 2 * wide
        keep = slice(wide - halo, wide + sub + halo)
        p_w = seq.widened(s, p_ref, pp_ref, pn_ref, True)
        pos = seq.sub_pos(s) - halo + lax.broadcasted_iota(jnp.int32, (sub + 2 * halo, 1), 0)
        mixed = []
        for gi, w in enumerate(POOL_WINDOWS):
            half = w // 2
            lanes = slice(gi * POOL_GROUP, (gi + 1) * POOL_GROUP)
            run = p_w[:, lanes]
            span = 1
            while span < w:
                run = run + pltpu.roll(run, span, 0)
                span *= 2
            if half > 1:
                run = pltpu.roll(run, n_w - (half - 1), 0)
            cnt = jnp.maximum(jnp.minimum(pos + half, seq.seq_len) - jnp.maximum(pos - half, 0), 1)
            centred = run[keep] / cnt.astype(F32) - p_w[keep, lanes]
            mixed.append(_dot(centred.astype(BF16), wp_ref[gi]))
        pool = jnp.concatenate(mixed, axis=1) * ps_ref[...]
        attn = seq.widened(s, attn_ref, attnp_ref, attnn_ref, False)[keep]
        return jnp.concatenate([attn.astype(BF16), pool.astype(BF16)], axis=1)

    return mix


N_TAIL_REFS = 18


def _mixer_ffn_kernel(seq, make_mix, *refs):
    (x_ref, xp_ref, xn_ref, g1_ref, sc2_ref, sh2_ref, g2_ref, wo_ref, lg1_ref, lb1_ref, wa_ref, wg_ref,
     fcw_ref, fcb_ref, wd_ref, lg2_ref, lb2_ref, o_ref) = refs[-N_TAIL_REFS:]
    halo = xp_ref.shape[0]
    mix = make_mix(seq, halo, refs[:-N_TAIL_REFS])

    def project(s, m_ext):
        return _dot(m_ext, wo_ref[...])

    def norms(s, y_ext):
        sub = seq.sub_sizes[s]
        x_ext = seq.widened(s, x_ref, xp_ref, xn_ref, False)
        x1_ext = _post_norm(x_ext, g1_ref[...], y_ext, lg1_ref[...], lb1_ref[...])
        u_ext = (_layer_norm(x1_ext) * (1.0 + sc2_ref[...]) + sh2_ref[...]).astype(BF16)
        return x1_ext[halo:halo + sub], u_ext

    def up(s, st):
        x1, u_ext = st
        sub = seq.sub_sizes[s]
        has_prev, has_next = seq.sub_edges(s)
        a_ext = _dot(u_ext, wa_ref[...])
        gate_lin = _dot(u_ext[halo:halo + sub], wg_ref[...])
        row = lax.broadcasted_iota(jnp.int32, (sub + 2 * halo, 1), 0)
        inside = ((row >= halo) | has_prev) & ((row < halo + sub) | has_next)
        return x1, jnp.where(inside, a_ext, 0.0), gate_lin

    def hidden(s, st):
        x1, a_ext, gate_lin = st
        n_ext = a_ext.shape[0]
        inner = slice(halo, n_ext - halo)
        conv = (fcb_ref[...] + pltpu.roll(a_ext, 1, 0)[inner] * fcw_ref[0:1, :] + a_ext[inner] * fcw_ref[1:2, :]
                + pltpu.roll(a_ext, n_ext - 1, 0)[inner] * fcw_ref[2:3, :])
        return x1, (conv * jax.nn.sigmoid(conv) * gate_lin).astype(BF16)

    def down(s, st):
        x1, h = st
        return x1, _dot(h, wd_ref[...])

    def out(s, st):
        x1, y = st
        o_ref[seq.sub_rows(s), :] = _post_norm(x1, g2_ref[...], y, lg2_ref[...], lb2_ref[...])

    _pipelined(seq.n_sub, [mix, project, norms, up, hidden, down, out])


def _mixer_ffn(seq, name, make_mix, mix_specs, mix_args, layer, x, ada5, w_out, w_up, ffn_conv_w, ffn_conv_b,
               w_down, ln_g, ln_b):
    tail_specs = [
        *seq.with_halos(D_MODEL, HALO_F32),
        seq.mod_spec(layer, 2), seq.mod_spec(layer, 4), seq.mod_spec(layer, 3), seq.mod_spec(layer, 5),
        _resident((None, D_MODEL, D_MODEL), lambda i: (layer, 0, 0)),
        _row_spec((layer, 0), D_MODEL, 2),
        _row_spec((layer, 0), D_MODEL, 2),
        _resident((None, D_MODEL, D_FF), lambda i: (layer, 0, 0)),
        _resident((None, D_MODEL, D_FF), lambda i: (layer, 0, 1)),
        pl.BlockSpec((None, 3, D_FF), lambda i: (layer, 0, 0)),
        _row_spec(layer, D_FF),
        _resident((None, D_FF, D_MODEL), lambda i: (layer, 0, 0)),
        _row_spec((layer, 1), D_MODEL, 2),
        _row_spec((layer, 1), D_MODEL, 2),
    ]
    tail_args = [x, x, x, ada5, ada5, ada5, ada5, w_out, ln_g, ln_b, w_up, w_up, ffn_conv_w, ffn_conv_b, w_down,
                 ln_g, ln_b]
    assert len(tail_specs) + 1 == N_TAIL_REFS and len(tail_args) + 1 == N_TAIL_REFS
    return pl.pallas_call(
        functools.partial(_mixer_ffn_kernel, seq, make_mix),
        grid=(seq.n_tiles,),
        in_specs=list(mix_specs) + tail_specs,
        out_specs=seq.tile_spec(D_MODEL),
        out_shape=jax.ShapeDtypeStruct((seq.n_rows, D_MODEL), F32),
        compiler_params=_params(1),
        name=name,
    )(*mix_args, *tail_args)


def _odd_ffn(seq, layer, o, ch, bg, fo, x, ada5, conv_w, conv_b, *tail):
    mix_specs = [*seq.with_halos(CONV_WIDTH, HALO_BF16), *seq.with_halos(CONV_WIDTH, HALO_BF16),
                 *seq.with_halos(FOURIER_WIDTH, HALO_BF16),
                 pl.BlockSpec((None, 3, CONV_WIDTH), lambda i: (o, 0, 0)), _row_spec(o, CONV_WIDTH)]
    mix_args = [ch, ch, ch, bg, bg, bg, fo, fo, fo, conv_w, conv_b]
    return _mixer_ffn(seq, "odd_ffn", _odd_mix, mix_specs, mix_args, layer, x, ada5, *tail)


def _even_ffn(seq, layer, e, attn, p, x, ada5, w_pool, pool_scale, *tail):
    mix_specs = [*seq.with_halos(ATTN_WIDTH, HALO_BF16), *seq.with_halos(POOL_WIDTH, HALO_BF16),
                 _resident((None, len(POOL_WINDOWS), POOL_GROUP, POOL_GROUP), lambda i: (e, 0, 0, 0)),
                 _row_spec(e, POOL_WIDTH)]
    mix_args = [attn, attn, attn, p, p, p, w_pool, pool_scale]
    return _mixer_ffn(seq, "even_ffn", _even_mix, mix_specs, mix_args, layer, x, ada5, *tail)


def _rope_tables(seq_len):
    t = lax.iota(jnp.int32, seq_len)
    row = (t // GRID_W).astype(F32)
    col = (t % GRID_W).astype(F32)
    n_freq = HEAD_DIM // 4
    inv = 1.0 / (ROPE_THETA ** (jnp.arange(n_freq, dtype=F32) / n_freq))
    ang = jnp.concatenate([row[:, None] * inv, col[:, None] * inv], -1)
    cos = jnp.repeat(jnp.cos(ang), 2, axis=1)
    sin = jnp.repeat(jnp.sin(ang), 2, axis=1) * jnp.tile(jnp.array([-1.0, 1.0], F32), HEAD_DIM // 2)
    reps = LANES // HEAD_DIM
    return jnp.tile(cos, (1, reps)), jnp.tile(sin, (1, reps))


def _fourier_tables(seq_len):
    cl, sl = _dft_tables(seq_len)
    ang = _dft_angles(lax.iota(jnp.int32, FOURIER_GROUP), FOURIER_GROUP)
    eye = jnp.eye(N_FOURIER_GROUPS, dtype=F32)
    return cl, sl, jnp.kron(eye, jnp.cos(ang)).astype(BF16), jnp.kron(eye, jnp.sin(ang)).astype(BF16)


def _run_trunk(seq, x, ada5, ctx_k, ctx_v, wts):
    (w_in_even, gq, gk, bd, w_pool, pool_scale, w_in_odd, conv_w, conv_b, w_out, w_up, ffn_conv_w,
     ffn_conv_b, w_down, ln_g, ln_b) = wts
    rope_tabs = _rope_tables(seq.seq_len) if seq.latent else None
    four_tabs = _fourier_tables(seq.seq_len)
    fused = seq.with_tile(FUSED_TILE_M)
    tail = (w_out, w_up, ffn_conv_w, ffn_conv_b, w_down, ln_g, ln_b)
    new_k, new_v = [], []
    for layer in range(DEPTH):
        if layer % 2 == 0:
            e = layer // 2
            outs = _even_in(seq, layer, e, x, ada5, w_in_even, gq, gk, bd, rope_tabs)
            q, k, vt, p = outs[:4]
            if not seq.latent:
                new_k.append(outs[4])
                new_v.append(outs[5])
            attn = _attention(seq, q, k, vt, (e, ctx_k, ctx_v) if seq.latent else None, gq[e])
            x = _even_ffn(fused, layer, e, attn, p, x, ada5, w_pool, pool_scale, *tail)
        else:
            o = layer // 2
            ch, bg, f = _odd_in(seq, layer, o, x, ada5, w_in_odd)
            fo = _fourier(seq, f, four_tabs)
            x = _odd_ffn(fused, layer, o, ch, bg, fo, x, ada5, conv_w, conv_b, *tail)
    return x, new_k, new_v


def kernel(x_prompt, x_sample, cache_k, cache_v, c, c_ctx, w_ada, b_ada, w_in_even, q_norm_g, k_norm_g,
           w_pool, pool_scale, w_in_odd, conv_w, conv_b, w_out, w_up, ffn_conv_w, ffn_conv_b, w_down,
           ln_g, ln_b):
    n_prompt, prompt_len, _ = x_prompt.shape
    n_sample, sample_len, _ = x_sample.shape
    n_even = w_in_even.shape[0]
    n_odd = w_in_odd.shape[0]
    assert n_sample <= CTX_ROW

    cond = jnp.zeros((COND_ROWS, D_MODEL), F32).at[:n_sample].set(c).at[CTX_ROW].set(c_ctx)
    ada = _ada_all(cond, w_ada, b_ada)
    ada5 = ada.reshape(DEPTH, COND_ROWS, 6, 1, D_MODEL)

    head_of = lax.iota(jnp.int32, ATTN_WIDTH) // HEAD_DIM
    bd = (head_of[:, None] == head_of[None, :]).astype(BF16)
    wts = (
        w_in_even.astype(BF16),
        jnp.tile(q_norm_g, (1, N_HEADS)).reshape(n_even, 1, ATTN_WIDTH),
        jnp.tile(k_norm_g, (1, N_KV_HEADS)).reshape(n_even, 1, KV_WIDTH),
        bd,
        w_pool.astype(BF16),
        pool_scale.reshape(n_even, 1, POOL_WIDTH),
        w_in_odd.astype(BF16),
        conv_w,
        conv_b.reshape(n_odd, 1, CONV_WIDTH),
        w_out.astype(BF16),
        w_up.astype(BF16),
        ffn_conv_w,
        ffn_conv_b.reshape(DEPTH, 1, D_FF),
        w_down.astype(BF16),
        ln_g.reshape(DEPTH, 2, 1, D_MODEL),
        ln_b.reshape(DEPTH, 2, 1, D_MODEL),
    )

    prompt = _Seq(n_prompt, prompt_len, latent=False)
    y_prompt, ks, vs = _run_trunk(prompt, x_prompt.reshape(-1, D_MODEL), ada5, None, None, wts)
    cache_shape = (n_prompt, prompt_len, N_KV_HEADS, HEAD_DIM)
    new_cache_k = jnp.stack([k.reshape(cache_shape) for k in ks], 1)
    new_cache_v = jnp.stack([v.reshape(cache_shape) for v in vs], 1)

    sample = _Seq(n_sample, sample_len, latent=True)
    past_len = cache_k.shape[2]
    ctx_k = cache_k.astype(BF16).transpose(0, 1, 3, 2, 4)
    ones = jnp.ones((n_sample, n_even, N_KV_HEADS, ATTN_ONES_ROWS, past_len), BF16)
    ctx_v = jnp.concatenate([cache_v.astype(BF16).transpose(0, 1, 3, 4, 2), ones], axis=3)
    y_sample, _, _ = _run_trunk(sample, x_sample.reshape(-1, D_MODEL), ada5, ctx_k, ctx_v, wts)

    return (y_prompt.reshape(x_prompt.shape), y_sample.reshape(x_sample.shape), new_cache_k, new_cache_v)
```

```python
import functools
import math

import jax
import jax.numpy as jnp
from jax import lax
from jax.experimental import pallas as pl
from jax.experimental.pallas import tpu as pltpu

D_MODEL = 1024
DEPTH = 4
GRID_W = 64
N_HEADS = 8
N_KV_HEADS = 2
HEAD_DIM = 64
ATTN_WIDTH = N_HEADS * HEAD_DIM
KV_WIDTH = N_KV_HEADS * HEAD_DIM
POOL_WIDTH = D_MODEL - ATTN_WIDTH
POOL_WINDOWS = (2, 4, 8, 16)
POOL_GROUP = POOL_WIDTH // len(POOL_WINDOWS)
FOURIER_WIDTH = D_MODEL // 4
N_FOURIER_GROUPS = 4
FOURIER_GROUP = FOURIER_WIDTH // N_FOURIER_GROUPS
CONV_WIDTH = D_MODEL - FOURIER_WIDTH
D_FF = 2816
ROPE_THETA = 10000.0
LN_EPS = 1e-6
RMS_EPS = 1e-6
IN_EVEN = ATTN_WIDTH + 2 * KV_WIDTH + POOL_WIDTH
IN_ODD = 3 * CONV_WIDTH + FOURIER_WIDTH
DEEPNORM_ALPHA = (2 * DEPTH) ** 0.25
Q_SCALE = HEAD_DIM ** -0.5 * math.log2(math.e)

SUBLANES = 8
LANES = 128
VMEM_LIMIT_BYTES = 56 * 1024 * 1024

TILE_M = 2048
FUSED_TILE_M = 512
SUB_M = 256
ATTN_TILE_M = 512
ATTN_SUB_M = 256
ATTN_ONES_ROWS = 16
ATTN_HEADS_PER_DOT = 2
ATTN_KEY_CHUNK = 256
ATTN_SCORE_BOUND = 80.0
HALO_F32 = SUBLANES
HALO_BF16 = 2 * SUBLANES
DFT_HI_ROWS_PER_STEP = 4
COND_ROWS = 16
CTX_ROW = 8

F32 = jnp.float32
BF16 = jnp.bfloat16


def _params(n_axes):
    return pltpu.CompilerParams(dimension_semantics=("arbitrary",) * n_axes,
                                vmem_limit_bytes=VMEM_LIMIT_BYTES)


def _resident(block_shape, index_map):
    return pl.BlockSpec(block_shape, index_map, pipeline_mode=pl.Buffered(1))


def _dot(a, b):
    return jnp.dot(a, b, preferred_element_type=F32)


def _layer_norm(x):
    mu = jnp.mean(x, axis=-1, keepdims=True)
    xc = x - mu
    var = jnp.mean(xc * xc, axis=-1, keepdims=True)
    return xc * lax.rsqrt(var + LN_EPS)


def _post_norm(x, gate, y, g, b):
    return _layer_norm(DEEPNORM_ALPHA * x + gate * y) * g + b


def _staggered(n, pre, mid, post):
    state = pre(0)
    done = None
    for s in range(n):
        cur = mid(s, state)
        if s + 1 < n:
            state = pre(s + 1)
        if done is not None:
            post(s - 1, done)
        done = cur
    post(n - 1, done)


def _pipelined(n_sub, stages):
    state = {}
    for t in range(len(stages) + n_sub - 1):
        for s in range(n_sub):
            k = t - s
            if 0 <= k < len(stages):
                state[s] = stages[k](s, state.get(s))


def _ada_kernel(cond_ref, w_ref, b_ref, o_ref):
    cnd = cond_ref[...]
    act = (cnd * jax.nn.sigmoid(cnd)).astype(BF16)
    o_ref[...] = _dot(act, w_ref[...].astype(BF16)) + b_ref[...]


def _ada_all(cond, w_ada, b_ada):
    tn = 3072
    n_out = 6 * D_MODEL
    return pl.pallas_call(
        _ada_kernel,
        grid=(DEPTH, n_out // tn),
        in_specs=[
            pl.BlockSpec((COND_ROWS, D_MODEL), lambda l, j: (0, 0)),
            pl.BlockSpec((None, D_MODEL, tn), lambda l, j: (l, 0, j)),
            pl.BlockSpec((None, 1, tn), lambda l, j: (l, 0, j)),
        ],
        out_specs=pl.BlockSpec((None, COND_ROWS, tn), lambda l, j: (l, 0, j)),
        out_shape=jax.ShapeDtypeStruct((DEPTH, COND_ROWS, n_out), F32),
        compiler_params=_params(2),
        name="ada",
    )(cond, w_ada, b_ada.reshape(DEPTH, 1, n_out))


def _sub_sizes(tile_m, seq_len):
    unit = min(SUB_M, seq_len)
    assert tile_m % unit == 0 and seq_len % unit == 0
    return (unit,) * (tile_m // unit)


class _Seq:
    def __init__(self, n_batch, seq_len, latent, tile_m=TILE_M):
        self.n_rows = n_batch * seq_len
        assert self.n_rows % tile_m == 0
        assert seq_len % tile_m == 0 or not latent
        self.n_batch = n_batch
        self.seq_len = seq_len
        self.latent = latent
        self.tile_m = tile_m
        self.sub_sizes = _sub_sizes(tile_m, seq_len)
        self.sub_starts = tuple(sum(self.sub_sizes[:s]) for s in range(len(self.sub_sizes)))
        self.n_sub = len(self.sub_sizes)
        self.n_tiles = self.n_rows // tile_m

    def with_tile(self, tile_m):
        return _Seq(self.n_batch, self.seq_len, self.latent, tile_m)

    def cond_row(self, i):
        return (i * self.tile_m) // self.seq_len if self.latent else CTX_ROW

    def tile_spec(self, width):
        return pl.BlockSpec((self.tile_m, width), lambda i: (i, 0))

    def halo_specs(self, width, halo):
        per_tile = self.tile_m // halo
        last = self.n_rows // halo - 1
        prev = pl.BlockSpec((halo, width), lambda i: (jnp.maximum(i * per_tile - 1, 0), 0))
        nxt = pl.BlockSpec((halo, width), lambda i: (jnp.minimum((i + 1) * per_tile, last), 0))
        return prev, nxt

    def with_halos(self, width, halo):
        return [self.tile_spec(width), *self.halo_specs(width, halo)]

    def mod_spec(self, layer, which):
        return pl.BlockSpec((None, None, None, 1, D_MODEL),
                            lambda i: (layer, self.cond_row(i), which, 0, 0))

    def sub_pos(self, s):
        if self.sub_sizes[s] == self.seq_len:
            return 0
        return (pl.program_id(0) * self.tile_m + self.sub_starts[s]) % self.seq_len

    def sub_edges(self, s):
        if self.sub_sizes[s] == self.seq_len:
            return False, False
        pos = self.sub_pos(s)
        return pos > 0, pos + self.sub_sizes[s] < self.seq_len

    def sub_rows(self, s):
        return slice(self.sub_starts[s], self.sub_starts[s] + self.sub_sizes[s])

    def neighbours(self, s, ref, prev_ref, next_ref):
        halo = prev_ref.shape[0]
        start, stop = self.sub_starts[s], self.sub_starts[s] + self.sub_sizes[s]
        lo = ref[start - halo:start, :] if s > 0 else prev_ref[...]
        hi = ref[stop:stop + halo, :] if s < self.n_sub - 1 else next_ref[...]
        return lo, hi

    def widened(self, s, ref, prev_ref, next_ref, mask_edges):
        lo, hi = self.neighbours(s, ref, prev_ref, next_ref)
        lo, hi = lo.astype(F32), hi.astype(F32)
        if mask_edges:
            has_prev, has_next = self.sub_edges(s)
            lo, hi = jnp.where(has_prev, lo, 0.0), jnp.where(has_next, hi, 0.0)
        return jnp.concatenate([lo, ref[self.sub_rows(s), :].astype(F32), hi], axis=0)


def _row_spec(layer, width, n_lead=1):
    if n_lead == 1:
        return pl.BlockSpec((None, 1, width), lambda i: (layer, 0, 0))
    return pl.BlockSpec((None, None, 1, width), lambda i: (layer[0], layer[1], 0, 0))


def _even_in_kernel(seq, x_ref, sc_ref, sh_ref, w_ref, gq_ref, gk_ref, bd_ref, *rest):
    if seq.latent:
        cos_ref, sin_ref, q_out, k_out, vt_out, p_out = rest
    else:
        q_out, k_out, vt_out, p_out, kraw_out, vraw_out = rest
    o1 = ATTN_WIDTH
    o2 = o1 + KV_WIDTH
    o3 = o2 + KV_WIDTH

    def rope(t, cos, sin):
        even_lane = (lax.broadcasted_iota(jnp.int32, cos.shape, 1) & 1) == 0
        outs = []
        for j in range(t.shape[1] // LANES):
            slab = t[:, j * LANES:(j + 1) * LANES]
            partner = jnp.where(even_lane, pltpu.roll(slab, LANES - 1, 1), pltpu.roll(slab, 1, 1))
            outs.append(slab * cos + partner * sin)
        return outs[0] if len(outs) == 1 else jnp.concatenate(outs, axis=1)

    def pre(s):
        return (_layer_norm(x_ref[seq.sub_rows(s), :]) * (1.0 + sc_ref[...]) + sh_ref[...]).astype(BF16)

    def mid(s, u):
        return _dot(u, w_ref[...])

    def post(s, proj):
        rows = seq.sub_rows(s)
        q = proj[:, :o1]
        k = proj[:, o1:o2]
        v = proj[:, o2:o3]
        p_out[rows, :] = proj[:, o3:].astype(BF16)
        bd = bd_ref[...]
        q = q * lax.rsqrt(_dot((q * q).astype(BF16), bd) * (1.0 / HEAD_DIM) + RMS_EPS) * gq_ref[...]
        k = k * lax.rsqrt(_dot((k * k).astype(BF16), bd[:KV_WIDTH, :KV_WIDTH]) * (1.0 / HEAD_DIM)
                          + RMS_EPS) * gk_ref[...]
        if seq.latent:
            cos = cos_ref[rows, :]
            sin = sin_ref[rows, :]
            q = rope(q, cos, sin)
            k = rope(k, cos, sin)
        else:
            kraw_out[rows, :] = k
            vraw_out[rows, :] = v
        q_out[rows, :] = (q * Q_SCALE).astype(BF16)
        vt = v.T.astype(BF16)
        for g in range(N_KV_HEADS):
            heads = slice(g * HEAD_DIM, (g + 1) * HEAD_DIM)
            k_out[g, rows, :] = k[:, heads].astype(BF16)
            vt_out[g, :HEAD_DIM, rows] = vt[heads, :]
            vt_out[g, HEAD_DIM:, rows] = jnp.ones((ATTN_ONES_ROWS, seq.sub_sizes[s]), BF16)

    _staggered(seq.n_sub, pre, mid, post)


def _even_in(seq, layer, e, x, ada5, w_in, gq, gk, bd, rope_tabs):
    n = seq.n_rows
    in_specs = [
        seq.tile_spec(D_MODEL),
        seq.mod_spec(layer, 1),
        seq.mod_spec(layer, 0),
        _resident((None, D_MODEL, IN_EVEN), lambda i: (e, 0, 0)),
        _row_spec(e, ATTN_WIDTH),
        _row_spec(e, KV_WIDTH),
        _resident((ATTN_WIDTH, ATTN_WIDTH), lambda i: (0, 0)),
    ]
    args = [x, ada5, ada5, w_in, gq, gk, bd]
    vt_rows = HEAD_DIM + ATTN_ONES_ROWS
    out_specs = [seq.tile_spec(ATTN_WIDTH),
                 pl.BlockSpec((N_KV_HEADS, seq.tile_m, HEAD_DIM), lambda i: (0, i, 0)),
                 pl.BlockSpec((N_KV_HEADS, vt_rows, seq.tile_m), lambda i: (0, 0, i)),
                 seq.tile_spec(POOL_WIDTH)]
    out_shape = [jax.ShapeDtypeStruct((n, ATTN_WIDTH), BF16),
                 jax.ShapeDtypeStruct((N_KV_HEADS, n, HEAD_DIM), BF16),
                 jax.ShapeDtypeStruct((N_KV_HEADS, vt_rows, n), BF16),
                 jax.ShapeDtypeStruct((n, POOL_WIDTH), BF16)]
    if seq.latent:
        tiles_per_seq = seq.seq_len // seq.tile_m
        in_specs += [pl.BlockSpec((seq.tile_m, LANES), lambda i: (i % tiles_per_seq, 0))] * 2
        args += list(rope_tabs)
    else:
        out_specs += [seq.tile_spec(KV_WIDTH)] * 2
        out_shape += [jax.ShapeDtypeStruct((n, KV_WIDTH), F32)] * 2
    return pl.pallas_call(
        functools.partial(_even_in_kernel, seq),
        grid=(seq.n_tiles,),
        in_specs=in_specs,
        out_specs=out_specs,
        out_shape=out_shape,
        compiler_params=_params(1),
        name="even_in",
    )(*args)


def _reduce_rows(x, op, final, chunk=256):
    n = x.shape[0]
    if n > chunk and n % chunk == 0:
        parts = [x[i:i + chunk] for i in range(0, n, chunk)]
        while len(parts) > 1:
            parts = [op(parts[i], parts[i + 1]) if i + 1 < len(parts) else parts[i]
                     for i in range(0, len(parts), 2)]
        x = parts[0]
        n = chunk
    while n > SUBLANES and n % (2 * SUBLANES) == 0:
        n //= 2
        x = op(x[:n], x[n:])
    return final(x, axis=0, keepdims=True)


def _attn_kernel(has_ctx, q_ref, k_ref, vt_ref, *rest):
    if has_ctx:
        kc_ref, vtc_ref, qmax_ref, o_ref, st_a, st_b, p_a, p_b, kmax_ref = rest
    else:
        qmax_ref, o_ref, st_a, st_b, p_a, p_b, kmax_ref = rest
        kc_ref = vtc_ref = None
    group = N_HEADS // N_KV_HEADS
    heads_per_dot = ATTN_HEADS_PER_DOT
    tq = ATTN_SUB_M
    stages = [(r, h0) for r in range(q_ref.shape[0] // tq) for h0 in range(0, N_HEADS, heads_per_dot)]
    st_bufs = (st_a, st_b)
    p_bufs = (p_a, p_b)
    n_new = k_ref.shape[1]
    n_keys = st_a.shape[0]
    chunk = ATTN_KEY_CHUNK
    key_chunks = [(k_ref, c, c) for c in range(0, n_new, chunk)]
    if has_ctx:
        key_chunks += [(kc_ref, c, n_new + c) for c in range(0, n_keys - n_new, chunk)]

    @pl.when(pl.program_id(1) == 0)
    def _():
        kmax = None
        for ref in (k_ref, kc_ref) if has_ctx else (k_ref,):
            for g in range(N_KV_HEADS):
                kk = ref[g].astype(F32)
                part = jnp.max(jnp.sum(kk * kk, axis=-1, keepdims=True))
                kmax = part if kmax is None else jnp.maximum(kmax, part)
        kmax_ref[0] = kmax

    bounded = qmax_ref[0] * kmax_ref[0] <= ATTN_SCORE_BOUND ** 2

    def score_chunks(i):
        r, h0 = stages[i]
        qs = jnp.concatenate([q_ref[r * tq:(r + 1) * tq, h * HEAD_DIM:(h + 1) * HEAD_DIM]
                              for h in range(h0, h0 + heads_per_dot)], axis=0)
        for ref, src, dst in key_chunks:
            yield dst, lax.dot_general(ref[h0 // group, src:src + chunk, :], qs, (((1,), (1,)), ((), ())),
                                       preferred_element_type=F32)

    def scores_bounded(i):
        for dst, st in score_chunks(i):
            p_bufs[i % 2][dst:dst + chunk, :] = jnp.exp2(st).astype(BF16)

    def scores_general(i):
        for dst, st in score_chunks(i):
            st_bufs[i % 2][dst:dst + chunk, :] = st

    def softmax_general(i):
        st_ref, p_ref = st_bufs[i % 2], p_bufs[i % 2]
        m = None
        for c in range(0, n_keys, chunk):
            part = _reduce_rows(st_ref[c:c + chunk, :], jnp.maximum, jnp.max)
            m = part if m is None else jnp.maximum(m, part)
        for c in range(0, n_keys, chunk):
            p_ref[c:c + chunk, :] = jnp.exp2(st_ref[c:c + chunk, :] - m).astype(BF16)

    def run(scores, softmax):
        outs = []
        scores(0)
        for i, (r, h0) in enumerate(stages):
            if i + 1 < len(stages):
                scores(i + 1)
            softmax(i)
            g = h0 // group
            ot = _dot(vt_ref[g], p_bufs[i % 2][:n_new, :])
            if has_ctx:
                ot = ot + _dot(vtc_ref[g], p_bufs[i % 2][n_new:, :])
            ot = ot[:HEAD_DIM] / ot[HEAD_DIM:HEAD_DIM + 1]
            outs += [ot[:, j * tq:(j + 1) * tq] for j in range(heads_per_dot)]
            if h0 + heads_per_dot == N_HEADS:
                o_ref[r * tq:(r + 1) * tq, :] = jnp.concatenate(outs, axis=0).T.astype(BF16)
                outs = []

    @pl.when(bounded)
    def _():
        run(scores_bounded, lambda i: None)

    @pl.when(jnp.logical_not(bounded))
    def _():
        run(scores_general, softmax_general)


def _attention(seq, q, k, vt, ctx, q_gain):
    qmax = (jnp.max(q_gain * q_gain) * (HEAD_DIM * Q_SCALE ** 2 * 1.02)).reshape(1)
    L = seq.seq_len
    vt_rows = HEAD_DIM + ATTN_ONES_ROWS
    tq = ATTN_TILE_M if L % ATTN_TILE_M == 0 else ATTN_SUB_M
    tps = L // tq
    cols = ATTN_HEADS_PER_DOT * ATTN_SUB_M
    in_specs = [
        pl.BlockSpec((tq, ATTN_WIDTH), lambda b, j: (b * tps + j, 0)),
        pl.BlockSpec((N_KV_HEADS, L, HEAD_DIM), lambda b, j: (0, b, 0)),
        pl.BlockSpec((N_KV_HEADS, vt_rows, L), lambda b, j: (0, 0, b)),
    ]
    args = [q, k, vt]
    n_keys = L
    if ctx is not None:
        e, k_ctx, vt_ctx = ctx
        past = k_ctx.shape[3]
        n_keys += past
        in_specs += [
            pl.BlockSpec((None, None, N_KV_HEADS, past, HEAD_DIM), lambda b, j: (b, e, 0, 0, 0)),
            pl.BlockSpec((None, None, N_KV_HEADS, vt_rows, past), lambda b, j: (b, e, 0, 0, 0)),
        ]
        args += [k_ctx, vt_ctx]
    assert L % ATTN_KEY_CHUNK == 0 and n_keys % ATTN_KEY_CHUNK == 0
    return pl.pallas_call(
        functools.partial(_attn_kernel, ctx is not None),
        grid=(seq.n_batch, tps),
        in_specs=in_specs + [pl.BlockSpec(memory_space=pltpu.SMEM)],
        out_specs=pl.BlockSpec((tq, ATTN_WIDTH), lambda b, j: (b * tps + j, 0)),
        out_shape=jax.ShapeDtypeStruct((seq.n_rows, ATTN_WIDTH), BF16),
        scratch_shapes=[pltpu.VMEM((n_keys, cols), F32), pltpu.VMEM((n_keys, cols), F32),
                        pltpu.VMEM((n_keys, cols), BF16), pltpu.VMEM((n_keys, cols), BF16),
                        pltpu.SMEM((1,), F32)],
        compiler_params=_params(2),
        name="attention",
    )(*args, qmax)


def _odd_in_kernel(seq, x_ref, sc_ref, sh_ref, w_ref, ch_out, bg_out, f_out):
    c = CONV_WIDTH

    def pre(s):
        return (_layer_norm(x_ref[seq.sub_rows(s), :]) * (1.0 + sc_ref[...]) + sh_ref[...]).astype(BF16)

    def mid(s, u):
        return _dot(u, w_ref[...])

    def post(s, proj):
        rows = seq.sub_rows(s)
        ch_out[rows, :] = (proj[:, 2 * c:3 * c] * proj[:, :c]).astype(BF16)
        bg_out[rows, :] = proj[:, c:2 * c].astype(BF16)
        f_out[rows, :] = proj[:, 3 * c:].astype(BF16)

    _staggered(seq.n_sub, pre, mid, post)


def _odd_in(seq, layer, o, x, ada5, w_in):
    n = seq.n_rows
    return pl.pallas_call(
        functools.partial(_odd_in_kernel, seq),
        grid=(seq.n_tiles,),
        in_specs=[
            seq.tile_spec(D_MODEL),
            seq.mod_spec(layer, 1),
            seq.mod_spec(layer, 0),
            _resident((None, D_MODEL, IN_ODD), lambda i: (o, 0, 0)),
        ],
        out_specs=[seq.tile_spec(CONV_WIDTH), seq.tile_spec(CONV_WIDTH), seq.tile_spec(FOURIER_WIDTH)],
        out_shape=[jax.ShapeDtypeStruct((n, CONV_WIDTH), BF16), jax.ShapeDtypeStruct((n, CONV_WIDTH), BF16),
                   jax.ShapeDtypeStruct((n, FOURIER_WIDTH), BF16)],
        compiler_params=_params(1),
        name="odd_in",
    )(x, ada5, ada5, w_in)


def _fourier_kernel(scale, f_ref, cl_ref, sl_ref, cc_ref, sc_ref, o_ref):
    fb = f_ref[...]
    g_cos = (_dot(fb, cc_ref[...]) * scale).astype(BF16)
    g_sin = (_dot(fb, sc_ref[...]) * scale).astype(BF16)
    o_ref[...] = (_dot(cl_ref[...], g_cos) - _dot(sl_ref[...], g_sin)).astype(BF16)


def _dft_angles(rows, n):
    k = lax.iota(jnp.int32, n)[None, :]
    return ((rows[:, None] * k) % n).astype(F32) * (2.0 * math.pi / n)


def _dft_expand_kernel(hc_ref, hs_ref, lc_ref, ls_ref, c_out, s_out):
    lc, ls = lc_ref[...], ls_ref[...]
    split = lc.shape[0]
    for r in range(hc_ref.shape[0]):
        hc, hs = hc_ref[r], hs_ref[r]
        c_out[r * split:(r + 1) * split, :] = (hc * lc - hs * ls).astype(BF16)
        s_out[r * split:(r + 1) * split, :] = (hs * lc + hc * ls).astype(BF16)


def _dft_tables(n):
    split = FOURIER_GROUP
    assert n % split == 0
    ang_hi = _dft_angles(lax.iota(jnp.int32, n // split) * split, n).reshape(n // split, 1, n)
    ang_lo = _dft_angles(lax.iota(jnp.int32, split), n)
    per_step = min(DFT_HI_ROWS_PER_STEP, n // split)
    assert (n // split) % per_step == 0
    hi_spec = pl.BlockSpec((per_step, 1, n), lambda i: (i, 0, 0))
    lo_spec = pl.BlockSpec((split, n), lambda i: (0, 0))
    out_spec = pl.BlockSpec((per_step * split, n), lambda i: (i, 0))
    return pl.pallas_call(
        _dft_expand_kernel,
        grid=(n // split // per_step,),
        in_specs=[hi_spec, hi_spec, lo_spec, lo_spec],
        out_specs=[out_spec, out_spec],
        out_shape=[jax.ShapeDtypeStruct((n, n), BF16)] * 2,
        compiler_params=_params(1),
        name="dft_tables",
    )(jnp.cos(ang_hi), jnp.sin(ang_hi), jnp.cos(ang_lo), jnp.sin(ang_lo))


def _fourier(seq, f, tabs):
    L = seq.seq_len
    cl, sl, cc, sc = tabs
    scale = 1.0 / math.sqrt(L * FOURIER_GROUP)
    return pl.pallas_call(
        functools.partial(_fourier_kernel, scale),
        grid=(seq.n_batch,),
        in_specs=[
            pl.BlockSpec((L, FOURIER_WIDTH), lambda b: (b, 0)),
            _resident((L, L), lambda b: (0, 0)),
            _resident((L, L), lambda b: (0, 0)),
            _resident((FOURIER_WIDTH, FOURIER_WIDTH), lambda b: (0, 0)),
            _resident((FOURIER_WIDTH, FOURIER_WIDTH), lambda b: (0, 0)),
        ],
        out_specs=pl.BlockSpec((L, FOURIER_WIDTH), lambda b: (b, 0)),
        out_shape=jax.ShapeDtypeStruct((seq.n_rows, FOURIER_WIDTH), BF16),
        compiler_params=_params(1),
        name="fourier",
    )(f, cl, sl, cc, sc)


def _odd_mix(seq, halo, refs):
    ch_ref, chp_ref, chn_ref, bg_ref, bgp_ref, bgn_ref, fo_ref, fop_ref, fon_ref, cw_ref, cb_ref = refs
    wide = chp_ref.shape[0]
    assert wide >= halo + 1

    def mix(s, _):
        sub = seq.sub_sizes[s]
        n_w = sub + 2 * wide
        keep = slice(wide - halo, wide + sub + halo)
        ch_w = seq.widened(s, ch_ref, chp_ref, chn_ref, True)
        conv = (cb_ref[...] + pltpu.roll(ch_w, 1, 0) * cw_ref[0:1, :] + ch_w * cw_ref[1:2, :]
                + pltpu.roll(ch_w, n_w - 1, 0) * cw_ref[2:3, :])
        conv_out = (seq.widened(s, bg_ref, bgp_ref, bgn_ref, False) * conv)[keep]
        four = seq.widened(s, fo_ref, fop_ref, fon_ref, False)[keep]
        return jnp.concatenate([conv_out.astype(BF16), four.astype(BF16)], axis=1)

    return mix


def _even_mix(seq, halo, refs):
    attn_ref, attnp_ref, attnn_ref, p_ref, pp_ref, pn_ref, wp_ref, ps_ref = refs
    wide = pp_ref.shape[0]
    assert wide >= halo + max(POOL_WINDOWS) // 2

    def mix(s, _):
        sub = seq.sub_sizes[s]
        n_w = sub + 2 * wide
        keep = slice(wide - halo, wide + sub + halo)
        p_w = seq.widened(s, p_ref, pp_ref, pn_ref, True)
        pos = seq.sub_pos(s) - halo + lax.broadcasted_iota(jnp.int32, (sub + 2 * halo, 1), 0)
        mixed = []
        for gi, w in enumerate(POOL_WINDOWS):
            half = w // 2
            lanes = slice(gi * POOL_GROUP, (gi + 1) * POOL_GROUP)
            run = p_w[:, lanes]
            span = 1
            while span < w:
                run = run + pltpu.roll(run, span, 0)
                span *= 2
            if half > 1:
                run = pltpu.roll(run, n_w - (half - 1), 0)
            cnt = jnp.maximum(jnp.minimum(pos + half, seq.seq_len) - jnp.maximum(pos - half, 0), 1)
            centred = run[keep] / cnt.astype(F32) - p_w[keep, lanes]
            mixed.append(_dot(centred.astype(BF16), wp_ref[gi]))
        pool = jnp.concatenate(mixed, axis=1) * ps_ref[...]
        attn = seq.widened(s, attn_ref, attnp_ref, attnn_ref, False)[keep]
        return jnp.concatenate([attn.astype(BF16), pool.astype(BF16)], axis=1)

    return mix


N_TAIL_REFS = 18


def _mixer_ffn_kernel(seq, make_mix, *refs):
    (x_ref, xp_ref, xn_ref, g1_ref, sc2_ref, sh2_ref, g2_ref, wo_ref, lg1_ref, lb1_ref, wa_ref, wg_ref,
     fcw_ref, fcb_ref, wd_ref, lg2_ref, lb2_ref, o_ref) = refs[-N_TAIL_REFS:]
    halo = xp_ref.shape[0]
    mix = make_mix(seq, halo, refs[:-N_TAIL_REFS])

    def project(s, m_ext):
        return _dot(m_ext, wo_ref[...])

    def norms(s, y_ext):
        sub = seq.sub_sizes[s]
        x_ext = seq.widened(s, x_ref, xp_ref, xn_ref, False)
        x1_ext = _post_norm(x_ext, g1_ref[...], y_ext, lg1_ref[...], lb1_ref[...])
        u_ext = (_layer_norm(x1_ext) * (1.0 + sc2_ref[...]) + sh2_ref[...]).astype(BF16)
        return x1_ext[halo:halo + sub], u_ext

    def up(s, st):
        x1, u_ext = st
        sub = seq.sub_sizes[s]
        has_prev, has_next = seq.sub_edges(s)
        a_ext = _dot(u_ext, wa_ref[...])
        gate_lin = _dot(u_ext[halo:halo + sub], wg_ref[...])
        row = lax.broadcasted_iota(jnp.int32, (sub + 2 * halo, 1), 0)
        inside = ((row >= halo) | has_prev) & ((row < halo + sub) | has_next)
        return x1, jnp.where(inside, a_ext, 0.0), gate_lin

    def hidden(s, st):
        x1, a_ext, gate_lin = st
        n_ext = a_ext.shape[0]
        inner = slice(halo, n_ext - halo)
        conv = (fcb_ref[...] + pltpu.roll(a_ext, 1, 0)[inner] * fcw_ref[0:1, :] + a_ext[inner] * fcw_ref[1:2, :]
                + pltpu.roll(a_ext, n_ext - 1, 0)[inner] * fcw_ref[2:3, :])
        return x1, (conv * jax.nn.sigmoid(conv) * gate_lin).astype(BF16)

    def down(s, st):
        x1, h = st
        return x1, _dot(h, wd_ref[...])

    def out(s, st):
        x1, y = st
        o_ref[seq.sub_rows(s), :] = _post_norm(x1, g2_ref[...], y, lg2_ref[...], lb2_ref[...])

    _pipelined(seq.n_sub, [mix, project, norms, up, hidden, down, out])


def _mixer_ffn(seq, name, make_mix, mix_specs, mix_args, layer, x, ada5, w_out, w_up, ffn_conv_w, ffn_conv_b,
               w_down, ln_g, ln_b):
    tail_specs = [
        *seq.with_halos(D_MODEL, HALO_F32),
        seq.mod_spec(layer, 2), seq.mod_spec(layer, 4), seq.mod_spec(layer, 3), seq.mod_spec(layer, 5),
        _resident((None, D_MODEL, D_MODEL), lambda i: (layer, 0, 0)),
        _row_spec((layer, 0), D_MODEL, 2),
        _row_spec((layer, 0), D_MODEL, 2),
        _resident((None, D_MODEL, D_FF), lambda i: (layer, 0, 0)),
        _resident((None, D_MODEL, D_FF), lambda i: (layer, 0, 1)),
        pl.BlockSpec((None, 3, D_FF), lambda i: (layer, 0, 0)),
        _row_spec(layer, D_FF),
        _resident((None, D_FF, D_MODEL), lambda i: (layer, 0, 0)),
        _row_spec((layer, 1), D_MODEL, 2),
        _row_spec((layer, 1), D_MODEL, 2),
    ]
    tail_args = [x, x, x, ada5, ada5, ada5, ada5, w_out, ln_g, ln_b, w_up, w_up, ffn_conv_w, ffn_conv_b, w_down,
                 ln_g, ln_b]
    assert len(tail_specs) + 1 == N_TAIL_REFS and len(tail_args) + 1 == N_TAIL_REFS
    return pl.pallas_call(
        functools.partial(_mixer_ffn_kernel, seq, make_mix),
        grid=(seq.n_tiles,),
        in_specs=list(mix_specs) + tail_specs,
        out_specs=seq.tile_spec(D_MODEL),
        out_shape=jax.ShapeDtypeStruct((seq.n_rows, D_MODEL), F32),
        compiler_params=_params(1),
        name=name,
    )(*mix_args, *tail_args)


def _odd_ffn(seq, layer, o, ch, bg, fo, x, ada5, conv_w, conv_b, *tail):
    mix_specs = [*seq.with_halos(CONV_WIDTH, HALO_BF16), *seq.with_halos(CONV_WIDTH, HALO_BF16),
                 *seq.with_halos(FOURIER_WIDTH, HALO_BF16),
                 pl.BlockSpec((None, 3, CONV_WIDTH), lambda i: (o, 0, 0)), _row_spec(o, CONV_WIDTH)]
    mix_args = [ch, ch, ch, bg, bg, bg, fo, fo, fo, conv_w, conv_b]
    return _mixer_ffn(seq, "odd_ffn", _odd_mix, mix_specs, mix_args, layer, x, ada5, *tail)


def _even_ffn(seq, layer, e, attn, p, x, ada5, w_pool, pool_scale, *tail):
    mix_specs = [*seq.with_halos(ATTN_WIDTH, HALO_BF16), *seq.with_halos(POOL_WIDTH, HALO_BF16),
                 _resident((None, len(POOL_WINDOWS), POOL_GROUP, POOL_GROUP), lambda i: (e, 0, 0, 0)),
                 _row_spec(e, POOL_WIDTH)]
    mix_args = [attn, attn, attn, p, p, p, w_pool, pool_scale]
    return _mixer_ffn(seq, "even_ffn", _even_mix, mix_specs, mix_args, layer, x, ada5, *tail)


def _rope_tables(seq_len):
    t = lax.iota(jnp.int32, seq_len)
    row = (t // GRID_W).astype(F32)
    col = (t % GRID_W).astype(F32)
    n_freq = HEAD_DIM // 4
    inv = 1.0 / (ROPE_THETA ** (jnp.arange(n_freq, dtype=F32) / n_freq))
    ang = jnp.concatenate([row[:, None] * inv, col[:, None] * inv], -1)
    cos = jnp.repeat(jnp.cos(ang), 2, axis=1)
    sin = jnp.repeat(jnp.sin(ang), 2, axis=1) * jnp.tile(jnp.array([-1.0, 1.0], F32), HEAD_DIM // 2)
    reps = LANES // HEAD_DIM
    return jnp.tile(cos, (1, reps)), jnp.tile(sin, (1, reps))


def _fourier_tables(seq_len):
    cl, sl = _dft_tables(seq_len)
    ang = _dft_angles(lax.iota(jnp.int32, FOURIER_GROUP), FOURIER_GROUP)
    eye = jnp.eye(N_FOURIER_GROUPS, dtype=F32)
    return cl, sl, jnp.kron(eye, jnp.cos(ang)).astype(BF16), jnp.kron(eye, jnp.sin(ang)).astype(BF16)


def _run_trunk(seq, x, ada5, ctx_k, ctx_v, wts):
    (w_in_even, gq, gk, bd, w_pool, pool_scale, w_in_odd, conv_w, conv_b, w_out, w_up, ffn_conv_w,
     ffn_conv_b, w_down, ln_g, ln_b) = wts
    rope_tabs = _rope_tables(seq.seq_len) if seq.latent else None
    four_tabs = _fourier_tables(seq.seq_len)
    fused = seq.with_tile(FUSED_TILE_M)
    tail = (w_out, w_up, ffn_conv_w, ffn_conv_b, w_down, ln_g, ln_b)
    new_k, new_v = [], []
    for layer in range(DEPTH):
        if layer % 2 == 0:
            e = layer // 2
            outs = _even_in(seq, layer, e, x, ada5, w_in_even, gq, gk, bd, rope_tabs)
            q, k, vt, p = outs[:4]
            if not seq.latent:
                new_k.append(outs[4])
                new_v.append(outs[5])
            attn = _attention(seq, q, k, vt, (e, ctx_k, ctx_v) if seq.latent else None, gq[e])
            x = _even_ffn(fused, layer, e, attn, p, x, ada5, w_pool, pool_scale, *tail)
        else:
            o = layer // 2
            ch, bg, f = _odd_in(seq, layer, o, x, ada5, w_in_odd)
            fo = _fourier(seq, f, four_tabs)
            x = _odd_ffn(fused, layer, o, ch, bg, fo, x, ada5, conv_w, conv_b, *tail)
    return x, new_k, new_v


def kernel(x_prompt, x_sample, cache_k, cache_v, c, c_ctx, w_ada, b_ada, w_in_even, q_norm_g, k_norm_g,
           w_pool, pool_scale, w_in_odd, conv_w, conv_b, w_out, w_up, ffn_conv_w, ffn_conv_b, w_down,
           ln_g, ln_b):
    n_prompt, prompt_len, _ = x_prompt.shape
    n_sample, sample_len, _ = x_sample.shape
    n_even = w_in_even.shape[0]
    n_odd = w_in_odd.shape[0]
    assert n_sample <= CTX_ROW

    cond = jnp.zeros((COND_ROWS, D_MODEL), F32).at[:n_sample].set(c).at[CTX_ROW].set(c_ctx)
    ada = _ada_all(cond, w_ada, b_ada)
    ada5 = ada.reshape(DEPTH, COND_ROWS, 6, 1, D_MODEL)

    head_of = lax.iota(jnp.int32, ATTN_WIDTH) // HEAD_DIM
    bd = (head_of[:, None] == head_of[None, :]).astype(BF16)
    wts = (
        w_in_even.astype(BF16),
        jnp.tile(q_norm_g, (1, N_HEADS)).reshape(n_even, 1, ATTN_WIDTH),
        jnp.tile(k_norm_g, (1, N_KV_HEADS)).reshape(n_even, 1, KV_WIDTH),
        bd,
        w_pool.astype(BF16),
        pool_scale.reshape(n_even, 1, POOL_WIDTH),
        w_in_odd.astype(BF16),
        conv_w,
        conv_b.reshape(n_odd, 1, CONV_WIDTH),
        w_out.astype(BF16),
        w_up.astype(BF16),
        ffn_conv_w,
        ffn_conv_b.reshape(DEPTH, 1, D_FF),
        w_down.astype(BF16),
        ln_g.reshape(DEPTH, 2, 1, D_MODEL),
        ln_b.reshape(DEPTH, 2, 1, D_MODEL),
    )

    prompt = _Seq(n_prompt, prompt_len, latent=False)
    y_prompt, ks, vs = _run_trunk(prompt, x_prompt.reshape(-1, D_MODEL), ada5, None, None, wts)
    cache_shape = (n_prompt, prompt_len, N_KV_HEADS, HEAD_DIM)
    new_cache_k = jnp.stack([k.reshape(cache_shape) for k in ks], 1)
    new_cache_v = jnp.stack([v.reshape(cache_shape) for v in vs], 1)

    sample = _Seq(n_sample, sample_len, latent=True)
    past_len = cache_k.shape[2]
    ctx_k = cache_k.astype(BF16).transpose(0, 1, 3, 2, 4)
    ones = jnp.ones((n_sample, n_even, N_KV_HEADS, ATTN_ONES_ROWS, past_len), BF16)
    ctx_v = jnp.concatenate([cache_v.astype(BF16).transpose(0, 1, 3, 4, 2), ones], axis=3)
    y_sample, _, _ = _run_trunk(sample, x_sample.reshape(-1, D_MODEL), ada5, ctx_k, ctx_v, wts)

    return (y_prompt.reshape(x_prompt.shape), y_sample.reshape(x_sample.shape), new_cache_k, new_cache_v)
```

```python
import functools
import math

import jax
import jax.numpy as jnp
from jax import lax
from jax.experimental import pallas as pl
from jax.experimental.pallas import tpu as pltpu

D_MODEL = 1024
DEPTH = 4
GRID_W = 64
N_HEADS = 8
N_KV_HEADS = 2
HEAD_DIM = 64
ATTN_WIDTH = N_HEADS * HEAD_DIM
KV_WIDTH = N_KV_HEADS * HEAD_DIM
POOL_WIDTH = D_MODEL - ATTN_WIDTH
POOL_WINDOWS = (2, 4, 8, 16)
POOL_GROUP = POOL_WIDTH // len(POOL_WINDOWS)
FOURIER_WIDTH = D_MODEL // 4
N_FOURIER_GROUPS = 4
FOURIER_GROUP = FOURIER_WIDTH // N_FOURIER_GROUPS
CONV_WIDTH = D_MODEL - FOURIER_WIDTH
D_FF = 2816
ROPE_THETA = 10000.0
LN_EPS = 1e-6
RMS_EPS = 1e-6
IN_EVEN = ATTN_WIDTH + 2 * KV_WIDTH + POOL_WIDTH
IN_ODD = 3 * CONV_WIDTH + FOURIER_WIDTH
DEEPNORM_ALPHA = (2 * DEPTH) ** 0.25
Q_SCALE = HEAD_DIM ** -0.5 * math.log2(math.e)

SUBLANES = 8
LANES = 128
VMEM_LIMIT_BYTES = 56 * 1024 * 1024

TILE_M = 1024
FUSED_TILE_M = 512
SUB_M = 256
ATTN_TILE_M = 1024
ATTN_SUB_M = 256
ATTN_ONES_ROWS = 16
ATTN_HEADS_PER_DOT = 2
ATTN_KEY_CHUNK = 256
ATTN_SCORE_BOUND = 80.0
HALO_F32 = SUBLANES
HALO_BF16 = 2 * SUBLANES
DFT_HI_ROWS_PER_STEP = 4
COND_ROWS = 16
CTX_ROW = 8

F32 = jnp.float32
BF16 = jnp.bfloat16


def _params(n_axes):
    return pltpu.CompilerParams(dimension_semantics=("arbitrary",) * n_axes,
                                vmem_limit_bytes=VMEM_LIMIT_BYTES)


def _resident(block_shape, index_map):
    return pl.BlockSpec(block_shape, index_map, pipeline_mode=pl.Buffered(1))


def _dot(a, b):
    return jnp.dot(a, b, preferred_element_type=F32)


def _layer_norm(x):
    mu = jnp.mean(x, axis=-1, keepdims=True)
    xc = x - mu
    var = jnp.mean(xc * xc, axis=-1, keepdims=True)
    return xc * lax.rsqrt(var + LN_EPS)


def _post_norm(x, gate, y, g, b):
    return _layer_norm(DEEPNORM_ALPHA * x + gate * y) * g + b


def _staggered(n, pre, mid, post):
    state = pre(0)
    done = None
    for s in range(n):
        cur = mid(s, state)
        if s + 1 < n:
            state = pre(s + 1)
        if done is not None:
            post(s - 1, done)
        done = cur
    post(n - 1, done)


def _pipelined(n_sub, stages):
    state = {}
    for t in range(len(stages) + n_sub - 1):
        for s in range(n_sub):
            k = t - s
            if 0 <= k < len(stages):
                state[s] = stages[k](s, state.get(s))


def _ada_kernel(cond_ref, w_ref, b_ref, o_ref):
    cnd = cond_ref[...]
    act = (cnd * jax.nn.sigmoid(cnd)).astype(BF16)
    o_ref[...] = _dot(act, w_ref[...].astype(BF16)) + b_ref[...]


def _ada_all(cond, w_ada, b_ada):
    tn = 3072
    n_out = 6 * D_MODEL
    return pl.pallas_call(
        _ada_kernel,
        grid=(DEPTH, n_out // tn),
        in_specs=[
            pl.BlockSpec((COND_ROWS, D_MODEL), lambda l, j: (0, 0)),
            pl.BlockSpec((None, D_MODEL, tn), lambda l, j: (l, 0, j)),
            pl.BlockSpec((None, 1, tn), lambda l, j: (l, 0, j)),
        ],
        out_specs=pl.BlockSpec((None, COND_ROWS, tn), lambda l, j: (l, 0, j)),
        out_shape=jax.ShapeDtypeStruct((DEPTH, COND_ROWS, n_out), F32),
        compiler_params=_params(2),
        name="ada",
    )(cond, w_ada, b_ada.reshape(DEPTH, 1, n_out))


def _sub_sizes(tile_m, seq_len):
    unit = min(SUB_M, seq_len)
    assert tile_m % unit == 0 and seq_len % unit == 0
    return (unit,) * (tile_m // unit)


class _Seq:
    def __init__(self, n_batch, seq_len, latent, tile_m=TILE_M):
        self.n_rows = n_batch * seq_len
        assert self.n_rows % tile_m == 0
        assert seq_len % tile_m == 0 or not latent
        self.n_batch = n_batch
        self.seq_len = seq_len
        self.latent = latent
        self.tile_m = tile_m
        self.sub_sizes = _sub_sizes(tile_m, seq_len)
        self.sub_starts = tuple(sum(self.sub_sizes[:s]) for s in range(len(self.sub_sizes)))
        self.n_sub = len(self.sub_sizes)
        self.n_tiles = self.n_rows // tile_m

    def with_tile(self, tile_m):
        return _Seq(self.n_batch, self.seq_len, self.latent, tile_m)

    def cond_row(self, i):
        return (i * self.tile_m) // self.seq_len if self.latent else CTX_ROW

    def tile_spec(self, width):
        return pl.BlockSpec((self.tile_m, width), lambda i: (i, 0))

    def halo_specs(self, width, halo):
        per_tile = self.tile_m // halo
        last = self.n_rows // halo - 1
        prev = pl.BlockSpec((halo, width), lambda i: (jnp.maximum(i * per_tile - 1, 0), 0))
        nxt = pl.BlockSpec((halo, width), lambda i: (jnp.minimum((i + 1) * per_tile, last), 0))
        return prev, nxt

    def with_halos(self, width, halo):
        return [self.tile_spec(width), *self.halo_specs(width, halo)]

    def mod_spec(self, layer, which):
        return pl.BlockSpec((None, None, None, 1, D_MODEL),
                            lambda i: (layer, self.cond_row(i), which, 0, 0))

    def sub_pos(self, s):
        if self.sub_sizes[s] == self.seq_len:
            return 0
        return (pl.program_id(0) * self.tile_m + self.sub_starts[s]) % self.seq_len

    def sub_edges(self, s):
        if self.sub_sizes[s] == self.seq_len:
            return False, False
        pos = self.sub_pos(s)
        return pos > 0, pos + self.sub_sizes[s] < self.seq_len

    def sub_rows(self, s):
        return slice(self.sub_starts[s], self.sub_starts[s] + self.sub_sizes[s])

    def neighbours(self, s, ref, prev_ref, next_ref):
        halo = prev_ref.shape[0]
        start, stop = self.sub_starts[s], self.sub_starts[s] + self.sub_sizes[s]
        lo = ref[start - halo:start, :] if s > 0 else prev_ref[...]
        hi = ref[stop:stop + halo, :] if s < self.n_sub - 1 else next_ref[...]
        return lo, hi

    def widened(self, s, ref, prev_ref, next_ref, mask_edges):
        lo, hi = self.neighbours(s, ref, prev_ref, next_ref)
        lo, hi = lo.astype(F32), hi.astype(F32)
        if mask_edges:
            has_prev, has_next = self.sub_edges(s)
            lo, hi = jnp.where(has_prev, lo, 0.0), jnp.where(has_next, hi, 0.0)
        return jnp.concatenate([lo, ref[self.sub_rows(s), :].astype(F32), hi], axis=0)


def _row_spec(layer, width, n_lead=1):
    if n_lead == 1:
        return pl.BlockSpec((None, 1, width), lambda i: (layer, 0, 0))
    return pl.BlockSpec((None, None, 1, width), lambda i: (layer[0], layer[1], 0, 0))


def _even_in_kernel(seq, x_ref, sc_ref, sh_ref, w_ref, gq_ref, gk_ref, bd_ref, *rest):
    if seq.latent:
        cos_ref, sin_ref, q_out, k_out, vt_out, p_out = rest
    else:
        q_out, k_out, vt_out, p_out, kraw_out, vraw_out = rest
    o1 = ATTN_WIDTH
    o2 = o1 + KV_WIDTH
    o3 = o2 + KV_WIDTH

    def rope(t, cos, sin):
        even_lane = (lax.broadcasted_iota(jnp.int32, cos.shape, 1) & 1) == 0
        outs = []
        for j in range(t.shape[1] // LANES):
            slab = t[:, j * LANES:(j + 1) * LANES]
            partner = jnp.where(even_lane, pltpu.roll(slab, LANES - 1, 1), pltpu.roll(slab, 1, 1))
            outs.append(slab * cos + partner * sin)
        return outs[0] if len(outs) == 1 else jnp.concatenate(outs, axis=1)

    def pre(s):
        return (_layer_norm(x_ref[seq.sub_rows(s), :]) * (1.0 + sc_ref[...]) + sh_ref[...]).astype(BF16)

    def mid(s, u):
        return _dot(u, w_ref[...])

    def post(s, proj):
        rows = seq.sub_rows(s)
        q = proj[:, :o1]
        k = proj[:, o1:o2]
        v = proj[:, o2:o3]
        p_out[rows, :] = proj[:, o3:].astype(BF16)
        bd = bd_ref[...]
        q = q * lax.rsqrt(_dot((q * q).astype(BF16), bd) * (1.0 / HEAD_DIM) + RMS_EPS) * gq_ref[...]
        k = k * lax.rsqrt(_dot((k * k).astype(BF16), bd[:KV_WIDTH, :KV_WIDTH]) * (1.0 / HEAD_DIM)
                          + RMS_EPS) * gk_ref[...]
        if seq.latent:
            cos = cos_ref[rows, :]
            sin = sin_ref[rows, :]
            q = rope(q, cos, sin)
            k = rope(k, cos, sin)
        else:
            kraw_out[rows, :] = k
            vraw_out[rows, :] = v
        q_out[rows, :] = (q * Q_SCALE).astype(BF16)
        vt = v.T.astype(BF16)
        for g in range(N_KV_HEADS):
            heads = slice(g * HEAD_DIM, (g + 1) * HEAD_DIM)
            k_out[g, rows, :] = k[:, heads].astype(BF16)
            vt_out[g, :HEAD_DIM, rows] = vt[heads, :]
            vt_out[g, HEAD_DIM:, rows] = jnp.ones((ATTN_ONES_ROWS, seq.sub_sizes[s]), BF16)

    _staggered(seq.n_sub, pre, mid, post)


def _even_in(seq, layer, e, x, ada5, w_in, gq, gk, bd, rope_tabs):
    n = seq.n_rows
    in_specs = [
        seq.tile_spec(D_MODEL),
        seq.mod_spec(layer, 1),
        seq.mod_spec(layer, 0),
        _resident((None, D_MODEL, IN_EVEN), lambda i: (e, 0, 0)),
        _row_spec(e, ATTN_WIDTH),
        _row_spec(e, KV_WIDTH),
        _resident((ATTN_WIDTH, ATTN_WIDTH), lambda i: (0, 0)),
    ]
    args = [x, ada5, ada5, w_in, gq, gk, bd]
    vt_rows = HEAD_DIM + ATTN_ONES_ROWS
    out_specs = [seq.tile_spec(ATTN_WIDTH),
                 pl.BlockSpec((N_KV_HEADS, seq.tile_m, HEAD_DIM), lambda i: (0, i, 0)),
                 pl.BlockSpec((N_KV_HEADS, vt_rows, seq.tile_m), lambda i: (0, 0, i)),
                 seq.tile_spec(POOL_WIDTH)]
    out_shape = [jax.ShapeDtypeStruct((n, ATTN_WIDTH), BF16),
                 jax.ShapeDtypeStruct((N_KV_HEADS, n, HEAD_DIM), BF16),
                 jax.ShapeDtypeStruct((N_KV_HEADS, vt_rows, n), BF16),
                 jax.ShapeDtypeStruct((n, POOL_WIDTH), BF16)]
    if seq.latent:
        tiles_per_seq = seq.seq_len // seq.tile_m
        in_specs += [pl.BlockSpec((seq.tile_m, LANES), lambda i: (i % tiles_per_seq, 0))] * 2
        args += list(rope_tabs)
    else:
        out_specs += [seq.tile_spec(KV_WIDTH)] * 2
        out_shape += [jax.ShapeDtypeStruct((n, KV_WIDTH), F32)] * 2
    return pl.pallas_call(
        functools.partial(_even_in_kernel, seq),
        grid=(seq.n_tiles,),
        in_specs=in_specs,
        out_specs=out_specs,
        out_shape=out_shape,
        compiler_params=_params(1),
        name="even_in",
    )(*args)


def _reduce_rows(x, op, final, chunk=256):
    n = x.shape[0]
    if n > chunk and n % chunk == 0:
        parts = [x[i:i + chunk] for i in range(0, n, chunk)]
        while len(parts) > 1:
            parts = [op(parts[i], parts[i + 1]) if i + 1 < len(parts) else parts[i]
                     for i in range(0, len(parts), 2)]
        x = parts[0]
        n = chunk
    while n > SUBLANES and n % (2 * SUBLANES) == 0:
        n //= 2
        x = op(x[:n], x[n:])
    return final(x, axis=0, keepdims=True)


def _attn_kernel(has_ctx, q_ref, k_ref, vt_ref, *rest):
    if has_ctx:
        kc_ref, vtc_ref, qmax_ref, o_ref, st_a, st_b, p_a, p_b, kmax_ref = rest
    else:
        qmax_ref, o_ref, st_a, st_b, p_a, p_b, kmax_ref = rest
        kc_ref = vtc_ref = None
    group = N_HEADS // N_KV_HEADS
    heads_per_dot = ATTN_HEADS_PER_DOT
    tq = ATTN_SUB_M
    stages = [(r, h0) for r in range(q_ref.shape[0] // tq) for h0 in range(0, N_HEADS, heads_per_dot)]
    st_bufs = (st_a, st_b)
    p_bufs = (p_a, p_b)
    n_new = k_ref.shape[1]
    n_keys = st_a.shape[0]
    chunk = ATTN_KEY_CHUNK
    key_chunks = [(k_ref, c, c) for c in range(0, n_new, chunk)]
    if has_ctx:
        key_chunks += [(kc_ref, c, n_new + c) for c in range(0, n_keys - n_new, chunk)]

    @pl.when(pl.program_id(1) == 0)
    def _():
        kmax = None
        for ref in (k_ref, kc_ref) if has_ctx else (k_ref,):
            for g in range(N_KV_HEADS):
                kk = ref[g].astype(F32)
                part = jnp.max(jnp.sum(kk * kk, axis=-1, keepdims=True))
                kmax = part if kmax is None else jnp.maximum(kmax, part)
        kmax_ref[0] = kmax

    bounded = qmax_ref[0] * kmax_ref[0] <= ATTN_SCORE_BOUND ** 2

    def score_chunks(i):
        r, h0 = stages[i]
        qs = jnp.concatenate([q_ref[r * tq:(r + 1) * tq, h * HEAD_DIM:(h + 1) * HEAD_DIM]
                              for h in range(h0, h0 + heads_per_dot)], axis=0)
        for ref, src, dst in key_chunks:
            yield dst, lax.dot_general(ref[h0 // group, src:src + chunk, :], qs, (((1,), (1,)), ((), ())),
                                       preferred_element_type=F32)

    def scores_bounded(i):
        for dst, st in score_chunks(i):
            p_bufs[i % 2][dst:dst + chunk, :] = jnp.exp2(st).astype(BF16)

    def scores_general(i):
        for dst, st in score_chunks(i):
            st_bufs[i % 2][dst:dst + chunk, :] = st

    def softmax_general(i):
        st_ref, p_ref = st_bufs[i % 2], p_bufs[i % 2]
        m = None
        for c in range(0, n_keys, chunk):
            part = _reduce_rows(st_ref[c:c + chunk, :], jnp.maximum, jnp.max)
            m = part if m is None else jnp.maximum(m, part)
        for c in range(0, n_keys, chunk):
            p_ref[c:c + chunk, :] = jnp.exp2(st_ref[c:c + chunk, :] - m).astype(BF16)

    def run(scores, softmax):
        outs = []
        scores(0)
        for i, (r, h0) in enumerate(stages):
            if i + 1 < len(stages):
                scores(i + 1)
            softmax(i)
            g = h0 // group
            ot = _dot(vt_ref[g], p_bufs[i % 2][:n_new, :])
            if has_ctx:
                ot = ot + _dot(vtc_ref[g], p_bufs[i % 2][n_new:, :])
            ot = ot[:HEAD_DIM] / ot[HEAD_DIM:HEAD_DIM + 1]
            outs += [ot[:, j * tq:(j + 1) * tq] for j in range(heads_per_dot)]
            if h0 + heads_per_dot == N_HEADS:
                o_ref[r * tq:(r + 1) * tq, :] = jnp.concatenate(outs, axis=0).T.astype(BF16)
                outs = []

    @pl.when(bounded)
    def _():
        run(scores_bounded, lambda i: None)

    @pl.when(jnp.logical_not(bounded))
    def _():
        run(scores_general, softmax_general)


def _attention(seq, q, k, vt, ctx, q_gain):
    qmax = (jnp.max(q_gain * q_gain) * (HEAD_DIM * Q_SCALE ** 2 * 1.02)).reshape(1)
    L = seq.seq_len
    vt_rows = HEAD_DIM + ATTN_ONES_ROWS
    tq = ATTN_TILE_M if L % ATTN_TILE_M == 0 else ATTN_SUB_M
    tps = L // tq
    cols = ATTN_HEADS_PER_DOT * ATTN_SUB_M
    in_specs = [
        pl.BlockSpec((tq, ATTN_WIDTH), lambda b, j: (b * tps + j, 0)),
        pl.BlockSpec((N_KV_HEADS, L, HEAD_DIM), lambda b, j: (0, b, 0)),
        pl.BlockSpec((N_KV_HEADS, vt_rows, L), lambda b, j: (0, 0, b)),
    ]
    args = [q, k, vt]
    n_keys = L
    if ctx is not None:
        e, k_ctx, vt_ctx = ctx
        past = k_ctx.shape[3]
        n_keys += past
        in_specs += [
            pl.BlockSpec((None, None, N_KV_HEADS, past, HEAD_DIM), lambda b, j: (b, e, 0, 0, 0)),
            pl.BlockSpec((None, None, N_KV_HEADS, vt_rows, past), lambda b, j: (b, e, 0, 0, 0)),
        ]
        args += [k_ctx, vt_ctx]
    assert L % ATTN_KEY_CHUNK == 0 and n_keys % ATTN_KEY_CHUNK == 0
    return pl.pallas_call(
        functools.partial(_attn_kernel, ctx is not None),
        grid=(seq.n_batch, tps),
        in_specs=in_specs + [pl.BlockSpec(memory_space=pltpu.SMEM)],
        out_specs=pl.BlockSpec((tq, ATTN_WIDTH), lambda b, j: (b * tps + j, 0)),
        out_shape=jax.ShapeDtypeStruct((seq.n_rows, ATTN_WIDTH), BF16),
        scratch_shapes=[pltpu.VMEM((n_keys, cols), F32), pltpu.VMEM((n_keys, cols), F32),
                        pltpu.VMEM((n_keys, cols), BF16), pltpu.VMEM((n_keys, cols), BF16),
                        pltpu.SMEM((1,), F32)],
        compiler_params=_params(2),
        name="attention",
    )(*args, qmax)


def _odd_in_kernel(seq, x_ref, sc_ref, sh_ref, w_ref, ch_out, bg_out, f_out):
    c = CONV_WIDTH

    def pre(s):
        return (_layer_norm(x_ref[seq.sub_rows(s), :]) * (1.0 + sc_ref[...]) + sh_ref[...]).astype(BF16)

    def mid(s, u):
        return _dot(u, w_ref[...])

    def post(s, proj):
        rows = seq.sub_rows(s)
        ch_out[rows, :] = (proj[:, 2 * c:3 * c] * proj[:, :c]).astype(BF16)
        bg_out[rows, :] = proj[:, c:2 * c].astype(BF16)
        f_out[rows, :] = proj[:, 3 * c:].astype(BF16)

    _staggered(seq.n_sub, pre, mid, post)


def _odd_in(seq, layer, o, x, ada5, w_in):
    n = seq.n_rows
    return pl.pallas_call(
        functools.partial(_odd_in_kernel, seq),
        grid=(seq.n_tiles,),
        in_specs=[
            seq.tile_spec(D_MODEL),
            seq.mod_spec(layer, 1),
            seq.mod_spec(layer, 0),
            _resident((None, D_MODEL, IN_ODD), lambda i: (o, 0, 0)),
        ],
        out_specs=[seq.tile_spec(CONV_WIDTH), seq.tile_spec(CONV_WIDTH), seq.tile_spec(FOURIER_WIDTH)],
        out_shape=[jax.ShapeDtypeStruct((n, CONV_WIDTH), BF16), jax.ShapeDtypeStruct((n, CONV_WIDTH), BF16),
                   jax.ShapeDtypeStruct((n, FOURIER_WIDTH), BF16)],
        compiler_params=_params(1),
        name="odd_in",
    )(x, ada5, ada5, w_in)


def _fourier_kernel(scale, f_ref, cl_ref, sl_ref, cc_ref, sc_ref, o_ref):
    fb = f_ref[...]
    g_cos = (_dot(fb, cc_ref[...]) * scale).astype(BF16)
    g_sin = (_dot(fb, sc_ref[...]) * scale).astype(BF16)
    o_ref[...] = (_dot(cl_ref[...], g_cos) - _dot(sl_ref[...], g_sin)).astype(BF16)


def _dft_angles(rows, n):
    k = lax.iota(jnp.int32, n)[None, :]
    return ((rows[:, None] * k) % n).astype(F32) * (2.0 * math.pi / n)


def _dft_expand_kernel(hc_ref, hs_ref, lc_ref, ls_ref, c_out, s_out):
    lc, ls = lc_ref[...], ls_ref[...]
    split = lc.shape[0]
    for r in range(hc_ref.shape[0]):
        hc, hs = hc_ref[r], hs_ref[r]
        c_out[r * split:(r + 1) * split, :] = (hc * lc - hs * ls).astype(BF16)
        s_out[r * split:(r + 1) * split, :] = (hs * lc + hc * ls).astype(BF16)


def _dft_tables(n):
    split = FOURIER_GROUP
    assert n % split == 0
    ang_hi = _dft_angles(lax.iota(jnp.int32, n // split) * split, n).reshape(n // split, 1, n)
    ang_lo = _dft_angles(lax.iota(jnp.int32, split), n)
    per_step = min(DFT_HI_ROWS_PER_STEP, n // split)
    assert (n // split) % per_step == 0
    hi_spec = pl.BlockSpec((per_step, 1, n), lambda i: (i, 0, 0))
    lo_spec = pl.BlockSpec((split, n), lambda i: (0, 0))
    out_spec = pl.BlockSpec((per_step * split, n), lambda i: (i, 0))
    return pl.pallas_call(
        _dft_expand_kernel,
        grid=(n // split // per_step,),
        in_specs=[hi_spec, hi_spec, lo_spec, lo_spec],
        out_specs=[out_spec, out_spec],
        out_shape=[jax.ShapeDtypeStruct((n, n), BF16)] * 2,
        compiler_params=_params(1),
        name="dft_tables",
    )(jnp.cos(ang_hi), jnp.sin(ang_hi), jnp.cos(ang_lo), jnp.sin(ang_lo))


def _fourier(seq, f, tabs):
    L = seq.seq_len
    cl, sl, cc, sc = tabs
    scale = 1.0 / math.sqrt(L * FOURIER_GROUP)
    return pl.pallas_call(
        functools.partial(_fourier_kernel, scale),
        grid=(seq.n_batch,),
        in_specs=[
            pl.BlockSpec((L, FOURIER_WIDTH), lambda b: (b, 0)),
            _resident((L, L), lambda b: (0, 0)),
            _resident((L, L), lambda b: (0, 0)),
            _resident((FOURIER_WIDTH, FOURIER_WIDTH), lambda b: (0, 0)),
            _resident((FOURIER_WIDTH, FOURIER_WIDTH), lambda b: (0, 0)),
        ],
        out_specs=pl.BlockSpec((L, FOURIER_WIDTH), lambda b: (b, 0)),
        out_shape=jax.ShapeDtypeStruct((seq.n_rows, FOURIER_WIDTH), BF16),
        compiler_params=_params(1),
        name="fourier",
    )(f, cl, sl, cc, sc)


def _odd_mix(seq, halo, refs):
    ch_ref, chp_ref, chn_ref, bg_ref, bgp_ref, bgn_ref, fo_ref, fop_ref, fon_ref, cw_ref, cb_ref = refs
    wide = chp_ref.shape[0]
    assert wide >= halo + 1

    def mix(s, _):
        sub = seq.sub_sizes[s]
        n_w = sub + 2 * wide
        keep = slice(wide - halo, wide + sub + halo)
        ch_w = seq.widened(s, ch_ref, chp_ref, chn_ref, True)
        conv = (cb_ref[...] + pltpu.roll(ch_w, 1, 0) * cw_ref[0:1, :] + ch_w * cw_ref[1:2, :]
                + pltpu.roll(ch_w, n_w - 1, 0) * cw_ref[2:3, :])
        conv_out = (seq.widened(s, bg_ref, bgp_ref, bgn_ref, False) * conv)[keep]
        four = seq.widened(s, fo_ref, fop_ref, fon_ref, False)[keep]
        return jnp.concatenate([conv_out.astype(BF16), four.astype(BF16)], axis=1)

    return mix


def _even_mix(seq, halo, refs):
    attn_ref, attnp_ref, attnn_ref, p_ref, pp_ref, pn_ref, wp_ref, ps_ref = refs
    wide = pp_ref.shape[0]
    assert wide >= halo + max(POOL_WINDOWS) // 2

    def mix(s, _):
        sub = seq.sub_sizes[s]
        n_w = sub + 2 * wide
        keep = slice(wide - halo, wide + sub + halo)
        p_w = seq.widened(s, p_ref, pp_ref, pn_ref, True)
        pos = seq.sub_pos(s) - halo + lax.broadcasted_iota(jnp.int32, (sub + 2 * halo, 1), 0)
        mixed = []
        for gi, w in enumerate(POOL_WINDOWS):
            half = w // 2
            lanes = slice(gi * POOL_GROUP, (gi + 1) * POOL_GROUP)
            run = p_w[:, lanes]
            span = 1
            while span < w:
                run = run + pltpu.roll(run, span, 0)
                span *= 2
            if half > 1:
                run = pltpu.roll(run, n_w - (half - 1), 0)
            cnt = jnp.maximum(jnp.minimum(pos + half, seq.seq_len) - jnp.maximum(pos - half, 0), 1)
            centred = run[keep] / cnt.astype(F32) - p_w[keep, lanes]
            mixed.append(_dot(centred.astype(BF16), wp_ref[gi]))
        pool = jnp.concatenate(mixed, axis=1) * ps_ref[...]
        attn = seq.widened(s, attn_ref, attnp_ref, attnn_ref, False)[keep]
        return jnp.concatenate([attn.astype(BF16), pool.astype(BF16)], axis=1)

    return mix


N_TAIL_REFS = 18


def _mixer_ffn_kernel(seq, make_mix, *refs):
    (x_ref, xp_ref, xn_ref, g1_ref, sc2_ref, sh2_ref, g2_ref, wo_ref, lg1_ref, lb1_ref, wa_ref, wg_ref,
     fcw_ref, fcb_ref, wd_ref, lg2_ref, lb2_ref, o_ref) = refs[-N_TAIL_REFS:]
    halo = xp_ref.shape[0]
    mix = make_mix(seq, halo, refs[:-N_TAIL_REFS])

    def project(s, m_ext):
        return _dot(m_ext, wo_ref[...])

    def norms(s, y_ext):
        sub = seq.sub_sizes[s]
        x_ext = seq.widened(s, x_ref, xp_ref, xn_ref, False)
        x1_ext = _post_norm(x_ext, g1_ref[...], y_ext, lg1_ref[...], lb1_ref[...])
        u_ext = (_layer_norm(x1_ext) * (1.0 + sc2_ref[...]) + sh2_ref[...]).astype(BF16)
        return x1_ext[halo:halo + sub], u_ext

    def up(s, st):
        x1, u_ext = st
        sub = seq.sub_sizes[s]
        has_prev, has_next = seq.sub_edges(s)
        a_ext = _dot(u_ext, wa_ref[...])
        gate_lin = _dot(u_ext[halo:halo + sub], wg_ref[...])
        row = lax.broadcasted_iota(jnp.int32, (sub + 2 * halo, 1), 0)
        inside = ((row >= halo) | has_prev) & ((row < halo + sub) | has_next)
        return x1, jnp.where(inside, a_ext, 0.0), gate_lin

    def hidden(s, st):
        x1, a_ext, gate_lin = st
        n_ext = a_ext.shape[0]
        inner = slice(halo, n_ext - halo)
        conv = (fcb_ref[...] + pltpu.roll(a_ext, 1, 0)[inner] * fcw_ref[0:1, :] + a_ext[inner] * fcw_ref[1:2, :]
                + pltpu.roll(a_ext, n_ext - 1, 0)[inner] * fcw_ref[2:3, :])
        return x1, (conv * jax.nn.sigmoid(conv) * gate_lin).astype(BF16)

    def down(s, st):
        x1, h = st
        return x1, _dot(h, wd_ref[...])

    def out(s, st):
        x1, y = st
        o_ref[seq.sub_rows(s), :] = _post_norm(x1, g2_ref[...], y, lg2_ref[...], lb2_ref[...])

    _pipelined(seq.n_sub, [mix, project, norms, up, hidden, down, out])


def _mixer_ffn(seq, name, make_mix, mix_specs, mix_args, layer, x, ada5, w_out, w_up, ffn_conv_w, ffn_conv_b,
               w_down, ln_g, ln_b):
    tail_specs = [
        *seq.with_halos(D_MODEL, HALO_F32),
        seq.mod_spec(layer, 2), seq.mod_spec(layer, 4), seq.mod_spec(layer, 3), seq.mod_spec(layer, 5),
        _resident((None, D_MODEL, D_MODEL), lambda i: (layer, 0, 0)),
        _row_spec((layer, 0), D_MODEL, 2),
        _row_spec((layer, 0), D_MODEL, 2),
        _resident((None, D_MODEL, D_FF), lambda i: (layer, 0, 0)),
        _resident((None, D_MODEL, D_FF), lambda i: (layer, 0, 1)),
        pl.BlockSpec((None, 3, D_FF), lambda i: (layer, 0, 0)),
        _row_spec(layer, D_FF),
        _resident((None, D_FF, D_MODEL), lambda i: (layer, 0, 0)),
        _row_spec((layer, 1), D_MODEL, 2),
        _row_spec((layer, 1), D_MODEL, 2),
    ]
    tail_args = [x, x, x, ada5, ada5, ada5, ada5, w_out, ln_g, ln_b, w_up, w_up, ffn_conv_w, ffn_conv_b, w_down,
                 ln_g, ln_b]
    assert len(tail_specs) + 1 == N_TAIL_REFS and len(tail_args) + 1 == N_TAIL_REFS
    return pl.pallas_call(
        functools.partial(_mixer_ffn_kernel, seq, make_mix),
        grid=(seq.n_tiles,),
        in_specs=list(mix_specs) + tail_specs,
        out_specs=seq.tile_spec(D_MODEL),
        out_shape=jax.ShapeDtypeStruct((seq.n_rows, D_MODEL), F32),
        compiler_params=_params(1),
        name=name,
    )(*mix_args, *tail_args)


def _odd_ffn(seq, layer, o, ch, bg, fo, x, ada5, conv_w, conv_b, *tail):
    mix_specs = [*seq.with_halos(CONV_WIDTH, HALO_BF16), *seq.with_halos(CONV_WIDTH, HALO_BF16),
                 *seq.with_halos(FOURIER_WIDTH, HALO_BF16),
                 pl.BlockSpec((None, 3, CONV_WIDTH), lambda i: (o, 0, 0)), _row_spec(o, CONV_WIDTH)]
    mix_args = [ch, ch, ch, bg, bg, bg, fo, fo, fo, conv_w, conv_b]
    return _mixer_ffn(seq, "odd_ffn", _odd_mix, mix_specs, mix_args, layer, x, ada5, *tail)


def _even_ffn(seq, layer, e, attn, p, x, ada5, w_pool, pool_scale, *tail):
    mix_specs = [*seq.with_halos(ATTN_WIDTH, HALO_BF16), *seq.with_halos(POOL_WIDTH, HALO_BF16),
                 _resident((None, len(POOL_WINDOWS), POOL_GROUP, POOL_GROUP), lambda i: (e, 0, 0, 0)),
                 _row_spec(e, POOL_WIDTH)]
    mix_args = [attn, attn, attn, p, p, p, w_pool, pool_scale]
    return _mixer_ffn(seq, "even_ffn", _even_mix, mix_specs, mix_args, layer, x, ada5, *tail)


def _rope_tables(seq_len):
    t = lax.iota(jnp.int32, seq_len)
    row = (t // GRID_W).astype(F32)
    col = (t % GRID_W).astype(F32)
    n_freq = HEAD_DIM // 4
    inv = 1.0 / (ROPE_THETA ** (jnp.arange(n_freq, dtype=F32) / n_freq))
    ang = jnp.concatenate([row[:, None] * inv, col[:, None] * inv], -1)
    cos = jnp.repeat(jnp.cos(ang), 2, axis=1)
    sin = jnp.repeat(jnp.sin(ang), 2, axis=1) * jnp.tile(jnp.array([-1.0, 1.0], F32), HEAD_DIM // 2)
    reps = LANES // HEAD_DIM
    return jnp.tile(cos, (1, reps)), jnp.tile(sin, (1, reps))


def _fourier_tables(seq_len):
    cl, sl = _dft_tables(seq_len)
    ang = _dft_angles(lax.iota(jnp.int32, FOURIER_GROUP), FOURIER_GROUP)
    eye = jnp.eye(N_FOURIER_GROUPS, dtype=F32)
    return cl, sl, jnp.kron(eye, jnp.cos(ang)).astype(BF16), jnp.kron(eye, jnp.sin(ang)).astype(BF16)


def _run_trunk(seq, x, ada5, ctx_k, ctx_v, wts):
    (w_in_even, gq, gk, bd, w_pool, pool_scale, w_in_odd, conv_w, conv_b, w_out, w_up, ffn_conv_w,
     ffn_conv_b, w_down, ln_g, ln_b) = wts
    rope_tabs = _rope_tables(seq.seq_len) if seq.latent else None
    four_tabs = _fourier_tables(seq.seq_len)
    fused = seq.with_tile(FUSED_TILE_M)
    tail = (w_out, w_up, ffn_conv_w, ffn_conv_b, w_down, ln_g, ln_b)
    new_k, new_v = [], []
    for layer in range(DEPTH):
        if layer % 2 == 0:
            e = layer // 2
            outs = _even_in(seq, layer, e, x, ada5, w_in_even, gq, gk, bd, rope_tabs)
            q, k, vt, p = outs[:4]
            if not seq.latent:
                new_k.append(outs[4])
                new_v.append(outs[5])
            attn = _attention(seq, q, k, vt, (e, ctx_k, ctx_v) if seq.latent else None, gq[e])
            x = _even_ffn(fused, layer, e, attn, p, x, ada5, w_pool, pool_scale, *tail)
        else:
            o = layer // 2
            ch, bg, f = _odd_in(seq, layer, o, x, ada5, w_in_odd)
            fo = _fourier(seq, f, four_tabs)
            x = _odd_ffn(fused, layer, o, ch, bg, fo, x, ada5, conv_w, conv_b, *tail)
    return x, new_k, new_v


def kernel(x_prompt, x_sample, cache_k, cache_v, c, c_ctx, w_ada, b_ada, w_in_even, q_norm_g, k_norm_g,
           w_pool, pool_scale, w_in_odd, conv_w, conv_b, w_out, w_up, ffn_conv_w, ffn_conv_b, w_down,
           ln_g, ln_b):
    n_prompt, prompt_len, _ = x_prompt.shape
    n_sample, sample_len, _ = x_sample.shape
    n_even = w_in_even.shape[0]
    n_odd = w_in_odd.shape[0]
    assert n_sample <= CTX_ROW

    cond = jnp.zeros((COND_ROWS, D_MODEL), F32).at[:n_sample].set(c).at[CTX_ROW].set(c_ctx)
    ada = _ada_all(cond, w_ada, b_ada)
    ada5 = ada.reshape(DEPTH, COND_ROWS, 6, 1, D_MODEL)

    head_of = lax.iota(jnp.int32, ATTN_WIDTH) // HEAD_DIM
    bd = (head_of[:, None] == head_of[None, :]).astype(BF16)
    wts = (
        w_in_even.astype(BF16),
        jnp.tile(q_norm_g, (1, N_HEADS)).reshape(n_even, 1, ATTN_WIDTH),
        jnp.tile(k_norm_g, (1, N_KV_HEADS)).reshape(n_even, 1, KV_WIDTH),
        bd,
        w_pool.astype(BF16),
        pool_scale.reshape(n_even, 1, POOL_WIDTH),
        w_in_odd.astype(BF16),
        conv_w,
        conv_b.reshape(n_odd, 1, CONV_WIDTH),
        w_out.astype(BF16),
        w_up.astype(BF16),
        ffn_conv_w,
        ffn_conv_b.reshape(DEPTH, 1, D_FF),
        w_down.astype(BF16),
        ln_g.reshape(DEPTH, 2, 1, D_MODEL),
        ln_b.reshape(DEPTH, 2, 1, D_MODEL),
    )

    prompt = _Seq(n_prompt, prompt_len, latent=False)
    y_prompt, ks, vs = _run_trunk(prompt, x_prompt.reshape(-1, D_MODEL), ada5, None, None, wts)
    cache_shape = (n_prompt, prompt_len, N_KV_HEADS, HEAD_DIM)
    new_cache_k = jnp.stack([k.reshape(cache_shape) for k in ks], 1)
    new_cache_v = jnp.stack([v.reshape(cache_shape) for v in vs], 1)

    sample = _Seq(n_sample, sample_len, latent=True)
    past_len = cache_k.shape[2]
    ctx_k = cache_k.astype(BF16).transpose(0, 1, 3, 2, 4)
    ones = jnp.ones((n_sample, n_even, N_KV_HEADS, ATTN_ONES_ROWS, past_len), BF16)
    ctx_v = jnp.concatenate([cache_v.astype(BF16).transpose(0, 1, 3, 4, 2), ones], axis=3)
    y_sample, _, _ = _run_trunk(sample, x_sample.reshape(-1, D_MODEL), ada5, ctx_k, ctx_v, wts)

    return (y_prompt.reshape(x_prompt.shape), y_sample.reshape(x_sample.shape), new_cache_k, new_cache_v)
```

```python
import functools
import math

import jax
import jax.numpy as jnp
from jax import lax
from jax.experimental import pallas as pl
from jax.experimental.pallas import tpu as pltpu

D_MODEL = 1024
DEPTH = 4
GRID_W = 64
N_HEADS = 8
N_KV_HEADS = 2
HEAD_DIM = 64
ATTN_WIDTH = N_HEADS * HEAD_DIM
KV_WIDTH = N_KV_HEADS * HEAD_DIM
POOL_WIDTH = D_MODEL - ATTN_WIDTH
POOL_WINDOWS = (2, 4, 8, 16)
POOL_GROUP = POOL_WIDTH // len(POOL_WINDOWS)
FOURIER_WIDTH = D_MODEL // 4
N_FOURIER_GROUPS = 4
FOURIER_GROUP = FOURIER_WIDTH // N_FOURIER_GROUPS
CONV_WIDTH = D_MODEL - FOURIER_WIDTH
D_FF = 2816
ROPE_THETA = 10000.0
LN_EPS = 1e-6
RMS_EPS = 1e-6
IN_EVEN = ATTN_WIDTH + 2 * KV_WIDTH + POOL_WIDTH
IN_ODD = 3 * CONV_WIDTH + FOURIER_WIDTH
DEEPNORM_ALPHA = (2 * DEPTH) ** 0.25
Q_SCALE = HEAD_DIM ** -0.5 * math.log2(math.e)

SUBLANES = 8
LANES = 128
VMEM_LIMIT_BYTES = 56 * 1024 * 1024

TILE_M = 1024
FUSED_TILE_M = 512
SUB_M = 256
ATTN_TILE_M = 256
ATTN_SUB_M = 256
ATTN_ONES_ROWS = 16
ATTN_HEADS_PER_DOT = 2
ATTN_KEY_CHUNK = 256
ATTN_SCORE_BOUND = 80.0
HALO_F32 = SUBLANES
HALO_BF16 = 2 * SUBLANES
DFT_HI_ROWS_PER_STEP = 4
COND_ROWS = 16
CTX_ROW = 8

F32 = jnp.float32
BF16 = jnp.bfloat16


def _params(n_axes):
    return pltpu.CompilerParams(dimension_semantics=("arbitrary",) * n_axes,
                                vmem_limit_bytes=VMEM_LIMIT_BYTES)


def _resident(block_shape, index_map):
    return pl.BlockSpec(block_shape, index_map, pipeline_mode=pl.Buffered(1))


def _dot(a, b):
    return jnp.dot(a, b, preferred_element_type=F32)


def _layer_norm(x):
    mu = jnp.mean(x, axis=-1, keepdims=True)
    xc = x - mu
    var = jnp.mean(xc * xc, axis=-1, keepdims=True)
    return xc * lax.rsqrt(var + LN_EPS)


def _post_norm(x, gate, y, g, b):
    return _layer_norm(DEEPNORM_ALPHA * x + gate * y) * g + b


def _staggered(n, pre, mid, post):
    state = pre(0)
    done = None
    for s in range(n):
        cur = mid(s, state)
        if s + 1 < n:
            state = pre(s + 1)
        if done is not None:
            post(s - 1, done)
        done = cur
    post(n - 1, done)


def _pipelined(n_sub, stages):
    state = {}
    for t in range(len(stages) + n_sub - 1):
        for s in range(n_sub):
            k = t - s
            if 0 <= k < len(stages):
                state[s] = stages[k](s, state.get(s))


def _ada_kernel(cond_ref, w_ref, b_ref, o_ref):
    cnd = cond_ref[...]
    act = (cnd * jax.nn.sigmoid(cnd)).astype(BF16)
    o_ref[...] = _dot(act, w_ref[...].astype(BF16)) + b_ref[...]


def _ada_all(cond, w_ada, b_ada):
    tn = 3072
    n_out = 6 * D_MODEL
    return pl.pallas_call(
        _ada_kernel,
        grid=(DEPTH, n_out // tn),
        in_specs=[
            pl.BlockSpec((COND_ROWS, D_MODEL), lambda l, j: (0, 0)),
            pl.BlockSpec((None, D_MODEL, tn), lambda l, j: (l, 0, j)),
            pl.BlockSpec((None, 1, tn), lambda l, j: (l, 0, j)),
        ],
        out_specs=pl.BlockSpec((None, COND_ROWS, tn), lambda l, j: (l, 0, j)),
        out_shape=jax.ShapeDtypeStruct((DEPTH, COND_ROWS, n_out), F32),
        compiler_params=_params(2),
        name="ada",
    )(cond, w_ada, b_ada.reshape(DEPTH, 1, n_out))


def _sub_sizes(tile_m, seq_len):
    unit = min(SUB_M, seq_len)
    assert tile_m % unit == 0 and seq_len % unit == 0
    return (unit,) * (tile_m // unit)


class _Seq:
    def __init__(self, n_batch, seq_len, latent, tile_m=TILE_M):
        self.n_rows = n_batch * seq_len
        assert self.n_rows % tile_m == 0
        assert seq_len % tile_m == 0 or not latent
        self.n_batch = n_batch
        self.seq_len = seq_len
        self.latent = latent
        self.tile_m = tile_m
        self.sub_sizes = _sub_sizes(tile_m, seq_len)
        self.sub_starts = tuple(sum(self.sub_sizes[:s]) for s in range(len(self.sub_sizes)))
        self.n_sub = len(self.sub_sizes)
        self.n_tiles = self.n_rows // tile_m

    def with_tile(self, tile_m):
        return _Seq(self.n_batch, self.seq_len, self.latent, tile_m)

    def cond_row(self, i):
        return (i * self.tile_m) // self.seq_len if self.latent else CTX_ROW

    def tile_spec(self, width):
        return pl.BlockSpec((self.tile_m, width), lambda i: (i, 0))

    def halo_specs(self, width, halo):
        per_tile = self.tile_m // halo
        last = self.n_rows // halo - 1
        prev = pl.BlockSpec((halo, width), lambda i: (jnp.maximum(i * per_tile - 1, 0), 0))
        nxt = pl.BlockSpec((halo, width), lambda i: (jnp.minimum((i + 1) * per_tile, last), 0))
        return prev, nxt

    def with_halos(self, width, halo):
        return [self.tile_spec(width), *self.halo_specs(width, halo)]

    def mod_spec(self, layer, which):
        return pl.BlockSpec((None, None, None, 1, D_MODEL),
                            lambda i: (layer, self.cond_row(i), which, 0, 0))

    def sub_pos(self, s):
        if self.sub_sizes[s] == self.seq_len:
            return 0
        return (pl.program_id(0) * self.tile_m + self.sub_starts[s]) % self.seq_len

    def sub_edges(self, s):
        if self.sub_sizes[s] == self.seq_len:
            return False, False
        pos = self.sub_pos(s)
        return pos > 0, pos + self.sub_sizes[s] < self.seq_len

    def sub_rows(self, s):
        return slice(self.sub_starts[s], self.sub_starts[s] + self.sub_sizes[s])

    def neighbours(self, s, ref, prev_ref, next_ref):
        halo = prev_ref.shape[0]
        start, stop = self.sub_starts[s], self.sub_starts[s] + self.sub_sizes[s]
        lo = ref[start - halo:start, :] if s > 0 else prev_ref[...]
        hi = ref[stop:stop + halo, :] if s < self.n_sub - 1 else next_ref[...]
        return lo, hi

    def widened(self, s, ref, prev_ref, next_ref, mask_edges):
        lo, hi = self.neighbours(s, ref, prev_ref, next_ref)
        lo, hi = lo.astype(F32), hi.astype(F32)
        if mask_edges:
            has_prev, has_next = self.sub_edges(s)
            lo, hi = jnp.where(has_prev, lo, 0.0), jnp.where(has_next, hi, 0.0)
        return jnp.concatenate([lo, ref[self.sub_rows(s), :].astype(F32), hi], axis=0)


def _row_spec(layer, width, n_lead=1):
    if n_lead == 1:
        return pl.BlockSpec((None, 1, width), lambda i: (layer, 0, 0))
    return pl.BlockSpec((None, None, 1, width), lambda i: (layer[0], layer[1], 0, 0))


def _even_in_kernel(seq, x_ref, sc_ref, sh_ref, w_ref, gq_ref, gk_ref, bd_ref, *rest):
    if seq.latent:
        cos_ref, sin_ref, q_out, k_out, vt_out, p_out = rest
    else:
        q_out, k_out, vt_out, p_out, kraw_out, vraw_out = rest
    o1 = ATTN_WIDTH
    o2 = o1 + KV_WIDTH
    o3 = o2 + KV_WIDTH

    def rope(t, cos, sin):
        even_lane = (lax.broadcasted_iota(jnp.int32, cos.shape, 1) & 1) == 0
        outs = []
        for j in range(t.shape[1] // LANES):
            slab = t[:, j * LANES:(j + 1) * LANES]
            partner = jnp.where(even_lane, pltpu.roll(slab, LANES - 1, 1), pltpu.roll(slab, 1, 1))
            outs.append(slab * cos + partner * sin)
        return outs[0] if len(outs) == 1 else jnp.concatenate(outs, axis=1)

    def pre(s):
        return (_layer_norm(x_ref[seq.sub_rows(s), :]) * (1.0 + sc_ref[...]) + sh_ref[...]).astype(BF16)

    def mid(s, u):
        return _dot(u, w_ref[...])

    def post(s, proj):
        rows = seq.sub_rows(s)
        q = proj[:, :o1]
        k = proj[:, o1:o2]
        v = proj[:, o2:o3]
        p_out[rows, :] = proj[:, o3:].astype(BF16)
        bd = bd_ref[...]
        q = q * lax.rsqrt(_dot((q * q).astype(BF16), bd) * (1.0 / HEAD_DIM) + RMS_EPS) * gq_ref[...]
        k = k * lax.rsqrt(_dot((k * k).astype(BF16), bd[:KV_WIDTH, :KV_WIDTH]) * (1.0 / HEAD_DIM)
                          + RMS_EPS) * gk_ref[...]
        if seq.latent:
            cos = cos_ref[rows, :]
            sin = sin_ref[rows, :]
            q = rope(q, cos, sin)
            k = rope(k, cos, sin)
        else:
            kraw_out[rows, :] = k
            vraw_out[rows, :] = v
        q_out[rows, :] = (q * Q_SCALE).astype(BF16)
        vt = v.T.astype(BF16)
        for g in range(N_KV_HEADS):
            heads = slice(g * HEAD_DIM, (g + 1) * HEAD_DIM)
            k_out[g, rows, :] = k[:, heads].astype(BF16)
            vt_out[g, :HEAD_DIM, rows] = vt[heads, :]
            vt_out[g, HEAD_DIM:, rows] = jnp.ones((ATTN_ONES_ROWS, seq.sub_sizes[s]), BF16)

    _staggered(seq.n_sub, pre, mid, post)


def _even_in(seq, layer, e, x, ada5, w_in, gq, gk, bd, rope_tabs):
    n = seq.n_rows
    in_specs = [
        seq.tile_spec(D_MODEL),
        seq.mod_spec(layer, 1),
        seq.mod_spec(layer, 0),
        _resident((None, D_MODEL, IN_EVEN), lambda i: (e, 0, 0)),
        _row_spec(e, ATTN_WIDTH),
        _row_spec(e, KV_WIDTH),
        _resident((ATTN_WIDTH, ATTN_WIDTH), lambda i: (0, 0)),
    ]
    args = [x, ada5, ada5, w_in, gq, gk, bd]
    vt_rows = HEAD_DIM + ATTN_ONES_ROWS
    out_specs = [seq.tile_spec(ATTN_WIDTH),
                 pl.BlockSpec((N_KV_HEADS, seq.tile_m, HEAD_DIM), lambda i: (0, i, 0)),
                 pl.BlockSpec((N_KV_HEADS, vt_rows, seq.tile_m), lambda i: (0, 0, i)),
                 seq.tile_spec(POOL_WIDTH)]
    out_shape = [jax.ShapeDtypeStruct((n, ATTN_WIDTH), BF16),
                 jax.ShapeDtypeStruct((N_KV_HEADS, n, HEAD_DIM), BF16),
                 jax.ShapeDtypeStruct((N_KV_HEADS, vt_rows, n), BF16),
                 jax.ShapeDtypeStruct((n, POOL_WIDTH), BF16)]
    if seq.latent:
        tiles_per_seq = seq.seq_len // seq.tile_m
        in_specs += [pl.BlockSpec((seq.tile_m, LANES), lambda i: (i % tiles_per_seq, 0))] * 2
        args += list(rope_tabs)
    else:
        out_specs += [seq.tile_spec(KV_WIDTH)] * 2
        out_shape += [jax.ShapeDtypeStruct((n, KV_WIDTH), F32)] * 2
    return pl.pallas_call(
        functools.partial(_even_in_kernel, seq),
        grid=(seq.n_tiles,),
        in_specs=in_specs,
        out_specs=out_specs,
        out_shape=out_shape,
        compiler_params=_params(1),
        name="even_in",
    )(*args)


def _reduce_rows(x, op, final, chunk=256):
    n = x.shape[0]
    if n > chunk and n % chunk == 0:
        parts = [x[i:i + chunk] for i in range(0, n, chunk)]
        while len(parts) > 1:
            parts = [op(parts[i], parts[i + 1]) if i + 1 < len(parts) else parts[i]
                     for i in range(0, len(parts), 2)]
        x = parts[0]
        n = chunk
    while n > SUBLANES and n % (2 * SUBLANES) == 0:
        n //= 2
        x = op(x[:n], x[n:])
    return final(x, axis=0, keepdims=True)


def _attn_kernel(has_ctx, q_ref, k_ref, vt_ref, *rest):
    if has_ctx:
        kc_ref, vtc_ref, qmax_ref, o_ref, st_a, st_b, p_a, p_b, kmax_ref = rest
    else:
        qmax_ref, o_ref, st_a, st_b, p_a, p_b, kmax_ref = rest
        kc_ref = vtc_ref = None
    group = N_HEADS // N_KV_HEADS
    heads_per_dot = ATTN_HEADS_PER_DOT
    tq = ATTN_SUB_M
    stages = [(r, h0) for r in range(q_ref.shape[0] // tq) for h0 in range(0, N_HEADS, heads_per_dot)]
    st_bufs = (st_a, st_b)
    p_bufs = (p_a, p_b)
    n_new = k_ref.shape[1]
    n_keys = st_a.shape[0]
    chunk = ATTN_KEY_CHUNK
    key_chunks = [(k_ref, c, c) for c in range(0, n_new, chunk)]
    if has_ctx:
        key_chunks += [(kc_ref, c, n_new + c) for c in range(0, n_keys - n_new, chunk)]

    @pl.when(pl.program_id(1) == 0)
    def _():
        kmax = None
        for ref in (k_ref, kc_ref) if has_ctx else (k_ref,):
            for g in range(N_KV_HEADS):
                kk = ref[g].astype(F32)
                part = jnp.max(jnp.sum(kk * kk, axis=-1, keepdims=True))
                kmax = part if kmax is None else jnp.maximum(kmax, part)
        kmax_ref[0] = kmax

    bounded = qmax_ref[0] * kmax_ref[0] <= ATTN_SCORE_BOUND ** 2

    def score_chunks(i):
        r, h0 = stages[i]
        qs = jnp.concatenate([q_ref[r * tq:(r + 1) * tq, h * HEAD_DIM:(h + 1) * HEAD_DIM]
                              for h in range(h0, h0 + heads_per_dot)], axis=0)
        for ref, src, dst in key_chunks:
            yield dst, lax.dot_general(ref[h0 // group, src:src + chunk, :], qs, (((1,), (1,)), ((), ())),
                                       preferred_element_type=F32)

    def scores_bounded(i):
        for dst, st in score_chunks(i):
            p_bufs[i % 2][dst:dst + chunk, :] = jnp.exp2(st).astype(BF16)

    def scores_general(i):
        for dst, st in score_chunks(i):
            st_bufs[i % 2][dst:dst + chunk, :] = st

    def softmax_general(i):
        st_ref, p_ref = st_bufs[i % 2], p_bufs[i % 2]
        m = None
        for c in range(0, n_keys, chunk):
            part = _reduce_rows(st_ref[c:c + chunk, :], jnp.maximum, jnp.max)
            m = part if m is None else jnp.maximum(m, part)
        for c in range(0, n_keys, chunk):
            p_ref[c:c + chunk, :] = jnp.exp2(st_ref[c:c + chunk, :] - m).astype(BF16)

    def run(scores, softmax):
        outs = []
        scores(0)
        for i, (r, h0) in enumerate(stages):
            if i + 1 < len(stages):
                scores(i + 1)
            softmax(i)
            g = h0 // group
            ot = _dot(vt_ref[g], p_bufs[i % 2][:n_new, :])
            if has_ctx:
                ot = ot + _dot(vtc_ref[g], p_bufs[i % 2][n_new:, :])
            ot = ot[:HEAD_DIM] / ot[HEAD_DIM:HEAD_DIM + 1]
            outs += [ot[:, j * tq:(j + 1) * tq] for j in range(heads_per_dot)]
            if h0 + heads_per_dot == N_HEADS:
                o_ref[r * tq:(r + 1) * tq, :] = jnp.concatenate(outs, axis=0).T.astype(BF16)
                outs = []

    @pl.when(bounded)
    def _():
        run(scores_bounded, lambda i: None)

    @pl.when(jnp.logical_not(bounded))
    def _():
        run(scores_general, softmax_general)


def _attention(seq, q, k, vt, ctx, q_gain):
    qmax = (jnp.max(q_gain * q_gain) * (HEAD_DIM * Q_SCALE ** 2 * 1.02)).reshape(1)
    L = seq.seq_len
    vt_rows = HEAD_DIM + ATTN_ONES_ROWS
    tq = ATTN_TILE_M if L % ATTN_TILE_M == 0 else ATTN_SUB_M
    tps = L // tq
    cols = ATTN_HEADS_PER_DOT * ATTN_SUB_M
    in_specs = [
        pl.BlockSpec((tq, ATTN_WIDTH), lambda b, j: (b * tps + j, 0)),
        pl.BlockSpec((N_KV_HEADS, L, HEAD_DIM), lambda b, j: (0, b, 0)),
        pl.BlockSpec((N_KV_HEADS, vt_rows, L), lambda b, j: (0, 0, b)),
    ]
    args = [q, k, vt]
    n_keys = L
    if ctx is not None:
        e, k_ctx, vt_ctx = ctx
        past = k_ctx.shape[3]
        n_keys += past
        in_specs += [
            pl.BlockSpec((None, None, N_KV_HEADS, past, HEAD_DIM), lambda b, j: (b, e, 0, 0, 0)),
            pl.BlockSpec((None, None, N_KV_HEADS, vt_rows, past), lambda b, j: (b, e, 0, 0, 0)),
        ]
        args += [k_ctx, vt_ctx]
    assert L % ATTN_KEY_CHUNK == 0 and n_keys % ATTN_KEY_CHUNK == 0
    return pl.pallas_call(
        functools.partial(_attn_kernel, ctx is not None),
        grid=(seq.n_batch, tps),
        in_specs=in_specs + [pl.BlockSpec(memory_space=pltpu.SMEM)],
        out_specs=pl.BlockSpec((tq, ATTN_WIDTH), lambda b, j: (b * tps + j, 0)),
        out_shape=jax.ShapeDtypeStruct((seq.n_rows, ATTN_WIDTH), BF16),
        scratch_shapes=[pltpu.VMEM((n_keys, cols), F32), pltpu.VMEM((n_keys, cols), F32),
                        pltpu.VMEM((n_keys, cols), BF16), pltpu.VMEM((n_keys, cols), BF16),
                        pltpu.SMEM((1,), F32)],
        compiler_params=_params(2),
        name="attention",
    )(*args, qmax)


def _odd_in_kernel(seq, x_ref, sc_ref, sh_ref, w_ref, ch_out, bg_out, f_out):
    c = CONV_WIDTH

    def pre(s):
        return (_layer_norm(x_ref[seq.sub_rows(s), :]) * (1.0 + sc_ref[...]) + sh_ref[...]).astype(BF16)

    def mid(s, u):
        return _dot(u, w_ref[...])

    def post(s, proj):
        rows = seq.sub_rows(s)
        ch_out[rows, :] = (proj[:, 2 * c:3 * c] * proj[:, :c]).astype(BF16)
        bg_out[rows, :] = proj[:, c:2 * c].astype(BF16)
        f_out[rows, :] = proj[:, 3 * c:].astype(BF16)

    _staggered(seq.n_sub, pre, mid, post)


def _odd_in(seq, layer, o, x, ada5, w_in):
    n = seq.n_rows
    return pl.pallas_call(
        functools.partial(_odd_in_kernel, seq),
        grid=(seq.n_tiles,),
        in_specs=[
            seq.tile_spec(D_MODEL),
            seq.mod_spec(layer, 1),
            seq.mod_spec(layer, 0),
            _resident((None, D_MODEL, IN_ODD), lambda i: (o, 0, 0)),
        ],
        out_specs=[seq.tile_spec(CONV_WIDTH), seq.tile_spec(CONV_WIDTH), seq.tile_spec(FOURIER_WIDTH)],
        out_shape=[jax.ShapeDtypeStruct((n, CONV_WIDTH), BF16), jax.ShapeDtypeStruct((n, CONV_WIDTH), BF16),
                   jax.ShapeDtypeStruct((n, FOURIER_WIDTH), BF16)],
        compiler_params=_params(1),
        name="odd_in",
    )(x, ada5, ada5, w_in)


def _fourier_kernel(scale, f_ref, cl_ref, sl_ref, cc_ref, sc_ref, o_ref):
    fb = f_ref[...]
    g_cos = (_dot(fb, cc_ref[...]) * scale).astype(BF16)
    g_sin = (_dot(fb, sc_ref[...]) * scale).astype(BF16)
    o_ref[...] = (_dot(cl_ref[...], g_cos) - _dot(sl_ref[...], g_sin)).astype(BF16)


def _dft_angles(rows, n):
    k = lax.iota(jnp.int32, n)[None, :]
    return ((rows[:, None] * k) % n).astype(F32) * (2.0 * math.pi / n)


def _dft_expand_kernel(hc_ref, hs_ref, lc_ref, ls_ref, c_out, s_out):
    lc, ls = lc_ref[...], ls_ref[...]
    split = lc.shape[0]
    for r in range(hc_ref.shape[0]):
        hc, hs = hc_ref[r], hs_ref[r]
        c_out[r * split:(r + 1) * split, :] = (hc * lc - hs * ls).astype(BF16)
        s_out[r * split:(r + 1) * split, :] = (hs * lc + hc * ls).astype(BF16)


def _dft_tables(n):
    split = FOURIER_GROUP
    assert n % split == 0
    ang_hi = _dft_angles(lax.iota(jnp.int32, n // split) * split, n).reshape(n // split, 1, n)
    ang_lo = _dft_angles(lax.iota(jnp.int32, split), n)
    per_step = min(DFT_HI_ROWS_PER_STEP, n // split)
    assert (n // split) % per_step == 0
    hi_spec = pl.BlockSpec((per_step, 1, n), lambda i: (i, 0, 0))
    lo_spec = pl.BlockSpec((split, n), lambda i: (0, 0))
    out_spec = pl.BlockSpec((per_step * split, n), lambda i: (i, 0))
    return pl.pallas_call(
        _dft_expand_kernel,
        grid=(n // split // per_step,),
        in_specs=[hi_spec, hi_spec, lo_spec, lo_spec],
        out_specs=[out_spec, out_spec],
        out_shape=[jax.ShapeDtypeStruct((n, n), BF16)] * 2,
        compiler_params=_params(1),
        name="dft_tables",
    )(jnp.cos(ang_hi), jnp.sin(ang_hi), jnp.cos(ang_lo), jnp.sin(ang_lo))


def _fourier(seq, f, tabs):
    L = seq.seq_len
    cl, sl, cc, sc = tabs
    scale = 1.0 / math.sqrt(L * FOURIER_GROUP)
    return pl.pallas_call(
        functools.partial(_fourier_kernel, scale),
        grid=(seq.n_batch,),
        in_specs=[
            pl.BlockSpec((L, FOURIER_WIDTH), lambda b: (b, 0)),
            _resident((L, L), lambda b: (0, 0)),
            _resident((L, L), lambda b: (0, 0)),
            _resident((FOURIER_WIDTH, FOURIER_WIDTH), lambda b: (0, 0)),
            _resident((FOURIER_WIDTH, FOURIER_WIDTH), lambda b: (0, 0)),
        ],
        out_specs=pl.BlockSpec((L, FOURIER_WIDTH), lambda b: (b, 0)),
        out_shape=jax.ShapeDtypeStruct((seq.n_rows, FOURIER_WIDTH), BF16),
        compiler_params=_params(1),
        name="fourier",
    )(f, cl, sl, cc, sc)


def _odd_mix(seq, halo, refs):
    ch_ref, chp_ref, chn_ref, bg_ref, bgp_ref, bgn_ref, fo_ref, fop_ref, fon_ref, cw_ref, cb_ref = refs
    wide = chp_ref.shape[0]
    assert wide >= halo + 1

    def mix(s, _):
        sub = seq.sub_sizes[s]
        n_w = sub + 2 * wide
        keep = slice(wide - halo, wide + sub + halo)
        ch_w = seq.widened(s, ch_ref, chp_ref, chn_ref, True)
        conv = (cb_ref[...] + pltpu.roll(ch_w, 1, 0) * cw_ref[0:1, :] + ch_w * cw_ref[1:2, :]
                + pltpu.roll(ch_w, n_w - 1, 0) * cw_ref[2:3, :])
        conv_out = (seq.widened(s, bg_ref, bgp_ref, bgn_ref, False) * conv)[keep]
        four = seq.widened(s, fo_ref, fop_ref, fon_ref, False)[keep]
        return jnp.concatenate([conv_out.astype(BF16), four.astype(BF16)], axis=1)

    return mix


def _even_mix(seq, halo, refs):
    attn_ref, attnp_ref, attnn_ref, p_ref, pp_ref, pn_ref, wp_ref, ps_ref = refs
    wide = pp_ref.shape[0]
    assert wide >= halo + max(POOL_WINDOWS) // 2

    def mix(s, _):
        sub = seq.sub_sizes[s]
        n_w = sub + 2 * wide
        keep = slice(wide - halo, wide + sub + halo)
        p_w = seq.widened(s, p_ref, pp_ref, pn_ref, True)
        pos = seq.sub_pos(s) - halo + lax.broadcasted_iota(jnp.int32, (sub + 2 * halo, 1), 0)
        mixed = []
        for gi, w in enumerate(POOL_WINDOWS):
            half = w // 2
            lanes = slice(gi * POOL_GROUP, (gi + 1) * POOL_GROUP)
            run = p_w[:, lanes]
            span = 1
            while span < w:
                run = run + pltpu.roll(run, span, 0)
                span *= 2
            if half > 1:
                run = pltpu.roll(run, n_w - (half - 1), 0)
            cnt = jnp.maximum(jnp.minimum(pos + half, seq.seq_len) - jnp.maximum(pos - half, 0), 1)
            centred = run[keep] / cnt.astype(F32) - p_w[keep, lanes]
            mixed.append(_dot(centred.astype(BF16), wp_ref[gi]))
        pool = jnp.concatenate(mixed, axis=1) * ps_ref[...]
        attn = seq.widened(s, attn_ref, attnp_ref, attnn_ref, False)[keep]
        return jnp.concatenate([attn.astype(BF16), pool.astype(BF16)], axis=1)

    return mix


N_TAIL_REFS = 18


def _mixer_ffn_kernel(seq, make_mix, *refs):
    (x_ref, xp_ref, xn_ref, g1_ref, sc2_ref, sh2_ref, g2_ref, wo_ref, lg1_ref, lb1_ref, wa_ref, wg_ref,
     fcw_ref, fcb_ref, wd_ref, lg2_ref, lb2_ref, o_ref) = refs[-N_TAIL_REFS:]
    halo = xp_ref.shape[0]
    mix = make_mix(seq, halo, refs[:-N_TAIL_REFS])

    def project(s, m_ext):
        return _dot(m_ext, wo_ref[...])

    def norms(s, y_ext):
        sub = seq.sub_sizes[s]
        x_ext = seq.widened(s, x_ref, xp_ref, xn_ref, False)
        x1_ext = _post_norm(x_ext, g1_ref[...], y_ext, lg1_ref[...], lb1_ref[...])
        u_ext = (_layer_norm(x1_ext) * (1.0 + sc2_ref[...]) + sh2_ref[...]).astype(BF16)
        return x1_ext[halo:halo + sub], u_ext

    def up(s, st):
        x1, u_ext = st
        sub = seq.sub_sizes[s]
        has_prev, has_next = seq.sub_edges(s)
        a_ext = _dot(u_ext, wa_ref[...])
        gate_lin = _dot(u_ext[halo:halo + sub], wg_ref[...])
        row = lax.broadcasted_iota(jnp.int32, (sub + 2 * halo, 1), 0)
        inside = ((row >= halo) | has_prev) & ((row < halo + sub) | has_next)
        return x1, jnp.where(inside, a_ext, 0.0), gate_lin

    def hidden(s, st):
        x1, a_ext, gate_lin = st
        n_ext = a_ext.shape[0]
        inner = slice(halo, n_ext - halo)
        conv = (fcb_ref[...] + pltpu.roll(a_ext, 1, 0)[inner] * fcw_ref[0:1, :] + a_ext[inner] * fcw_ref[1:2, :]
                + pltpu.roll(a_ext, n_ext - 1, 0)[inner] * fcw_ref[2:3, :])
        return x1, (conv * jax.nn.sigmoid(conv) * gate_lin).astype(BF16)

    def down(s, st):
        x1, h = st
        return x1, _dot(h, wd_ref[...])

    def out(s, st):
        x1, y = st
        o_ref[seq.sub_rows(s), :] = _post_norm(x1, g2_ref[...], y, lg2_ref[...], lb2_ref[...])

    _pipelined(seq.n_sub, [mix, project, norms, up, hidden, down, out])


def _mixer_ffn(seq, name, make_mix, mix_specs, mix_args, layer, x, ada5, w_out, w_up, ffn_conv_w, ffn_conv_b,
               w_down, ln_g, ln_b):
    tail_specs = [
        *seq.with_halos(D_MODEL, HALO_F32),
        seq.mod_spec(layer, 2), seq.mod_spec(layer, 4), seq.mod_spec(layer, 3), seq.mod_spec(layer, 5),
        _resident((None, D_MODEL, D_MODEL), lambda i: (layer, 0, 0)),
        _row_spec((layer, 0), D_MODEL, 2),
        _row_spec((layer, 0), D_MODEL, 2),
        _resident((None, D_MODEL, D_FF), lambda i: (layer, 0, 0)),
        _resident((None, D_MODEL, D_FF), lambda i: (layer, 0, 1)),
        pl.BlockSpec((None, 3, D_FF), lambda i: (layer, 0, 0)),
        _row_spec(layer, D_FF),
        _resident((None, D_FF, D_MODEL), lambda i: (layer, 0, 0)),
        _row_spec((layer, 1), D_MODEL, 2),
        _row_spec((layer, 1), D_MODEL, 2),
    ]
    tail_args = [x, x, x, ada5, ada5, ada5, ada5, w_out, ln_g, ln_b, w_up, w_up, ffn_conv_w, ffn_conv_b, w_down,
                 ln_g, ln_b]
    assert len(tail_specs) + 1 == N_TAIL_REFS and len(tail_args) + 1 == N_TAIL_REFS
    return pl.pallas_call(
        functools.partial(_mixer_ffn_kernel, seq, make_mix),
        grid=(seq.n_tiles,),
        in_specs=list(mix_specs) + tail_specs,
        out_specs=seq.tile_spec(D_MODEL),
        out_shape=jax.ShapeDtypeStruct((seq.n_rows, D_MODEL), F32),
        compiler_params=_params(1),
        name=name,
    )(*mix_args, *tail_args)


def _odd_ffn(seq, layer, o, ch, bg, fo, x, ada5, conv_w, conv_b, *tail):
    mix_specs = [*seq.with_halos(CONV_WIDTH, HALO_BF16), *seq.with_halos(CONV_WIDTH, HALO_BF16),
                 *seq.with_halos(FOURIER_WIDTH, HALO_BF16),
                 pl.BlockSpec((None, 3, CONV_WIDTH), lambda i: (o, 0, 0)), _row_spec(o, CONV_WIDTH)]
    mix_args = [ch, ch, ch, bg, bg, bg, fo, fo, fo, conv_w, conv_b]
    return _mixer_ffn(seq, "odd_ffn", _odd_mix, mix_specs, mix_args, layer, x, ada5, *tail)


def _even_ffn(seq, layer, e, attn, p, x, ada5, w_pool, pool_scale, *tail):
    mix_specs = [*seq.with_halos(ATTN_WIDTH, HALO_BF16), *seq.with_halos(POOL_WIDTH, HALO_BF16),
                 _resident((None, len(POOL_WINDOWS), POOL_GROUP, POOL_GROUP), lambda i: (e, 0, 0, 0)),
                 _row_spec(e, POOL_WIDTH)]
    mix_args = [attn, attn, attn, p, p, p, w_pool, pool_scale]
    return _mixer_ffn(seq, "even_ffn", _even_mix, mix_specs, mix_args, layer, x, ada5, *tail)


def _rope_tables(seq_len):
    t = lax.iota(jnp.int32, seq_len)
    row = (t // GRID_W).astype(F32)
    col = (t % GRID_W).astype(F32)
    n_freq = HEAD_DIM // 4
    inv = 1.0 / (ROPE_THETA ** (jnp.arange(n_freq, dtype=F32) / n_freq))
    ang = jnp.concatenate([row[:, None] * inv, col[:, None] * inv], -1)
    cos = jnp.repeat(jnp.cos(ang), 2, axis=1)
    sin = jnp.repeat(jnp.sin(ang), 2, axis=1) * jnp.tile(jnp.array([-1.0, 1.0], F32), HEAD_DIM // 2)
    reps = LANES // HEAD_DIM
    return jnp.tile(cos, (1, reps)), jnp.tile(sin, (1, reps))


def _fourier_tables(seq_len):
    cl, sl = _dft_tables(seq_len)
    ang = _dft_angles(lax.iota(jnp.int32, FOURIER_GROUP), FOURIER_GROUP)
    eye = jnp.eye(N_FOURIER_GROUPS, dtype=F32)
    return cl, sl, jnp.kron(eye, jnp.cos(ang)).astype(BF16), jnp.kron(eye, jnp.sin(ang)).astype(BF16)


def _run_trunk(seq, x, ada5, ctx_k, ctx_v, wts):
    (w_in_even, gq, gk, bd, w_pool, pool_scale, w_in_odd, conv_w, conv_b, w_out, w_up, ffn_conv_w,
     ffn_conv_b, w_down, ln_g, ln_b) = wts
    rope_tabs = _rope_tables(seq.seq_len) if seq.latent else None
    four_tabs = _fourier_tables(seq.seq_len)
    fused = seq.with_tile(FUSED_TILE_M)
    tail = (w_out, w_up, ffn_conv_w, ffn_conv_b, w_down, ln_g, ln_b)
    new_k, new_v = [], []
    for layer in range(DEPTH):
        if layer % 2 == 0:
            e = layer // 2
            outs = _even_in(seq, layer, e, x, ada5, w_in_even, gq, gk, bd, rope_tabs)
            q, k, vt, p = outs[:4]
            if not seq.latent:
                new_k.append(outs[4])
                new_v.append(outs[5])
            attn = _attention(seq, q, k, vt, (e, ctx_k, ctx_v) if seq.latent else None, gq[e])
            x = _even_ffn(fused, layer, e, attn, p, x, ada5, w_pool, pool_scale, *tail)
        else:
            o = layer // 2
            ch, bg, f = _odd_in(seq, layer, o, x, ada5, w_in_odd)
            fo = _fourier(seq, f, four_tabs)
            x = _odd_ffn(fused, layer, o, ch, bg, fo, x, ada5, conv_w, conv_b, *tail)
    return x, new_k, new_v


def kernel(x_prompt, x_sample, cache_k, cache_v, c, c_ctx, w_ada, b_ada, w_in_even, q_norm_g, k_norm_g,
           w_pool, pool_scale, w_in_odd, conv_w, conv_b, w_out, w_up, ffn_conv_w, ffn_conv_b, w_down,
           ln_g, ln_b):
    n_prompt, prompt_len, _ = x_prompt.shape
    n_sample, sample_len, _ = x_sample.shape
    n_even = w_in_even.shape[0]
    n_odd = w_in_odd.shape[0]
    assert n_sample <= CTX_ROW

    cond = jnp.zeros((COND_ROWS, D_MODEL), F32).at[:n_sample].set(c).at[CTX_ROW].set(c_ctx)
    ada = _ada_all(cond, w_ada, b_ada)
    ada5 = ada.reshape(DEPTH, COND_ROWS, 6, 1, D_MODEL)

    head_of = lax.iota(jnp.int32, ATTN_WIDTH) // HEAD_DIM
    bd = (head_of[:, None] == head_of[None, :]).astype(BF16)
    wts = (
        w_in_even.astype(BF16),
        jnp.tile(q_norm_g, (1, N_HEADS)).reshape(n_even, 1, ATTN_WIDTH),
        jnp.tile(k_norm_g, (1, N_KV_HEADS)).reshape(n_even, 1, KV_WIDTH),
        bd,
        w_pool.astype(BF16),
        pool_scale.reshape(n_even, 1, POOL_WIDTH),
        w_in_odd.astype(BF16),
        conv_w,
        conv_b.reshape(n_odd, 1, CONV_WIDTH),
        w_out.astype(BF16),
        w_up.astype(BF16),
        ffn_conv_w,
        ffn_conv_b.reshape(DEPTH, 1, D_FF),
        w_down.astype(BF16),
        ln_g.reshape(DEPTH, 2, 1, D_MODEL),
        ln_b.reshape(DEPTH, 2, 1, D_MODEL),
    )

    prompt = _Seq(n_prompt, prompt_len, latent=False)
    y_prompt, ks, vs = _run_trunk(prompt, x_prompt.reshape(-1, D_MODEL), ada5, None, None, wts)
    cache_shape = (n_prompt, prompt_len, N_KV_HEADS, HEAD_DIM)
    new_cache_k = jnp.stack([k.reshape(cache_shape) for k in ks], 1)
    new_cache_v = jnp.stack([v.reshape(cache_shape) for v in vs], 1)

    sample = _Seq(n_sample, sample_len, latent=True)
    past_len = cache_k.shape[2]
    ctx_k = cache_k.astype(BF16).transpose(0, 1, 3, 2, 4)
    ones = jnp.ones((n_sample, n_even, N_KV_HEADS, ATTN_ONES_ROWS, past_len), BF16)
    ctx_v = jnp.concatenate([cache_v.astype(BF16).transpose(0, 1, 3, 4, 2), ones], axis=3)
    y_sample, _, _ = _run_trunk(sample, x_sample.reshape(-1, D_MODEL), ada5, ctx_k, ctx_v, wts)

    return (y_prompt.reshape(x_prompt.shape), y_sample.reshape(x_sample.shape), new_cache_k, new_cache_v)
```
